```python
import math
import jax
import jax.numpy as jnp
from jax import lax
import numpy as np

D_MODEL = 2048
BATCH = 1
SEQ = 8192
DEPTH = 2

N_META = 16
CHUNK = 64
Q_BLOCK = 128
NORM_EPS = 1e-6

A_HEADS = 8
A_HEAD_DIM = 128
A_Q_RANK = 512
A_KV_RANK = 256
IDX_HEADS = 16
IDX_DIM = 64
TOPK_MAX = 256
TOPK_FRAC = 4

B_HEADS = 8
B_HEAD_DIM = 128
CONV_K = 4

C_HEADS = 8
C_QK_DIM = 256
C_V_DIM = 512
ROT_BASE = 10000.0

D_FF = -(-8 * D_MODEL // (3 * 256)) * 256

A_WIDTH = A_HEADS * A_HEAD_DIM
B_WIDTH = B_HEADS * B_HEAD_DIM
HY_IN_SIZES = (A_Q_RANK, A_KV_RANK, IDX_DIM, IDX_HEADS, 3 * B_WIDTH, B_WIDTH, B_HEADS, B_HEADS)
HY_IN_COLS = sum(HY_IN_SIZES)
C_QK_WIDTH = C_HEADS * C_QK_DIM
C_V_WIDTH = C_HEADS * C_V_DIM
RET_IN_SIZES = (C_QK_WIDTH, C_QK_WIDTH, C_V_WIDTH, C_V_WIDTH)
RET_IN_COLS = sum(RET_IN_SIZES)
N_EVEN = (DEPTH + 1) // 2
N_ODD = DEPTH // 2

kernel_name = 'hybrid_dsa_gdn_retention_trunk'


def _split(x, sizes):
    out, start = [], 0
    for s in sizes:
        out.append(x[..., start:start + s])
        start += s
    return out


def _rms(x):
    xf = x.astype(jnp.float32)
    return xf * lax.rsqrt(jnp.mean(xf * xf, axis=-1, keepdims=True) + NORM_EPS)


def rmsnorm(x, gain):
    return (_rms(x) * gain.astype(jnp.float32)).astype(x.dtype)


def l2norm(x):
    xf = x.astype(jnp.float32)
    return (xf * lax.rsqrt(jnp.sum(xf * xf, axis=-1, keepdims=True) + NORM_EPS)).astype(x.dtype)


def _to_chunks(a):
    front = (-N_META) % CHUNK
    a = jnp.pad(a, [(0, 0), (front, 0)] + [(0, 0)] * (a.ndim - 2))
    b, tp = a.shape[:2]
    a = a.reshape((b, tp // CHUNK, CHUNK) + a.shape[2:])
    return jnp.moveaxis(a, 3, 1)


def _from_chunks(a, t):
    b, h, n, c, d = a.shape
    a = a.reshape(b, h, n * c, d)[:, :, n * c - t:]
    return jnp.moveaxis(a, 1, 2)


def _causal_conv(x, w):
    c = x.shape[-1]
    return lax.conv_general_dilated(
        x, w.reshape(CONV_K, 1, c).astype(x.dtype), window_strides=(1,),
        padding=[(CONV_K - 1, 0)], dimension_numbers=('NWC', 'WIO', 'NWC'),
        feature_group_count=c)


def _rotate_pairs(x, pos):
    d = x.shape[-1]
    inv = 1.0 / (ROT_BASE ** jnp.linspace(0.0, 1.0, d // 2, dtype=jnp.float32))
    ang = pos.astype(jnp.float32)[:, None] * inv[None, :]
    cos, sin = jnp.cos(ang)[:, None, :], jnp.sin(ang)[:, None, :]
    x1 = x[..., 0::2].astype(jnp.float32)
    x2 = x[..., 1::2].astype(jnp.float32)
    y = jnp.stack([x1 * cos - x2 * sin, x1 * sin + x2 * cos], axis=-1)
    return y.reshape(x.shape).astype(x.dtype)


def dsa_attention(c_q, c_kv, k_idx_raw, w_idx_raw, q_norm, w_uq, w_qidx, kv_norm, w_ukv,
                  kidx_norm, n_ctx):
    b, t, _ = c_q.shape
    cq = rmsnorm(c_q, q_norm)
    q = (cq @ w_uq).reshape(b, t, A_HEADS, A_HEAD_DIM)
    q_idx = (cq @ w_qidx).reshape(b, t, IDX_HEADS, IDX_DIM)
    kv = (rmsnorm(c_kv, kv_norm) @ w_ukv).reshape(b, t, A_HEADS, 2 * A_HEAD_DIM)
    k, v = kv[..., :A_HEAD_DIM], kv[..., A_HEAD_DIM:]
    k_idx = rmsnorm(k_idx_raw, kidx_norm).astype(jnp.float32)
    w_idx = w_idx_raw.astype(jnp.float32) * (IDX_HEADS * IDX_DIM) ** -0.5
    top_k = min(TOPK_MAX, n_ctx // TOPK_FRAC)
    scale = A_HEAD_DIM ** -0.5
    n_blk = -(-t // Q_BLOCK)
    pad = n_blk * Q_BLOCK - t
    key_pos = jnp.arange(t)

    def blocks(a):
        a = jnp.pad(a, [(0, 0), (0, pad)] + [(0, 0)] * (a.ndim - 2))
        return jnp.moveaxis(a.reshape((b, n_blk, Q_BLOCK) + a.shape[2:]), 1, 0)

    def one_block(args):
        blk, q_b, qi_b, wi_b = args
        q_pos = blk * Q_BLOCK + jnp.arange(Q_BLOCK)
        causal = key_pos[None, :] <= q_pos[:, None]
        dots = jax.nn.relu(jnp.einsum('bqhd,bsd->bqhs', qi_b.astype(jnp.float32), k_idx))
        score = jnp.einsum('bqh,bqhs->bqs', wi_b, dots)
        score = jnp.where(causal[None], score, -jnp.inf)
        _, sel = lax.top_k(score, top_k)
        valid = sel <= q_pos[None, :, None]
        k_sel = jax.vmap(lambda kk, ii: kk[ii])(k, sel)
        v_sel = jax.vmap(lambda vv, ii: vv[ii])(v, sel)
        logits = jnp.einsum('bqhd,bqkhd->bqhk', q_b, k_sel).astype(jnp.float32) * scale
        logits = jnp.where(valid[:, :, None, :], logits, -jnp.inf)
        p = jax.nn.softmax(logits, axis=-1).astype(v.dtype)
        return jnp.einsum('bqhk,bqkhd->bqhd', p, v_sel)

    out = lax.map(one_block, (jnp.arange(n_blk), blocks(q), blocks(q_idx), blocks(w_idx)))
    out = jnp.moveaxis(out, 0, 1).reshape(b, n_blk * Q_BLOCK, A_WIDTH)
    return out[:, :t]


def _chunk_gated_delta(q, k, v, beta, g):
    f32 = jnp.float32
    t = q.shape[1]
    qc, kc, vc = (_to_chunks(a.astype(f32)) for a in (q, k, v))
    bc = _to_chunks(beta[..., None])[..., 0]
    gcum = jnp.cumsum(_to_chunks(g[..., None])[..., 0], axis=-1)
    lower = jnp.tril(jnp.ones((CHUNK, CHUNK), bool))
    strict = jnp.tril(jnp.ones((CHUNK, CHUNK), bool), -1)
    decay = jnp.exp(jnp.where(lower, gcum[..., :, None] - gcum[..., None, :], -jnp.inf))
    kb = kc * bc[..., None]
    a_mat = jnp.where(strict, jnp.einsum('bhnid,bhnjd->bhnij', kb, kc) * decay, 0.0)
    eye = jnp.eye(CHUNK, dtype=f32)
    t_mat = lax.linalg.triangular_solve(eye + a_mat, jnp.broadcast_to(eye, a_mat.shape),
                                        left_side=True, lower=True, unit_diagonal=True)
    u = t_mat @ (vc * bc[..., None])
    w = t_mat @ (kb * jnp.exp(gcum)[..., None])
    qk = jnp.einsum('bhnid,bhnjd->bhnij', qc, kc) * decay
    qg = qc * jnp.exp(gcum)[..., None]
    kd = kc * jnp.exp(gcum[..., -1:] - gcum)[..., None]
    g_last = jnp.exp(gcum[..., -1])

    def step(state, xs):
        u_i, w_i, qg_i, qk_i, kd_i, gl_i = xs
        v_new = u_i - w_i @ state
        o_i = qg_i @ state + qk_i @ v_new
        state = state * gl_i[..., None, None] + jnp.swapaxes(kd_i, -1, -2) @ v_new
        return state, o_i

    b, h = qc.shape[:2]
    state0 = jnp.zeros((b, h, B_HEAD_DIM, B_HEAD_DIM), f32)
    xs = tuple(jnp.moveaxis(a, 2, 0) for a in (u, w, qg, qk, kd, g_last))
    _, o = lax.scan(step, state0, xs)
    return _from_chunks(jnp.moveaxis(o, 0, 2), t)


def gated_deltanet(qkv, z, beta_raw, a_raw, conv_w, a_log, dt_bias, o_norm):
    b, t, _ = qkv.shape
    f32 = jnp.float32
    qkv = jax.nn.silu(_causal_conv(qkv, conv_w))
    q, k, v = _split(qkv, (B_WIDTH, B_WIDTH, B_WIDTH))
    shp = (b, t, B_HEADS, B_HEAD_DIM)
    q = l2norm(q.reshape(shp)) * B_HEAD_DIM ** -0.5
    k = l2norm(k.reshape(shp))
    v = v.reshape(shp)
    beta = jax.nn.sigmoid(beta_raw.astype(f32))
    g = -jnp.exp(a_log.astype(f32)) * jax.nn.softplus(a_raw.astype(f32) + dt_bias.astype(f32))
    o = _chunk_gated_delta(q, k, v, beta, g)
    o = rmsnorm(o, o_norm) * jax.nn.silu(z.reshape(shp).astype(f32))
    return o.reshape(b, t, B_WIDTH).astype(qkv.dtype)


def _chunk_retention(q, k, v):
    f32 = jnp.float32
    t = q.shape[1]
    qc, kc, vc = (_to_chunks(a.astype(f32)) for a in (q, k, v))
    log_gamma = jnp.log(1.0 - 2.0 ** (-5.0 - jnp.arange(C_HEADS, dtype=f32)))
    pos = jnp.arange(CHUNK, dtype=f32)
    rel = pos[:, None] - pos[None, :]
    dmask = jnp.where(rel >= 0, jnp.exp(jnp.maximum(rel, 0.0)[None] * log_gamma[:, None, None]), 0.0)
    scores = jnp.einsum('bhnid,bhnjd->bhnij', qc, kc) * dmask[None, :, None]
    o_intra = scores @ vc
    q_decay = jnp.exp((pos + 1.0)[None, :] * log_gamma[:, None])
    k_decay = jnp.exp((CHUNK - 1.0 - pos)[None, :] * log_gamma[:, None])
    chunk_decay = jnp.exp(CHUNK * log_gamma)
    qd = qc * q_decay[None, :, None, :, None]
    kd = kc * k_decay[None, :, None, :, None]

    def step(r, xs):
        qd_i, kd_i, v_i = xs
        o_i = qd_i @ r
        r = r * chunk_decay[None, :, None, None] + jnp.swapaxes(kd_i, -1, -2) @ v_i
        return r, o_i

    b, h = qc.shape[:2]
    r0 = jnp.zeros((b, h, C_QK_DIM, C_V_DIM), f32)
    _, o_inter = lax.scan(step, r0, tuple(jnp.moveaxis(a, 2, 0) for a in (qd, kd, vc)))
    return _from_chunks(o_intra + jnp.moveaxis(o_inter, 0, 2), t)


def hybrid_mixer(h, w_in, q_norm, w_uq, w_qidx, kv_norm, w_ukv, kidx_norm,
                 conv_w, a_log, dt_bias, o_norm, w_out, n_ctx):
    c_q, c_kv, k_idx, w_idx, qkv, z, beta_raw, a_raw = _split(h @ w_in, HY_IN_SIZES)
    o_a = dsa_attention(c_q, c_kv, k_idx, w_idx, q_norm, w_uq, w_qidx, kv_norm, w_ukv,
                        kidx_norm, n_ctx)
    o_b = gated_deltanet(qkv, z, beta_raw, a_raw, conv_w, a_log, dt_bias, o_norm)
    return jnp.concatenate([o_a, o_b], axis=-1) @ w_out


def retention_mixer(h, w_in, w_out):
    b, t, _ = h.shape
    q, k, v, gate = _split(h @ w_in, RET_IN_SIZES)
    pos = jnp.arange(t)
    q = _rotate_pairs(q.reshape(b, t, C_HEADS, C_QK_DIM), pos)
    k = _rotate_pairs(k.reshape(b, t, C_HEADS, C_QK_DIM), pos) * C_QK_DIM ** -0.5
    o = _chunk_retention(q, k, v.reshape(b, t, C_HEADS, C_V_DIM))
    o = _rms(o).astype(h.dtype).reshape(b, t, C_V_WIDTH)
    return (jax.nn.silu(gate) * o) @ w_out


def swiglu(h, w_gate, w_up, w_down):
    return (jax.nn.silu(h @ w_gate) * (h @ w_up)) @ w_down


def setup_inputs(seed: int = 0) -> dict:
    key = jax.random.key(seed)
    keys = iter(jax.random.split(key, 32))
    f32 = jnp.float32

    def normal(shape, scale):
        return jax.random.normal(next(keys), shape, f32) * scale

    def gain(shape):
        return 1.0 + normal(shape, 0.02)

    ne, no = N_EVEN, N_ODD
    a_log = jnp.log(jax.random.uniform(next(keys), (ne, B_HEADS), f32, 1.0, 16.0))
    dt = jnp.exp(jax.random.uniform(next(keys), (ne, B_HEADS), f32,
                                    math.log(1e-3), math.log(1e-1)))
    return {
        'x': normal((BATCH, SEQ, D_MODEL), 1.0),
        'meta_tokens': normal((N_META, D_MODEL), 1.0),
        'mix_norm_pre': gain((DEPTH, D_MODEL)),
        'mix_norm_post': gain((DEPTH, D_MODEL)),
        'ffn_norm_pre': gain((DEPTH, D_MODEL)),
        'ffn_norm_post': gain((DEPTH, D_MODEL)),
        'hy_w_in': normal((ne, D_MODEL, HY_IN_COLS), D_MODEL ** -0.5),
        'dsa_q_norm': gain((ne, A_Q_RANK)),
        'dsa_w_uq': normal((ne, A_Q_RANK, A_WIDTH), A_Q_RANK ** -0.5),
        'dsa_w_qidx': normal((ne, A_Q_RANK, IDX_HEADS * IDX_DIM), A_Q_RANK ** -0.5),
        'dsa_kv_norm': gain((ne, A_KV_RANK)),
        'dsa_w_ukv': normal((ne, A_KV_RANK, 2 * A_WIDTH), A_KV_RANK ** -0.5),
        'dsa_kidx_norm': gain((ne, IDX_DIM)),
        'gdn_conv_w': normal((ne, CONV_K, 3 * B_WIDTH), CONV_K ** -0.5),
        'gdn_a_log': a_log,
        'gdn_dt_bias': dt + jnp.log(-jnp.expm1(-dt)),
        'gdn_o_norm': gain((ne, B_HEAD_DIM)),
        'hy_w_out': normal((ne, A_WIDTH + B_WIDTH, D_MODEL), (A_WIDTH + B_WIDTH) ** -0.5),
        'ret_w_in': normal((no, D_MODEL, RET_IN_COLS), D_MODEL ** -0.5),
        'ret_w_out': normal((no, C_V_WIDTH, D_MODEL), C_V_WIDTH ** -0.5),
        'ffn_w_gate': normal((DEPTH, D_MODEL, D_FF), D_MODEL ** -0.5),
        'ffn_w_up': normal((DEPTH, D_MODEL, D_FF), D_MODEL ** -0.5),
        'ffn_w_down': normal((DEPTH, D_FF, D_MODEL), D_FF ** -0.5),
    }


def reference(x, meta_tokens, mix_norm_pre, mix_norm_post, ffn_norm_pre, ffn_norm_post,
              hy_w_in, dsa_q_norm, dsa_w_uq, dsa_w_qidx, dsa_kv_norm, dsa_w_ukv, dsa_kidx_norm,
              gdn_conv_w, gdn_a_log, gdn_dt_bias, gdn_o_norm, hy_w_out,
              ret_w_in, ret_w_out, ffn_w_gate, ffn_w_up, ffn_w_down):
    b, seq, _ = x.shape
    meta = jnp.broadcast_to(meta_tokens[None].astype(x.dtype), (b, N_META, D_MODEL))
    h = jnp.concatenate([meta, x], axis=1)
    for layer in range(DEPTH):
        i = layer // 2
        u = rmsnorm(h, mix_norm_pre[layer])
        if layer % 2 == 0:
            u = hybrid_mixer(u, hy_w_in[i], dsa_q_norm[i], dsa_w_uq[i], dsa_w_qidx[i],
                             dsa_kv_norm[i], dsa_w_ukv[i], dsa_kidx_norm[i], gdn_conv_w[i],
                             gdn_a_log[i], gdn_dt_bias[i], gdn_o_norm[i], hy_w_out[i], seq)
        else:
            u = retention_mixer(u, ret_w_in[i], ret_w_out[i])
        h = h + rmsnorm(u, mix_norm_post[layer])
        u = swiglu(rmsnorm(h, ffn_norm_pre[layer]), ffn_w_gate[layer], ffn_w_up[layer],
                   ffn_w_down[layer])
        h = h + rmsnorm(u, ffn_norm_post[layer])
    return h[:, N_META:]
```

```python
import functools
import math

import jax
import jax.numpy as jnp
from jax import lax
from jax.experimental import pallas as pl
from jax.experimental.pallas import tpu as pltpu

F32 = jnp.float32
BF16 = jnp.bfloat16

N_META = 16
NORM_EPS = 1e-6

A_HEADS = 8
A_HEAD_DIM = 128
A_Q_RANK = 512
A_KV_RANK = 256
IDX_HEADS = 16
IDX_DIM = 64
TOPK_MAX = 256
TOPK_FRAC = 4

B_HEADS = 8
B_HEAD_DIM = 128
CONV_K = 4
GDN_CHUNK = 128
DSA_HEADS_PER_STEP = 2
DSA_SUB_BLOCK = 64

C_HEADS = 8
C_QK_DIM = 256
C_V_DIM = 512
ROT_BASE = 10000.0
RET_CHUNK = 128

LANES = 128
SUBLANES = 8
MXU_DIM = 256
ROW_ALIGN = MXU_DIM
ROW_TILE = 3 * MXU_DIM
VMEM_LIMIT = 48 * 1024 * 1024

SM_WIDX = IDX_DIM
SM_BETA = SM_WIDX + IDX_HEADS
SM_A = SM_BETA + B_HEADS

INT_MIN = -2 ** 31
KEY_FLT_MAX = 0x7F7FFFFF
KEY_FLT_LOWEST = (0xFF7FFFFF ^ 0x7FFFFFFF) - 2 ** 32


def _params(*sem):
    return pltpu.CompilerParams(dimension_semantics=sem, vmem_limit_bytes=VMEM_LIMIT)


def _hilo(x):
    hi = x.astype(BF16)
    return hi, (x - hi.astype(F32)).astype(BF16)


def _silu(x):
    return x * jax.nn.sigmoid(x)


def _mm_kernel(*refs, na, nb, nk):
    a, b = refs[:na], refs[na:na + nb]
    o_ref, acc_ref = refs[na + nb], refs[na + nb + 1]
    k = pl.program_id(2)
    a_hi, b_hi = a[0][...], b[0][...]
    part = jnp.dot(a_hi, b_hi, preferred_element_type=F32)
    if na == 2:
        part += jnp.dot(a_hi, b[1][...], preferred_element_type=F32)
        part += jnp.dot(a[1][...], b_hi, preferred_element_type=F32)

    @pl.when(k == 0)
    def _():
        acc_ref[...] = part

    @pl.when(k > 0)
    def _():
        acc_ref[...] += part

    @pl.when(k == nk - 1)
    def _():
        o_ref[...] = acc_ref[...].astype(o_ref.dtype)


def _matmul(name, a_parts, b_parts, out_dtype, *, tm, tn, tk=None):
    m, kdim = a_parts[0].shape
    n = b_parts[0].shape[1]
    tk = kdim if tk is None else tk
    assert m % tm == 0 and n % tn == 0 and kdim % tk == 0, (m, n, kdim, tm, tn, tk)
    assert len(a_parts) == len(b_parts)
    nk = kdim // tk
    a_spec = pl.BlockSpec((tm, tk), lambda i, j, k: (i, k))
    b_spec = pl.BlockSpec((tk, tn), lambda i, j, k: (k, j))
    return pl.pallas_call(
        functools.partial(_mm_kernel, na=len(a_parts), nb=len(b_parts), nk=nk),
        name=name,
        out_shape=jax.ShapeDtypeStruct((m, n), out_dtype),
        grid=(m // tm, n // tn, nk),
        in_specs=[a_spec] * len(a_parts) + [b_spec] * len(b_parts),
        out_specs=pl.BlockSpec((tm, tn), lambda i, j, k: (i, j)),
        scratch_shapes=[pltpu.VMEM((tm, tn), F32)],
        compiler_params=_params("parallel", "parallel", "arbitrary"),
    )(*a_parts, *b_parts)


def _ffn_act_kernel(x_ref, wg_ref, wu_ref, o_ref):
    x = x_ref[...]
    g = jnp.dot(x, wg_ref[...], preferred_element_type=F32)
    u = jnp.dot(x, wu_ref[...], preferred_element_type=F32)
    o_ref[...] = (_silu(g) * u).astype(o_ref.dtype)


def _ffn_act(x, wg, wu, *, tm, tn):
    m, kdim = x.shape
    n = wg.shape[1]
    assert m % tm == 0 and n % tn == 0
    w_spec = pl.BlockSpec((kdim, tn), lambda i, j: (0, j))
    return pl.pallas_call(
        _ffn_act_kernel,
        name="ffn_act",
        out_shape=jax.ShapeDtypeStruct((m, n), BF16),
        grid=(m // tm, n // tn),
        in_specs=[pl.BlockSpec((tm, kdim), lambda i, j: (i, 0)), w_spec, w_spec],
        out_specs=pl.BlockSpec((tm, tn), lambda i, j: (i, j)),
        compiler_params=_params("parallel", "parallel"),
    )(x, wg, wu)


def _rms_rows(x):
    return x * lax.rsqrt(jnp.mean(x * x, axis=-1, keepdims=True) + NORM_EPS)


def _rms_kernel(x_ref, g_ref, *o_refs):
    y = _rms_rows(x_ref[...].astype(F32)) * g_ref[...]
    hi = y.astype(BF16)
    o_refs[0][...] = hi
    if len(o_refs) == 2:
        o_refs[1][...] = (y - hi.astype(F32)).astype(BF16)


def _rms(x, gain, *, width, col_block, tm, lo=False):
    m = x.shape[0]
    n_out = 2 if lo else 1
    out = pl.pallas_call(
        _rms_kernel,
        name="rms",
        out_shape=[jax.ShapeDtypeStruct((m, width), BF16)] * n_out,
        grid=(m // tm,),
        in_specs=[pl.BlockSpec((tm, width), lambda i: (i, col_block)),
                  pl.BlockSpec((1, width), lambda i: (0, 0))],
        out_specs=[pl.BlockSpec((tm, width), lambda i: (i, 0))] * n_out,
        compiler_params=_params("parallel"),
    )(x, gain.reshape(1, width).astype(F32))
    return out if lo else out[0]


def _rms_t_kernel(x_ref, g_ref, *o_refs):
    x = x_ref[...]
    y = x * lax.rsqrt(jnp.mean(x * x, axis=0, keepdims=True) + NORM_EPS) * g_ref[...]
    hi = y.astype(BF16)
    o_refs[0][...] = hi
    if len(o_refs) == 2:
        o_refs[1][...] = (y - hi.astype(F32)).astype(BF16)


def _rms_t(xt, gain, *, width, row_block, tc, lo=False):
    t = xt.shape[1]
    n_out = 2 if lo else 1
    out = pl.pallas_call(
        _rms_t_kernel,
        name="rms_t",
        out_shape=[jax.ShapeDtypeStruct((width, t), BF16)] * n_out,
        grid=(t // tc,),
        in_specs=[pl.BlockSpec((width, tc), lambda i: (row_block, i)),
                  pl.BlockSpec((width, 1), lambda i: (0, 0))],
        out_specs=[pl.BlockSpec((width, tc), lambda i: (0, i))] * n_out,
        compiler_params=_params("parallel"),
    )(xt, gain.reshape(width, 1).astype(F32))
    return out if lo else out[0]


def _resid_kernel(h_ref, u_ref, gp_ref, *rest, with_next):
    hn = h_ref[...] + _rms_rows(u_ref[...]) * gp_ref[...]
    if with_next:
        gn_ref, hn_ref, un_ref = rest
        un_ref[...] = (_rms_rows(hn) * gn_ref[...]).astype(BF16)
    else:
        (hn_ref,) = rest
    hn_ref[...] = hn


def _resid_norm(h, u, g_post, g_next, *, tm):
    m, d = h.shape
    row = pl.BlockSpec((tm, d), lambda i: (i, 0))
    vec = pl.BlockSpec((1, d), lambda i: (0, 0))
    with_next = g_next is not None
    gains = [g_post.reshape(1, d).astype(F32)]
    out_shape = [jax.ShapeDtypeStruct((m, d), F32)]
    if with_next:
        gains.append(g_next.reshape(1, d).astype(F32))
        out_shape.append(jax.ShapeDtypeStruct((m, d), BF16))
    out = pl.pallas_call(
        functools.partial(_resid_kernel, with_next=with_next),
        name="resid_norm",
        out_shape=out_shape,
        grid=(m // tm,),
        in_specs=[row, row] + [vec] * len(gains),
        out_specs=[row] * len(out_shape),
        compiler_params=_params("parallel"),
    )(h, u, *gains)
    return out if with_next else out[0]


def _kidx_kernel(sm_ref, g_ref, o_ref):
    sm = sm_ref[...]
    lane = lax.broadcasted_iota(jnp.int32, sm.shape, 1)
    x = jnp.where(lane < IDX_DIM, sm, 0.0)
    ms = jnp.sum(x * x, axis=-1, keepdims=True) * (1.0 / IDX_DIM)
    y = x * lax.rsqrt(ms + NORM_EPS) * g_ref[...]
    hi = y.astype(BF16)
    lo = (y - hi.astype(F32)).astype(BF16)
    y2 = y + pltpu.roll(y, IDX_DIM, axis=1)
    o_ref[:, 0:LANES] = y2.astype(BF16)
    o_ref[:, LANES:2 * LANES] = lo


def _kidx(p1, gain, *, tm):
    m = p1.shape[0]
    g = jnp.zeros((1, LANES), F32).at[0, :IDX_DIM].set(gain.astype(F32))
    return pl.pallas_call(
        _kidx_kernel,
        name="dsa_kidx",
        out_shape=jax.ShapeDtypeStruct((m, 2 * LANES), BF16),
        grid=(m // tm,),
        in_specs=[pl.BlockSpec((tm, LANES), lambda i: (i, A_Q_RANK // LANES)),
                  pl.BlockSpec((1, LANES), lambda i: (0, 0))],
        out_specs=pl.BlockSpec((tm, 2 * LANES), lambda i: (i, 0)),
        compiler_params=_params("parallel"),
    )(p1, g)


def _dsa_kernel(qt_ref, qi3t_ref, wt_ref, k3_ref, k_ref, vt_ref, o_ref, sc_scr, acc_scr, sa_scr,
                sb_scr, pa_scr, pb_scr, *, tq, tk, sb, top_k, pad, hps, w_scale, pos_bits):
    qi = pl.program_id(0)
    hd = pl.program_id(1)
    q0 = qi * tq
    nkt = (q0 + tq + tk - 1) // tk

    krel = lax.broadcasted_iota(jnp.int32, (tk, tq), 0)

    tp = k_ref.shape[0]

    @pl.when(hd == 0)
    def _select():
        sc_scr[tp:tp + tk, :] = jnp.full((tk, tq), -jnp.inf, F32)
        qpos = q0 + lax.broadcasted_iota(jnp.int32, (tk, tq), 1)

        def score_tile(kt, carry):
            ks = pl.multiple_of(kt * tk, tk)
            k3 = k3_ref[pl.ds(ks, tk), :]
            acc = jnp.zeros((tk, tq), F32)
            for ih in range(IDX_HEADS):
                d = jnp.dot(k3, qi3t_ref[ih * MXU_DIM:(ih + 1) * MXU_DIM, :],
                            preferred_element_type=F32)
                acc = acc + (wt_ref[ih:ih + 1, :] * w_scale) * jnp.maximum(d, 0.0)
            kpos = ks + krel
            ok = (kpos <= qpos) & (kpos >= pad)
            sc_scr[pl.ds(ks, tk), :] = jnp.where(ok, acc, -jnp.inf)
            return carry

        lax.fori_loop(0, nkt, score_tile, 0)

        def count(pred):
            def body(kt, acc):
                ks = pl.multiple_of(kt * tk, tk)
                hit = pred(sc_scr[pl.ds(ks, tk), :], ks).astype(jnp.int32)
                return acc + jnp.sum(hit.reshape(tk // SUBLANES, SUBLANES, tq), axis=0)

            acc = lax.fori_loop(0, nkt, body, jnp.zeros((SUBLANES, tq), jnp.int32))
            return jnp.sum(acc, axis=0, keepdims=True)

        def key_to_f32(t):
            t = jnp.clip(t, jnp.int32(KEY_FLT_LOWEST), jnp.int32(KEY_FLT_MAX))
            return pltpu.bitcast(jnp.where(t >= 0, t, t ^ jnp.int32(0x7FFFFFFF)), F32)

        zero = jnp.zeros((1, tq), jnp.int32)
        t = jnp.where(count(lambda s, ks: s >= 0.0) >= top_k, zero, jnp.int32(INT_MIN))

        def bit_body(i, t):
            cand = t + lax.shift_left(jnp.int32(1), 30 - i)
            cand_f = key_to_f32(cand)
            ok = (count(lambda s, ks: s >= cand_f) >= top_k) & (cand <= jnp.int32(KEY_FLT_MAX))
            return jnp.where(ok, cand, t)

        t = lax.fori_loop(0, 31, bit_body, t)
        thr = key_to_f32(t)

        n_ge = count(lambda s, ks: s >= thr)

        @pl.when(jnp.max(n_ge) > top_k)
        def _ties():
            thr_next = key_to_f32(t + 1)
            tie = lambda s: (s >= thr) & jnp.logical_not(s >= thr_next)
            want = top_k - count(lambda s, ks: s >= thr_next)

            def pos_body(i, cut):
                cand = cut + lax.shift_left(jnp.int32(1), pos_bits - 1 - i)
                n = count(lambda s, ks: tie(s) & (ks + krel < cand))
                return jnp.where(n < want, cand, cut)

            cut = lax.fori_loop(0, pos_bits, pos_body, zero)

            def drop_tile(kt, carry):
                ks = pl.multiple_of(kt * tk, tk)
                s = sc_scr[pl.ds(ks, tk), :]
                sc_scr[pl.ds(ks, tk), :] = jnp.where(tie(s) & (ks + krel > cut), -jnp.inf, s)
                return carry

            lax.fori_loop(0, nkt, drop_tile, 0)

        def bias_tile(kt, carry):
            ks = pl.multiple_of(kt * tk, tk)
            sc_scr[pl.ds(ks, tk), :] = jnp.where(sc_scr[pl.ds(ks, tk), :] >= thr, 0.0, -jnp.inf)
            return carry

        lax.fori_loop(0, nkt, bias_tile, 0)

    hd_sl = [slice(j * A_HEAD_DIM, (j + 1) * A_HEAD_DIM) for j in range(hps)]
    q = [qt_ref[sl, :] for sl in hd_sl]

    def logits_into(kt, s_ref):
        ks = pl.multiple_of(kt * tk, tk)
        for j in range(hps):
            s_ref[j] = jnp.dot(k_ref[pl.ds(ks, tk), hd_sl[j]], q[j], preferred_element_type=F32)

    fold = lambda x: x.reshape(sb // SUBLANES, SUBLANES, tq)

    def att_tile(kt, kt_next, bias_row, s_cur, s_next, p_ref, carry):
        logits_into(kt_next, s_next)
        ks = pl.multiple_of(kt * tk, tk)

        def masked(j, i):
            r = i * sb
            return s_cur[j, r:r + sb, :] + sc_scr[pl.ds(bias_row + r, sb), :]

        out, alphas = [], []
        for j in range(hps):
            m, l = carry[j]
            mx = jnp.full((SUBLANES, tq), -jnp.inf, F32)
            for i in range(tk // sb):
                mx = jnp.maximum(mx, jnp.max(fold(masked(j, i)), axis=0))
            m_new = jnp.maximum(m, jnp.max(mx, axis=0, keepdims=True))
            m_safe = jnp.where(m_new == -jnp.inf, 0.0, m_new)
            ls = jnp.zeros((SUBLANES, tq), F32)
            for i in range(tk // sb):
                p = jnp.exp2(masked(j, i) - m_safe)
                p_ref[j, i * sb:(i + 1) * sb, :] = p.astype(BF16)
                ls = ls + jnp.sum(fold(p), axis=0)
            alpha = jnp.exp2(m - m_safe)
            out.append((m_new, alpha * l + jnp.sum(ls, axis=0, keepdims=True)))
            alphas.append(alpha)
        pv = [jnp.dot(vt_ref[hd_sl[j], pl.ds(ks, tk)], p_ref[j], preferred_element_type=F32)
              for j in range(hps)]
        for j in range(hps):
            acc_scr[j] = alphas[j] * acc_scr[j] + pv[j]
        return tuple(out)

    def att_pair(i, carry):
        kt0 = 2 * i
        kt1 = jnp.minimum(kt0 + 1, nkt - 1)
        kt2 = jnp.minimum(kt0 + 2, nkt - 1)
        row1 = pl.multiple_of(jnp.where(kt0 + 1 < nkt, kt1 * tk, tp), tk)
        carry = att_tile(kt0, kt1, pl.multiple_of(kt0 * tk, tk), sa_scr, sb_scr, pa_scr, carry)
        return att_tile(kt1, kt2, row1, sb_scr, sa_scr, pb_scr, carry)

    acc_scr[...] = jnp.zeros_like(acc_scr)
    logits_into(0, sa_scr)
    init = (jnp.full((1, tq), -jnp.inf, F32), jnp.zeros((1, tq), F32))
    res = lax.fori_loop(0, (nkt + 1) // 2, att_pair, (init,) * hps)
    for j in range(hps):
        l = res[j][1]
        o_ref[hd_sl[j], :] = jnp.where(l > 0.0, acc_scr[j] / l, 0.0).astype(o_ref.dtype)


def _dsa(qt, qi3t, wt, k3, k, vt, *, top_k, pad, tq, tk):
    tp = k.shape[0]
    assert tp % tq == 0 and tp % tk == 0 and tp >= top_k
    hps = DSA_HEADS_PER_STEP
    hw = hps * A_HEAD_DIM
    kern = functools.partial(
        _dsa_kernel, tq=tq, tk=tk, sb=DSA_SUB_BLOCK, top_k=top_k, pad=pad, hps=hps,
        w_scale=(IDX_HEADS * IDX_DIM) ** -0.5, pos_bits=tp.bit_length())
    return pl.pallas_call(
        kern,
        name="dsa",
        out_shape=jax.ShapeDtypeStruct((A_HEADS * A_HEAD_DIM, tp), BF16),
        grid=(tp // tq, A_HEADS // hps),
        in_specs=[
            pl.BlockSpec((hw, tq), lambda i, h: (h, i)),
            pl.BlockSpec((IDX_HEADS * MXU_DIM, tq), lambda i, h: (0, i)),
            pl.BlockSpec((IDX_HEADS, tq), lambda i, h: (0, i)),
            pl.BlockSpec((tp, MXU_DIM), lambda i, h: (0, 0)),
            pl.BlockSpec((tp, hw), lambda i, h: (0, h)),
            pl.BlockSpec((hw, tp), lambda i, h: (h, 0)),
        ],
        out_specs=pl.BlockSpec((hw, tq), lambda i, h: (h, i)),
        scratch_shapes=[pltpu.VMEM((tp + tk, tq), F32), pltpu.VMEM((hps, A_HEAD_DIM, tq), F32)]
        + [pltpu.VMEM((hps, tk, tq), F32)] * 2 + [pltpu.VMEM((hps, tk, tq), BF16)] * 2,
        compiler_params=_params("arbitrary", "arbitrary"),
    )(qt, qi3t, wt, k3, k, vt)


def _dot3(a, b, dims=(((1,), (0,)), ((), ()))):
    ah, al = _hilo(a)
    bh, bl = _hilo(b)
    f = lambda x, y: lax.dot_general(x, y, dims, preferred_element_type=F32)
    return f(ah, bh) + f(ah, bl) + f(al, bh)


_NN = (((1,), (0,)), ((), ()))
_NT = (((1,), (1,)), ((), ()))
_TN = (((0,), (0,)), ((), ()))


def _mm(a, b, dims=_NN):
    return lax.dot_general(a.astype(BF16), b.astype(BF16), dims, preferred_element_type=F32)


def _gdn_kernel(qkv_ref, z_ref, sm_ref, convw_ref, alog_ref, dtb_ref, onorm_ref, o_ref,
                xbuf, s_scr, *, pad):
    c = pl.program_id(0)
    ch = GDN_CHUNK
    hist = SUBLANES

    @pl.when(c == 0)
    def _():
        xbuf[0:hist, :] = jnp.zeros((hist, xbuf.shape[1]), F32)
        s_scr[...] = jnp.zeros_like(s_scr)

    x = qkv_ref[...]
    xbuf[hist:hist + ch, :] = x
    w = convw_ref[...]
    y = x * w[CONV_K - 1:CONV_K, :]
    for j in range(CONV_K - 1):
        y = y + xbuf[hist - (CONV_K - 1) + j:hist - (CONV_K - 1) + j + ch, :] * w[j:j + 1, :]
    xbuf[0:hist, :] = x[ch - hist:ch, :]
    y = _silu(y)

    sm = sm_ref[...]
    row = c * ch + lax.broadcasted_iota(jnp.int32, sm.shape, 0)
    beta = jax.nn.sigmoid(sm)
    zsm = sm + dtb_ref[...]
    softplus = jnp.maximum(zsm, 0.0) + jnp.log(1.0 + jnp.exp(-jnp.abs(zsm)))
    g = jnp.where(row >= pad, -jnp.exp(alog_ref[...]) * softplus, 0.0)

    ri = lax.broadcasted_iota(jnp.int32, (ch, ch), 0)
    ci = lax.broadcasted_iota(jnp.int32, (ch, ch), 1)
    lower = ri >= ci
    strict = ri > ci
    tri = jnp.where(lower, 1.0, 0.0).astype(BF16)
    g1 = g.astype(BF16)
    r1 = g - g1.astype(F32)
    g2 = r1.astype(BF16)
    g3 = (r1 - g2.astype(F32)).astype(BF16)
    gcum = (jnp.dot(tri, g1, preferred_element_type=F32)
            + jnp.dot(tri, g2, preferred_element_type=F32)
            + jnp.dot(tri, g3, preferred_element_type=F32))
    gcum_t = gcum.T
    eye = jnp.where(ri == ci, 1.0, 0.0).astype(F32)

    bw = B_HEADS * B_HEAD_DIM
    hs = range(B_HEADS)
    sl = [slice(h * B_HEAD_DIM, (h + 1) * B_HEAD_DIM) for h in hs]
    part = lambda j: [y[:, j * bw + h * B_HEAD_DIM:j * bw + (h + 1) * B_HEAD_DIM] for h in hs]
    l2n = lambda x: x * lax.rsqrt(jnp.sum(x * x, axis=-1, keepdims=True) + NORM_EPS)
    q = [l2n(x) * B_HEAD_DIM ** -0.5 for x in part(0)]
    k = [l2n(x) for x in part(1)]
    v = part(2)
    b = [beta[:, SM_BETA + h:SM_BETA + h + 1] for h in hs]
    gc = [gcum[:, SM_A + h:SM_A + h + 1] for h in hs]
    gr = [gcum_t[SM_A + h:SM_A + h + 1, :] for h in hs]
    gl = [x[ch - 1:ch, :] for x in gc]
    eg = [jnp.exp(x) for x in gc]
    decay = [jnp.where(lower, jnp.exp(jnp.where(lower, gc[h] - gr[h], 0.0)), 0.0) for h in hs]
    kb = [k[h] * b[h] for h in hs]
    a_mat = [jnp.where(strict, _dot3(kb[h], k[h], _NT) * decay[h], 0.0) for h in hs]
    t_mat = [eye - a for a in a_mat]
    pw = a_mat
    for _ in range(ch.bit_length() - 2):
        pw = [_dot3(p, p) for p in pw]
        t_mat = [t_mat[h] + _dot3(t_mat[h], pw[h]) for h in hs]
    u = [_mm(t_mat[h], v[h] * b[h]) for h in hs]
    wm = [_mm(t_mat[h], kb[h] * eg[h]) for h in hs]
    qk = [jnp.where(lower, _mm(q[h], k[h], _NT) * decay[h], 0.0) for h in hs]
    state = [s_scr[h] for h in hs]
    v_new = [u[h] - _mm(wm[h], state[h]) for h in hs]
    o = [_mm(q[h] * eg[h], state[h]) + _mm(qk[h], v_new[h]) for h in hs]
    kd = [k[h] * jnp.exp(gl[h] - gc[h]) for h in hs]
    new_state = [state[h] * jnp.exp(gl[h]) + _mm(kd[h], v_new[h], _TN) for h in hs]
    for h in hs:
        s_scr[h] = new_state[h]
    for h in hs:
        o_h = _rms_rows(o[h]) * onorm_ref[...] * _silu(z_ref[:, sl[h]])
        o_ref[:, sl[h]] = o_h.astype(o_ref.dtype)


def _gdn(p2, p1, conv_w, a_log, dt_bias, o_norm, *, pad):
    tp = p2.shape[0]
    bw = B_HEADS * B_HEAD_DIM
    alog = jnp.zeros((1, LANES), F32).at[0, SM_A:SM_A + B_HEADS].set(a_log.astype(F32))
    dtb = jnp.zeros((1, LANES), F32).at[0, SM_A:SM_A + B_HEADS].set(dt_bias.astype(F32))
    ch = GDN_CHUNK
    return pl.pallas_call(
        functools.partial(_gdn_kernel, pad=pad),
        name="gdn",
        out_shape=jax.ShapeDtypeStruct((tp, bw), BF16),
        grid=(tp // ch,),
        in_specs=[
            pl.BlockSpec((ch, 3 * bw), lambda c: (c, 0)),
            pl.BlockSpec((ch, bw), lambda c: (c, 3)),
            pl.BlockSpec((ch, LANES), lambda c: (c, A_Q_RANK // LANES)),
            pl.BlockSpec((CONV_K, 3 * bw), lambda c: (0, 0)),
            pl.BlockSpec((1, LANES), lambda c: (0, 0)),
            pl.BlockSpec((1, LANES), lambda c: (0, 0)),
            pl.BlockSpec((1, B_HEAD_DIM), lambda c: (0, 0)),
        ],
        out_specs=pl.BlockSpec((ch, bw), lambda c: (c, 0)),
        scratch_shapes=[pltpu.VMEM((SUBLANES + ch, 3 * bw), F32),
                        pltpu.VMEM((B_HEADS, B_HEAD_DIM, B_HEAD_DIM), F32)],
        compiler_params=_params("arbitrary"),
    )(p2, p2, p1, conv_w.astype(F32), alog, dtb, o_norm.reshape(1, B_HEAD_DIM).astype(F32))


def _ret_kernel(q_ref, k_ref, v_ref, g_ref, cos_ref, sin_ref, dmask_ref, qdec_ref, kdec_ref,
                cdec_ref, o_ref, r_scr):
    @pl.when(pl.program_id(1) == 0)
    def _():
        r_scr[...] = jnp.zeros_like(r_scr)

    cos, sin = cos_ref[...], sin_ref[...]
    half = C_QK_DIM // 2

    def rot(x):
        x1, x2 = x[:, :half], x[:, half:]
        return jnp.concatenate([x1 * cos - x2 * sin, x1 * sin + x2 * cos], axis=1)

    q = rot(q_ref[...].astype(F32))
    k = rot(k_ref[...].astype(F32)) * C_QK_DIM ** -0.5
    v = v_ref[...]
    s = lax.dot_general(q.astype(BF16), k.astype(BF16), _NT, preferred_element_type=F32)
    s = s * dmask_ref[...]
    r = r_scr[...]
    o = jnp.dot(s.astype(BF16), v, preferred_element_type=F32)
    o = o + jnp.dot((q * qdec_ref[...]).astype(BF16), r.astype(BF16), preferred_element_type=F32)
    kd = (k * kdec_ref[...]).astype(BF16)
    r_scr[...] = r * cdec_ref[...] + lax.dot_general(kd, v, _TN, preferred_element_type=F32)
    gate = g_ref[...].astype(F32)
    o_ref[...] = (_silu(gate) * _rms_rows(o)).astype(o_ref.dtype)


def _retention(r_all, cos, sin):
    tp = r_all.shape[0]
    ch = RET_CHUNK
    nq = C_HEADS
    log_gamma = jnp.log(1.0 - 2.0 ** (-5.0 - jnp.arange(C_HEADS, dtype=F32)))
    pos = jnp.arange(ch, dtype=F32)
    rel = pos[:, None] - pos[None, :]
    dmask = jnp.where(rel >= 0, jnp.exp(jnp.maximum(rel, 0.0)[None] * log_gamma[:, None, None]), 0.0)
    qdec = jnp.exp((pos + 1.0)[None, :] * log_gamma[:, None])[:, :, None]
    kdec = jnp.exp((ch - 1.0 - pos)[None, :] * log_gamma[:, None])[:, :, None]
    cdec = jnp.broadcast_to(jnp.exp(ch * log_gamma)[:, None, None], (C_HEADS, 1, C_V_DIM))
    vcol0 = 2 * C_HEADS * C_QK_DIM // C_V_DIM
    return pl.pallas_call(
        _ret_kernel,
        name="retention",
        out_shape=jax.ShapeDtypeStruct((tp, C_HEADS * C_V_DIM), BF16),
        grid=(C_HEADS, tp // ch),
        in_specs=[
            pl.BlockSpec((ch, C_QK_DIM), lambda h, c: (c, h)),
            pl.BlockSpec((ch, C_QK_DIM), lambda h, c: (c, nq + h)),
            pl.BlockSpec((ch, C_V_DIM), lambda h, c: (c, vcol0 + h)),
            pl.BlockSpec((ch, C_V_DIM), lambda h, c: (c, vcol0 + C_HEADS + h)),
            pl.BlockSpec((ch, C_QK_DIM // 2), lambda h, c: (c, 0)),
            pl.BlockSpec((ch, C_QK_DIM // 2), lambda h, c: (c, 0)),
            pl.BlockSpec((None, ch, ch), lambda h, c: (h, 0, 0)),
            pl.BlockSpec((None, ch, 1), lambda h, c: (h, 0, 0)),
            pl.BlockSpec((None, ch, 1), lambda h, c: (h, 0, 0)),
            pl.BlockSpec((None, 1, C_V_DIM), lambda h, c: (h, 0, 0)),
        ],
        out_specs=pl.BlockSpec((ch, C_V_DIM), lambda h, c: (c, h)),
        scratch_shapes=[pltpu.VMEM((C_QK_DIM, C_V_DIM), F32)],
        compiler_params=_params("arbitrary", "arbitrary"),
    )(r_all, r_all, r_all, r_all, cos, sin, dmask, qdec, kdec, cdec)


def _hybrid_mixer(u_hi, u_lo, w_in, q_norm, w_uq, w_qidx, kv_norm, w_ukv, kidx_norm,
                  conv_w, a_log, dt_bias, o_norm, w_out, *, top_k, pad, tm):
    tp = u_hi.shape[0]
    o = [0]
    for s in (A_Q_RANK, A_KV_RANK, IDX_DIM, IDX_HEADS, 3 * B_HEADS * B_HEAD_DIM,
              B_HEADS * B_HEAD_DIM, B_HEADS, B_HEADS):
        o.append(o[-1] + s)
    col = lambda i: w_in[:, o[i]:o[i + 1]]
    n_small = IDX_DIM + IDX_HEADS + 2 * B_HEADS
    w1 = jnp.concatenate([col(0), col(2), col(3), col(6), col(7),
                          jnp.zeros((w_in.shape[0], LANES - n_small), w_in.dtype)], axis=1)
    w2 = jnp.concatenate([col(4), col(5), col(1)], axis=1).astype(BF16)
    p1 = _matmul("hy_in_x3", (u_hi, u_lo), _hilo(w1), F32, tm=tm, tn=w1.shape[1])
    p2 = _matmul("hy_in", (u_hi,), (w2,), F32, tm=tm, tn=MXU_DIM)

    p1t = p1.T
    cqt_hi, cqt_lo = _rms_t(p1t, q_norm, width=A_Q_RANK, row_block=0, tc=tm, lo=True)
    qidxt = _matmul("dsa_qidx", _hilo(w_qidx.T), (cqt_hi, cqt_lo), F32, tm=MXU_DIM, tn=tm)
    r = qidxt.reshape(IDX_HEADS, IDX_DIM, tp)
    r_hi, r_lo = _hilo(r)
    qi3t = jnp.concatenate([r_hi, r_lo, r_hi, jnp.zeros_like(r_hi)], axis=1)
    qi3t = qi3t.reshape(IDX_HEADS * MXU_DIM, tp)
    q_scale = A_HEAD_DIM ** -0.5 * math.log2(math.e)
    qt = _matmul("dsa_q", ((w_uq.T * q_scale).astype(BF16),), (cqt_hi,), BF16, tm=MXU_DIM, tn=tm)
    k3 = _kidx(p1, kidx_norm, tm=tm)
    wt = p1t[A_Q_RANK + SM_WIDX:A_Q_RANK + SM_BETA]
    ckv_block = (p2.shape[1] - A_KV_RANK) // A_KV_RANK
    kvn = _rms(p2, kv_norm, width=A_KV_RANK, col_block=ckv_block, tm=tm)
    kvnt = _rms_t(p2[:, p2.shape[1] - A_KV_RANK:].T, kv_norm, width=A_KV_RANK, row_block=0, tc=tm)
    w_kv = w_ukv.reshape(A_KV_RANK, A_HEADS, 2, A_HEAD_DIM)
    w_k = w_kv[:, :, 0].reshape(A_KV_RANK, A_HEADS * A_HEAD_DIM).astype(BF16)
    w_vt = w_kv[:, :, 1].reshape(A_KV_RANK, A_HEADS * A_HEAD_DIM).T.astype(BF16)
    k = _matmul("dsa_k", (kvn,), (w_k,), BF16, tm=tm, tn=MXU_DIM)
    vt = _matmul("dsa_vt", (w_vt,), (kvnt,), BF16, tm=MXU_DIM, tn=tm)
    o_a = _dsa(qt, qi3t, wt, k3, k, vt, top_k=top_k, pad=pad, tq=MXU_DIM, tk=tm).T

    o_b = _gdn(p2, p1, conv_w, a_log, dt_bias, o_norm, pad=pad)

    o_ab = jnp.concatenate([o_a, o_b], axis=1)
    return _matmul("hy_out", (o_ab,), (w_out.astype(BF16),), F32, tm=tm, tn=2 * MXU_DIM)


def _retention_mixer(u, w_in, w_out, *, pad, tm):
    tp = u.shape[0]
    qk_w = C_HEADS * C_QK_DIM
    half = C_QK_DIM // 2
    within = jnp.concatenate([jnp.arange(0, C_QK_DIM, 2), jnp.arange(1, C_QK_DIM, 2)])
    perm = (jnp.arange(C_HEADS)[:, None] * C_QK_DIM + within[None, :]).reshape(-1)
    w = jnp.concatenate([w_in[:, :qk_w][:, perm], w_in[:, qk_w:2 * qk_w][:, perm],
                         w_in[:, 2 * qk_w:]], axis=1).astype(BF16)
    r_all = _matmul("ret_in", (u,), (w,), BF16, tm=tm, tn=2 * MXU_DIM)
    inv = 1.0 / (ROT_BASE ** jnp.linspace(0.0, 1.0, half, dtype=F32))
    posn = (jnp.arange(tp) - pad).astype(F32)
    ang = posn[:, None] * inv[None, :]
    og = _retention(r_all, jnp.cos(ang), jnp.sin(ang))
    return _matmul("ret_out", (og,), (w_out.astype(BF16),), F32, tm=tm, tn=2 * MXU_DIM,
                   tk=4 * MXU_DIM)


def _ffn(u, w_gate, w_up, w_down, *, tm):
    act = _ffn_act(u, w_gate.astype(BF16), w_up.astype(BF16), tm=tm, tn=2 * MXU_DIM)
    d_ff = w_down.shape[0]
    tk = d_ff // 4 if d_ff % (4 * LANES) == 0 else d_ff
    return _matmul("ffn_down", (act,), (w_down.astype(BF16),), F32, tm=tm, tn=2 * MXU_DIM, tk=tk)


def kernel(x, meta_tokens, mix_norm_pre, mix_norm_post, ffn_norm_pre, ffn_norm_post, hy_w_in, dsa_q_norm, dsa_w_uq, dsa_w_qidx, dsa_kv_norm, dsa_w_ukv, dsa_kidx_norm, gdn_conv_w, gdn_a_log, gdn_dt_bias, gdn_o_norm, hy_w_out, ret_w_in, ret_w_out, ffn_w_gate, ffn_w_up, ffn_w_down):
    b, seq, d = x.shape
    assert b == 1 and mix_norm_pre.shape[0] == 2
    pad = (-N_META) % ROW_ALIGN
    tp = pad + N_META + seq
    tm = ROW_TILE
    assert tp % tm == 0
    top_k = min(TOPK_MAX, seq // TOPK_FRAC)
    h = jnp.concatenate([jnp.zeros((pad, d), F32), meta_tokens.astype(F32), x[0].astype(F32)], axis=0)

    u_hi, u_lo = _rms(h, mix_norm_pre[0], width=d, col_block=0, tm=tm, lo=True)
    mix = _hybrid_mixer(u_hi, u_lo, hy_w_in[0], dsa_q_norm[0], dsa_w_uq[0], dsa_w_qidx[0],
                        dsa_kv_norm[0], dsa_w_ukv[0], dsa_kidx_norm[0], gdn_conv_w[0],
                        gdn_a_log[0], gdn_dt_bias[0], gdn_o_norm[0], hy_w_out[0],
                        top_k=top_k, pad=pad, tm=tm)
    h, u = _resid_norm(h, mix, mix_norm_post[0], ffn_norm_pre[0], tm=tm)
    f = _ffn(u, ffn_w_gate[0], ffn_w_up[0], ffn_w_down[0], tm=tm)
    h, u = _resid_norm(h, f, ffn_norm_post[0], mix_norm_pre[1], tm=tm)
    mix = _retention_mixer(u, ret_w_in[0], ret_w_out[0], pad=pad, tm=tm)
    h, u = _resid_norm(h, mix, mix_norm_post[1], ffn_norm_pre[1], tm=tm)
    f = _ffn(u, ffn_w_gate[1], ffn_w_up[1], ffn_w_down[1], tm=tm)
    h = _resid_norm(h, f, ffn_norm_post[1], None, tm=tm)
    return h[pad + N_META:][None].astype(x.dtype)
```

```python
import functools
import math

import jax
import jax.numpy as jnp
from jax import lax
from jax.experimental import pallas as pl
from jax.experimental.pallas import tpu as pltpu

F32 = jnp.float32
BF16 = jnp.bfloat16

N_META = 16
NORM_EPS = 1e-6

A_HEADS = 8
A_HEAD_DIM = 128
A_Q_RANK = 512
A_KV_RANK = 256
IDX_HEADS = 16
IDX_DIM = 64
TOPK_MAX = 256
TOPK_FRAC = 4

B_HEADS = 8
B_HEAD_DIM = 128
CONV_K = 4
GDN_CHUNK = 128
DSA_HEADS_PER_STEP = 2
DSA_SUB_BLOCK = 64

C_HEADS = 8
C_QK_DIM = 256
C_V_DIM = 512
ROT_BASE = 10000.0
RET_CHUNK = 256

LANES = 128
SUBLANES = 8
MXU_DIM = 256
ROW_ALIGN = MXU_DIM
ROW_TILE = 3 * MXU_DIM
WIDE_ROW_TILE = 11 * LANES
VMEM_LIMIT = 48 * 1024 * 1024

SM_WIDX = IDX_DIM
SM_BETA = SM_WIDX + IDX_HEADS
SM_A = SM_BETA + B_HEADS

INT_MIN = -2 ** 31
KEY_FLT_MAX = 0x7F7FFFFF
KEY_MIN_NORMAL = 0x00800000
KEY_FLT_LOWEST = (0xFF7FFFFF ^ 0x7FFFFFFF) - 2 ** 32


def _params(*sem):
    return pltpu.CompilerParams(dimension_semantics=sem, vmem_limit_bytes=VMEM_LIMIT)


def _hilo(x):
    hi = x.astype(BF16)
    return hi, (x - hi.astype(F32)).astype(BF16)


def _silu(x):
    return x * jax.nn.sigmoid(x)


def _mm_kernel(*refs, n_parts, epilogue):
    a, b, o = refs[:n_parts], refs[n_parts:2 * n_parts], refs[2 * n_parts:]
    a_hi, b_hi = a[0][...], b[0][...]
    acc = jnp.dot(a_hi, b_hi, preferred_element_type=F32)
    if n_parts == 2:
        acc += jnp.dot(a_hi, b[1][...], preferred_element_type=F32)
        acc += jnp.dot(a[1][...], b_hi, preferred_element_type=F32)
    if epilogue is None:
        o[0][...] = acc.astype(o[0].dtype)
    elif epilogue == "transposed":
        o[0][...] = acc.T
        o[1][...] = acc[:, acc.shape[1] - LANES:]
    elif epilogue == "split3":
        for g in range(acc.shape[0] // IDX_DIM):
            hi, lo = _hilo(acc[g * IDX_DIM:(g + 1) * IDX_DIM, :])
            base = g * MXU_DIM
            o[0][base:base + IDX_DIM, :] = hi
            o[0][base + IDX_DIM:base + 2 * IDX_DIM, :] = lo
            o[0][base + 2 * IDX_DIM:base + 3 * IDX_DIM, :] = hi
            o[0][base + 3 * IDX_DIM:base + MXU_DIM, :] = jnp.zeros_like(hi)


def _matmul(name, a_parts, b_parts, out_dtype, *, tm, tn, epilogue=None):
    m, kdim = a_parts[0].shape
    n = b_parts[0].shape[1]
    assert m % tm == 0 and n % tn == 0, (m, n, tm, tn)
    assert len(a_parts) == len(b_parts)
    if epilogue is None:
        out_shape = [jax.ShapeDtypeStruct((m, n), out_dtype)]
        out_specs = [pl.BlockSpec((tm, tn), lambda i, j: (i, j))]
    elif epilogue == "transposed":
        assert tn == n
        out_shape = [jax.ShapeDtypeStruct((n, m), out_dtype),
                     jax.ShapeDtypeStruct((m, LANES), out_dtype)]
        out_specs = [pl.BlockSpec((n, tm), lambda i, j: (0, i)),
                     pl.BlockSpec((tm, LANES), lambda i, j: (i, 0))]
    else:
        scale = MXU_DIM // IDX_DIM
        out_shape = [jax.ShapeDtypeStruct((m * scale, n), out_dtype)]
        out_specs = [pl.BlockSpec((tm * scale, tn), lambda i, j: (i, j))]
    out = pl.pallas_call(
        functools.partial(_mm_kernel, n_parts=len(a_parts), epilogue=epilogue),
        name=name,
        out_shape=out_shape,
        grid=(m // tm, n // tn),
        in_specs=[pl.BlockSpec((tm, kdim), lambda i, j: (i, 0))] * len(a_parts)
        + [pl.BlockSpec((kdim, tn), lambda i, j: (0, j))] * len(b_parts),
        out_specs=out_specs,
        compiler_params=_params("parallel", "parallel"),
    )(*a_parts, *b_parts)
    return out[0] if len(out) == 1 else out


def _mm_w32_kernel(*refs, n_a):
    a, w_ref, o_ref = refs[:n_a], refs[n_a], refs[n_a + 1]
    w = w_ref[...].astype(BF16)
    acc, k0 = None, 0
    for a_ref in a:
        k1 = k0 + a_ref.shape[1]
        part = jnp.dot(a_ref[...], w[k0:k1, :], preferred_element_type=F32)
        acc = part if acc is None else acc + part
        k0 = k1
    o_ref[...] = acc.astype(o_ref.dtype)


def _matmul_w32(name, a_list, w, layer, out_dtype, *, tm, tn, col0=0, n=None):
    m = a_list[0].shape[0]
    kdim = sum(a.shape[1] for a in a_list)
    n = w.shape[2] - col0 if n is None else n
    assert w.shape[1] == kdim and m % tm == 0 and n % tn == 0 and col0 % tn == 0
    jb = col0 // tn
    return pl.pallas_call(
        functools.partial(_mm_w32_kernel, n_a=len(a_list)),
        name=name,
        out_shape=jax.ShapeDtypeStruct((m, n), out_dtype),
        grid=(m // tm, n // tn),
        in_specs=[pl.BlockSpec((tm, a.shape[1]), lambda i, j: (i, 0)) for a in a_list]
        + [pl.BlockSpec((None, kdim, tn), lambda i, j: (layer, 0, j + jb))],
        out_specs=pl.BlockSpec((tm, tn), lambda i, j: (i, j)),
        compiler_params=_params("parallel", "parallel"),
    )(*a_list, w)


def _ffn_act_kernel(x_ref, wg_ref, wu_ref, o_ref):
    x = x_ref[...]
    g = jnp.dot(x, wg_ref[...].astype(BF16), preferred_element_type=F32)
    u = jnp.dot(x, wu_ref[...].astype(BF16), preferred_element_type=F32)
    o_ref[...] = (_silu(g) * u).astype(o_ref.dtype)


def _ffn_act(x, wg, wu, layer, *, tm, tn):
    m, kdim = x.shape
    n = wg.shape[2]
    assert m % tm == 0 and n % tn == 0
    w_spec = pl.BlockSpec((None, kdim, tn), lambda i, j: (layer, 0, j))
    return pl.pallas_call(
        _ffn_act_kernel,
        name="ffn_act",
        out_shape=jax.ShapeDtypeStruct((m, n), BF16),
        grid=(m // tm, n // tn),
        in_specs=[pl.BlockSpec((tm, kdim), lambda i, j: (i, 0)), w_spec, w_spec],
        out_specs=pl.BlockSpec((tm, tn), lambda i, j: (i, j)),
        compiler_params=_params("parallel", "parallel"),
    )(x, wg, wu)


def _rms_rows(x):
    return x * lax.rsqrt(jnp.mean(x * x, axis=-1, keepdims=True) + NORM_EPS)


def _rms_kernel(x_ref, g_ref, *o_refs):
    y = _rms_rows(x_ref[...].astype(F32)) * g_ref[...]
    hi = y.astype(BF16)
    o_refs[0][...] = hi
    if len(o_refs) == 2:
        o_refs[1][...] = (y - hi.astype(F32)).astype(BF16)


def _rms(x, gain, *, width, col_block, tm, lo=False):
    m = x.shape[0]
    n_out = 2 if lo else 1
    out = pl.pallas_call(
        _rms_kernel,
        name="rms",
        out_shape=[jax.ShapeDtypeStruct((m, width), BF16)] * n_out,
        grid=(m // tm,),
        in_specs=[pl.BlockSpec((tm, width), lambda i: (i, col_block)),
                  pl.BlockSpec((1, width), lambda i: (0, 0))],
        out_specs=[pl.BlockSpec((tm, width), lambda i: (i, 0))] * n_out,
        compiler_params=_params("parallel"),
    )(x, gain.reshape(1, width).astype(F32))
    return out if lo else out[0]


def _rms_both_kernel(x_ref, g_ref, o_ref, ot_ref):
    y = _rms_rows(x_ref[...]) * g_ref[...]
    o_ref[...] = y.astype(BF16)
    ot_ref[...] = y.T.astype(BF16)


def _rms_both(x, gain, *, width, col_block, tm):
    m = x.shape[0]
    return pl.pallas_call(
        _rms_both_kernel,
        name="rms_both",
        out_shape=[jax.ShapeDtypeStruct((m, width), BF16), jax.ShapeDtypeStruct((width, m), BF16)],
        grid=(m // tm,),
        in_specs=[pl.BlockSpec((tm, width), lambda i: (i, col_block)),
                  pl.BlockSpec((1, width), lambda i: (0, 0))],
        out_specs=[pl.BlockSpec((tm, width), lambda i: (i, 0)),
                   pl.BlockSpec((width, tm), lambda i: (0, i))],
        compiler_params=_params("parallel"),
    )(x, gain.reshape(1, width).astype(F32))


def _rms_t_kernel(x_ref, g_ref, *o_refs):
    x = x_ref[...]
    y = x * lax.rsqrt(jnp.mean(x * x, axis=0, keepdims=True) + NORM_EPS) * g_ref[...]
    hi = y.astype(BF16)
    o_refs[0][...] = hi
    if len(o_refs) == 2:
        o_refs[1][...] = (y - hi.astype(F32)).astype(BF16)


def _rms_t(xt, gain, *, width, row_block, tc, lo=False):
    t = xt.shape[1]
    n_out = 2 if lo else 1
    out = pl.pallas_call(
        _rms_t_kernel,
        name="rms_t",
        out_shape=[jax.ShapeDtypeStruct((width, t), BF16)] * n_out,
        grid=(t // tc,),
        in_specs=[pl.BlockSpec((width, tc), lambda i: (row_block, i)),
                  pl.BlockSpec((width, 1), lambda i: (0, 0))],
        out_specs=[pl.BlockSpec((width, tc), lambda i: (0, i))] * n_out,
        compiler_params=_params("parallel"),
    )(xt, gain.reshape(width, 1).astype(F32))
    return out if lo else out[0]


def _resid_kernel(h_ref, u_ref, gp_ref, *rest, with_next):
    hn = h_ref[...] + _rms_rows(u_ref[...]) * gp_ref[...]
    if with_next:
        gn_ref, hn_ref, un_ref = rest
        un_ref[...] = (_rms_rows(hn) * gn_ref[...]).astype(BF16)
    else:
        (hn_ref,) = rest
    hn_ref[...] = hn


def _resid_norm(h, u, g_post, g_next, *, tm):
    m, d = h.shape
    row = pl.BlockSpec((tm, d), lambda i: (i, 0))
    vec = pl.BlockSpec((1, d), lambda i: (0, 0))
    with_next = g_next is not None
    gains = [g_post.reshape(1, d).astype(F32)]
    out_shape = [jax.ShapeDtypeStruct((m, d), F32)]
    if with_next:
        gains.append(g_next.reshape(1, d).astype(F32))
        out_shape.append(jax.ShapeDtypeStruct((m, d), BF16))
    out = pl.pallas_call(
        functools.partial(_resid_kernel, with_next=with_next),
        name="resid_norm",
        out_shape=out_shape,
        grid=(m // tm,),
        in_specs=[row, row] + [vec] * len(gains),
        out_specs=[row] * len(out_shape),
        compiler_params=_params("parallel"),
    )(h, u, *gains)
    return out if with_next else out[0]


def _kidx_kernel(sm_ref, g_ref, o_ref):
    sm = sm_ref[...]
    lane = lax.broadcasted_iota(jnp.int32, sm.shape, 1)
    x = jnp.where(lane < IDX_DIM, sm, 0.0)
    ms = jnp.sum(x * x, axis=-1, keepdims=True) * (1.0 / IDX_DIM)
    y = x * lax.rsqrt(ms + NORM_EPS) * g_ref[...]
    hi = y.astype(BF16)
    lo = (y - hi.astype(F32)).astype(BF16)
    y2 = y + pltpu.roll(y, IDX_DIM, axis=1)
    o_ref[:, 0:LANES] = y2.astype(BF16)
    o_ref[:, LANES:2 * LANES] = lo


def _kidx(p1, gain, *, tm):
    m = p1.shape[0]
    g = jnp.zeros((1, LANES), F32).at[0, :IDX_DIM].set(gain.astype(F32))
    return pl.pallas_call(
        _kidx_kernel,
        name="dsa_kidx",
        out_shape=jax.ShapeDtypeStruct((m, 2 * LANES), BF16),
        grid=(m // tm,),
        in_specs=[pl.BlockSpec((tm, LANES), lambda i: (i, 0)),
                  pl.BlockSpec((1, LANES), lambda i: (0, 0))],
        out_specs=pl.BlockSpec((tm, 2 * LANES), lambda i: (i, 0)),
        compiler_params=_params("parallel"),
    )(p1, g)


def _dsa_kernel(qt_ref, qi3t_ref, wt_ref, k3_ref, k_ref, vt_ref, o_ref, sc_scr, acc_scr, sa_scr,
                sb_scr, pa_scr, pb_scr, *, tq, tk, sb, top_k, pad, hps, w_scale, pos_bits):
    qi = pl.program_id(0)
    hd = pl.program_id(1)
    q0 = qi * tq
    nkt = (q0 + tq + tk - 1) // tk

    krel = lax.broadcasted_iota(jnp.int32, (tk, tq), 0)

    tp = k_ref.shape[0]

    @pl.when(hd == 0)
    def _select():
        sc_scr[tp:tp + tk, :] = jnp.full((tk, tq), -jnp.inf, F32)
        qpos = q0 + lax.broadcasted_iota(jnp.int32, (tk, tq), 1)

        def score_tile(kt, carry):
            ks = pl.multiple_of(kt * tk, tk)
            k3 = k3_ref[pl.ds(ks, tk), :]
            acc = jnp.zeros((tk, tq), F32)
            for ih in range(IDX_HEADS):
                d = jnp.dot(k3, qi3t_ref[ih * MXU_DIM:(ih + 1) * MXU_DIM, :],
                            preferred_element_type=F32)
                acc = acc + (wt_ref[ih:ih + 1, :] * w_scale) * jnp.maximum(d, 0.0)
            kpos = ks + krel
            ok = (kpos <= qpos) & (kpos >= pad)
            masked = jnp.where(ok, acc, -jnp.inf)
            sc_scr[pl.ds(ks, tk), :] = masked
            gmax[...] = jnp.maximum(gmax[...], masked)
            return carry

        gmax = sa_scr.at[0]
        gmax[...] = jnp.full((tk, tq), -jnp.inf, F32)
        lax.fori_loop(0, nkt, score_tile, 0)

        def count(pred):
            def body(kt, acc):
                ks = pl.multiple_of(kt * tk, tk)
                hit = pred(sc_scr[pl.ds(ks, tk), :], ks).astype(jnp.int32)
                return acc + jnp.sum(hit.reshape(tk // SUBLANES, SUBLANES, tq), axis=0)

            acc = lax.fori_loop(0, nkt, body, jnp.zeros((SUBLANES, tq), jnp.int32))
            return jnp.sum(acc, axis=0, keepdims=True)

        def key_to_f32(t):
            t = jnp.clip(t, jnp.int32(KEY_FLT_LOWEST), jnp.int32(KEY_FLT_MAX))
            return pltpu.bitcast(jnp.where(t >= 0, t, t ^ jnp.int32(0x7FFFFFFF)), F32)

        def f32_to_key(x):
            bits = pltpu.bitcast(x, jnp.int32)
            return jnp.where(bits >= 0, bits, bits ^ jnp.int32(0x7FFFFFFF))

        zero = jnp.zeros((1, tq), jnp.int32)
        g = gmax[...]
        lo = jnp.min(g, axis=0, keepdims=True)
        hi = jnp.max(g, axis=0, keepdims=True)
        int_min = jnp.int32(INT_MIN)
        lo_u = jnp.where(lo == -jnp.inf, zero, f32_to_key(lo) ^ int_min)
        hi_u = jnp.where(hi == -jnp.inf, zero, f32_to_key(hi) ^ int_min)
        top_bit = jnp.max(31 - lax.clz(lo_u ^ hi_u))
        keep = jnp.minimum(top_bit + 1, 31)
        t_u = lax.shift_left(lax.shift_right_logical(lo_u, keep), keep)
        t_u = jnp.where(top_bit >= 31, zero, t_u)

        def bit_body(i, t_u):
            cand_u = t_u | lax.shift_left(jnp.int32(1), top_bit - i)
            cand = cand_u ^ int_min
            cand_f = key_to_f32(cand)
            ok = (count(lambda s, ks: s >= cand_f) >= top_k) & (cand <= jnp.int32(KEY_FLT_MAX))
            return jnp.where(ok, cand_u, t_u)

        t = lax.fori_loop(0, top_bit + 1, bit_body, t_u) ^ int_min
        thr = key_to_f32(t)

        n_ge = count(lambda s, ks: s >= thr)

        @pl.when(jnp.max(n_ge) > top_k)
        def _ties():
            flushed = (t >= jnp.int32(-KEY_MIN_NORMAL)) & (t < jnp.int32(KEY_MIN_NORMAL))
            thr_next = key_to_f32(jnp.where(flushed, jnp.int32(KEY_MIN_NORMAL), t + 1))
            tie = lambda s: (s >= thr) & jnp.logical_not(s >= thr_next)
            want = top_k - count(lambda s, ks: s >= thr_next)

            def pos_body(i, cut):
                cand = cut + lax.shift_left(jnp.int32(1), pos_bits - 1 - i)
                n = count(lambda s, ks: tie(s) & (ks + krel < cand))
                return jnp.where(n < want, cand, cut)

            cut = lax.fori_loop(0, pos_bits, pos_body, zero)

            def drop_tile(kt, carry):
                ks = pl.multiple_of(kt * tk, tk)
                s = sc_scr[pl.ds(ks, tk), :]
                sc_scr[pl.ds(ks, tk), :] = jnp.where(tie(s) & (ks + krel > cut), -jnp.inf, s)
                return carry

            lax.fori_loop(0, nkt, drop_tile, 0)

        def bias_tile(kt, carry):
            ks = pl.multiple_of(kt * tk, tk)
            sc_scr[pl.ds(ks, tk), :] = jnp.where(sc_scr[pl.ds(ks, tk), :] >= thr, 0.0, -jnp.inf)
            return carry

        lax.fori_loop(0, nkt, bias_tile, 0)

    hd_sl = [slice(j * A_HEAD_DIM, (j + 1) * A_HEAD_DIM) for j in range(hps)]
    q = [qt_ref[sl, :] for sl in hd_sl]

    def logits_into(kt, s_ref):
        ks = pl.multiple_of(kt * tk, tk)
        for j in range(hps):
            s_ref[j] = jnp.dot(k_ref[pl.ds(ks, tk), hd_sl[j]], q[j], preferred_element_type=F32)

    fold = lambda x: x.reshape(sb // SUBLANES, SUBLANES, tq)

    def att_tile(kt, kt_next, bias_row, s_cur, s_next, p_ref, carry):
        logits_into(kt_next, s_next)
        ks = pl.multiple_of(kt * tk, tk)

        def masked(j, i):
            r = i * sb
            return s_cur[j, r:r + sb, :] + sc_scr[pl.ds(bias_row + r, sb), :]

        out, alphas = [], []
        for j in range(hps):
            m, l = carry[j]
            mx = jnp.full((SUBLANES, tq), -jnp.inf, F32)
            for i in range(tk // sb):
                mx = jnp.maximum(mx, jnp.max(fold(masked(j, i)), axis=0))
            m_new = jnp.maximum(m, jnp.max(mx, axis=0, keepdims=True))
            m_safe = jnp.where(m_new == -jnp.inf, 0.0, m_new)
            ls = jnp.zeros((SUBLANES, tq), F32)
            for i in range(tk // sb):
                p = jnp.exp2(masked(j, i) - m_safe)
                p_ref[j, i * sb:(i + 1) * sb, :] = p.astype(BF16)
                ls = ls + jnp.sum(fold(p), axis=0)
            alpha = jnp.exp2(m - m_safe)
            out.append((m_new, alpha * l + jnp.sum(ls, axis=0, keepdims=True)))
            alphas.append(alpha)
        pv = [jnp.dot(vt_ref[hd_sl[j], pl.ds(ks, tk)], p_ref[j], preferred_element_type=F32)
              for j in range(hps)]
        for j in range(hps):
            acc_scr[j] = alphas[j] * acc_scr[j] + pv[j]
        return tuple(out)

    def att_pair(i, carry):
        kt0 = 2 * i
        kt1 = jnp.minimum(kt0 + 1, nkt - 1)
        kt2 = jnp.minimum(kt0 + 2, nkt - 1)
        row1 = pl.multiple_of(jnp.where(kt0 + 1 < nkt, kt1 * tk, tp), tk)
        carry = att_tile(kt0, kt1, pl.multiple_of(kt0 * tk, tk), sa_scr, sb_scr, pa_scr, carry)
        return att_tile(kt1, kt2, row1, sb_scr, sa_scr, pb_scr, carry)

    acc_scr[...] = jnp.zeros_like(acc_scr)
    logits_into(0, sa_scr)
    init = (jnp.full((1, tq), -jnp.inf, F32), jnp.zeros((1, tq), F32))
    res = lax.fori_loop(0, (nkt + 1) // 2, att_pair, (init,) * hps)
    for j in range(hps):
        l = res[j][1]
        o_ref[:, hd_sl[j]] = jnp.where(l > 0.0, acc_scr[j] / l, 0.0).T.astype(o_ref.dtype)


def _dsa(qt, qi3t, wt, k3, k, vt, *, top_k, pad, tq, tk):
    tp = k.shape[0]
    assert tp % tq == 0 and tp % tk == 0 and tk >= top_k
    hps = DSA_HEADS_PER_STEP
    hw = hps * A_HEAD_DIM
    kern = functools.partial(
        _dsa_kernel, tq=tq, tk=tk, sb=DSA_SUB_BLOCK, top_k=top_k, pad=pad, hps=hps,
        w_scale=(IDX_HEADS * IDX_DIM) ** -0.5, pos_bits=tp.bit_length())
    return pl.pallas_call(
        kern,
        name="dsa",
        out_shape=jax.ShapeDtypeStruct((tp, A_HEADS * A_HEAD_DIM), BF16),
        grid=(tp // tq, A_HEADS // hps),
        in_specs=[
            pl.BlockSpec((hw, tq), lambda i, h: (h, i)),
            pl.BlockSpec((IDX_HEADS * MXU_DIM, tq), lambda i, h: (0, i)),
            pl.BlockSpec((IDX_HEADS, tq), lambda i, h: (0, i)),
            pl.BlockSpec((tp, MXU_DIM), lambda i, h: (0, 0)),
            pl.BlockSpec((tp, hw), lambda i, h: (0, h)),
            pl.BlockSpec((hw, tp), lambda i, h: (h, 0)),
        ],
        out_specs=pl.BlockSpec((tq, hw), lambda i, h: (i, h)),
        scratch_shapes=[pltpu.VMEM((tp + tk, tq), F32), pltpu.VMEM((hps, A_HEAD_DIM, tq), F32)]
        + [pltpu.VMEM((hps, tk, tq), F32)] * 2 + [pltpu.VMEM((hps, tk, tq), BF16)] * 2,
        compiler_params=_params("arbitrary", "arbitrary"),
    )(qt, qi3t, wt, k3, k, vt)


def _dot3(a, b, dims=(((1,), (0,)), ((), ()))):
    ah, al = _hilo(a)
    bh, bl = _hilo(b)
    f = lambda x, y: lax.dot_general(x, y, dims, preferred_element_type=F32)
    return f(ah, bh) + f(ah, bl) + f(al, bh)


_NN = (((1,), (0,)), ((), ()))
_NT = (((1,), (1,)), ((), ()))
_TN = (((0,), (0,)), ((), ()))


def _mm(a, b, dims=_NN):
    return lax.dot_general(a.astype(BF16), b.astype(BF16), dims, preferred_element_type=F32)


def _gdn_kernel(qkv_ref, z_ref, sm_ref, convw_ref, alog_ref, dtb_ref, onorm_ref, o_ref,
                xbuf, s_scr, *, pad):
    c = pl.program_id(0)
    ch = GDN_CHUNK
    hist = SUBLANES

    @pl.when(c == 0)
    def _():
        xbuf[0:hist, :] = jnp.zeros((hist, xbuf.shape[1]), F32)
        s_scr[...] = jnp.zeros_like(s_scr)

    x = qkv_ref[...]
    xbuf[hist:hist + ch, :] = x
    w = convw_ref[...]
    y = x * w[CONV_K - 1:CONV_K, :]
    for j in range(CONV_K - 1):
        y = y + xbuf[hist - (CONV_K - 1) + j:hist - (CONV_K - 1) + j + ch, :] * w[j:j + 1, :]
    xbuf[0:hist, :] = x[ch - hist:ch, :]
    y = _silu(y)

    sm = sm_ref[...]
    row = c * ch + lax.broadcasted_iota(jnp.int32, sm.shape, 0)
    beta = jax.nn.sigmoid(sm)
    zsm = sm + dtb_ref[...]
    softplus = jnp.maximum(zsm, 0.0) + jnp.log(1.0 + jnp.exp(-jnp.abs(zsm)))
    g = jnp.where(row >= pad, -jnp.exp(alog_ref[...]) * softplus, 0.0)

    ri = lax.broadcasted_iota(jnp.int32, (ch, ch), 0)
    ci = lax.broadcasted_iota(jnp.int32, (ch, ch), 1)
    lower = ri >= ci
    strict = ri > ci
    tri = jnp.where(lower, 1.0, 0.0).astype(BF16)
    g1 = g.astype(BF16)
    r1 = g - g1.astype(F32)
    g2 = r1.astype(BF16)
    g3 = (r1 - g2.astype(F32)).astype(BF16)
    gcum = (jnp.dot(tri, g1, preferred_element_type=F32)
            + jnp.dot(tri, g2, preferred_element_type=F32)
            + jnp.dot(tri, g3, preferred_element_type=F32))
    gcum_t = gcum.T
    eye = jnp.where(ri == ci, 1.0, 0.0).astype(F32)

    bw = B_HEADS * B_HEAD_DIM
    hs = range(B_HEADS)
    sl = [slice(h * B_HEAD_DIM, (h + 1) * B_HEAD_DIM) for h in hs]
    part = lambda j: [y[:, j * bw + h * B_HEAD_DIM:j * bw + (h + 1) * B_HEAD_DIM] for h in hs]
    l2n = lambda x: x * lax.rsqrt(jnp.sum(x * x, axis=-1, keepdims=True) + NORM_EPS)
    q = [l2n(x) * B_HEAD_DIM ** -0.5 for x in part(0)]
    k = [l2n(x) for x in part(1)]
    v = part(2)
    b = [beta[:, SM_BETA + h:SM_BETA + h + 1] for h in hs]
    gc = [gcum[:, SM_A + h:SM_A + h + 1] for h in hs]
    gr = [gcum_t[SM_A + h:SM_A + h + 1, :] for h in hs]
    gl = [x[ch - 1:ch, :] for x in gc]
    eg = [jnp.exp(x) for x in gc]
    decay = [jnp.where(lower, jnp.exp(jnp.where(lower, gc[h] - gr[h], 0.0)), 0.0) for h in hs]
    kb = [k[h] * b[h] for h in hs]
    a_mat = [jnp.where(strict, _dot3(kb[h], k[h], _NT) * decay[h], 0.0) for h in hs]
    t_mat = [eye - a for a in a_mat]
    pw = a_mat
    for _ in range(ch.bit_length() - 2):
        pw = [_dot3(p, p) for p in pw]
        t_mat = [t_mat[h] + _dot3(t_mat[h], pw[h]) for h in hs]
    u = [_mm(t_mat[h], v[h] * b[h]) for h in hs]
    wm = [_mm(t_mat[h], kb[h] * eg[h]) for h in hs]
    qk = [jnp.where(lower, _mm(q[h], k[h], _NT) * decay[h], 0.0) for h in hs]
    state = [s_scr[h] for h in hs]
    v_new = [u[h] - _mm(wm[h], state[h]) for h in hs]
    o = [_mm(q[h] * eg[h], state[h]) + _mm(qk[h], v_new[h]) for h in hs]
    kd = [k[h] * jnp.exp(gl[h] - gc[h]) for h in hs]
    new_state = [state[h] * jnp.exp(gl[h]) + _mm(kd[h], v_new[h], _TN) for h in hs]
    for h in hs:
        s_scr[h] = new_state[h]
    for h in hs:
        o_h = _rms_rows(o[h]) * onorm_ref[...] * _silu(z_ref[:, sl[h]])
        o_ref[:, sl[h]] = o_h.astype(o_ref.dtype)


def _gdn(p2, p1, conv_w, a_log, dt_bias, o_norm, *, pad):
    tp = p2.shape[0]
    bw = B_HEADS * B_HEAD_DIM
    alog = jnp.zeros((1, LANES), F32).at[0, SM_A:SM_A + B_HEADS].set(a_log.astype(F32))
    dtb = jnp.zeros((1, LANES), F32).at[0, SM_A:SM_A + B_HEADS].set(dt_bias.astype(F32))
    ch = GDN_CHUNK
    return pl.pallas_call(
        functools.partial(_gdn_kernel, pad=pad),
        name="gdn",
        out_shape=jax.ShapeDtypeStruct((tp, bw), BF16),
        grid=(tp // ch,),
        in_specs=[
            pl.BlockSpec((ch, 3 * bw), lambda c: (c, 0)),
            pl.BlockSpec((ch, bw), lambda c: (c, 3)),
            pl.BlockSpec((ch, LANES), lambda c: (c, 0)),
            pl.BlockSpec((CONV_K, 3 * bw), lambda c: (0, 0)),
            pl.BlockSpec((1, LANES), lambda c: (0, 0)),
            pl.BlockSpec((1, LANES), lambda c: (0, 0)),
            pl.BlockSpec((1, B_HEAD_DIM), lambda c: (0, 0)),
        ],
        out_specs=pl.BlockSpec((ch, bw), lambda c: (c, 0)),
        scratch_shapes=[pltpu.VMEM((SUBLANES + ch, 3 * bw), F32),
                        pltpu.VMEM((B_HEADS, B_HEAD_DIM, B_HEAD_DIM), F32)],
        compiler_params=_params("arbitrary"),
    )(p2, p2, p1, conv_w.astype(F32), alog, dtb, o_norm.reshape(1, B_HEAD_DIM).astype(F32))


def _ret_kernel(q_ref, k_ref, v_ref, g_ref, cos_ref, sin_ref, dmask_ref, qdec_ref, kdec_ref,
                cdec_ref, o_ref, r_scr):
    @pl.when(pl.program_id(1) == 0)
    def _():
        r_scr[...] = jnp.zeros_like(r_scr)

    cos, sin = cos_ref[...], sin_ref[...]
    half = C_QK_DIM // 2

    def rot(x):
        x1, x2 = x[:, :half], x[:, half:]
        return jnp.concatenate([x1 * cos - x2 * sin, x1 * sin + x2 * cos], axis=1)

    q = rot(q_ref[...].astype(F32))
    k = rot(k_ref[...].astype(F32)) * C_QK_DIM ** -0.5
    v = v_ref[...]
    s = lax.dot_general(q.astype(BF16), k.astype(BF16), _NT, preferred_element_type=F32)
    s = s * dmask_ref[...]
    r = r_scr[...]
    o = jnp.dot(s.astype(BF16), v, preferred_element_type=F32)
    o = o + jnp.dot((q * qdec_ref[...]).astype(BF16), r.astype(BF16), preferred_element_type=F32)
    kd = (k * kdec_ref[...]).astype(BF16)
    r_scr[...] = r * cdec_ref[...] + lax.dot_general(kd, v, _TN, preferred_element_type=F32)
    gate = g_ref[...].astype(F32)
    o_ref[...] = (_silu(gate) * _rms_rows(o)).astype(o_ref.dtype)


def _retention(r_qk, r_vg, cos, sin):
    tp = r_qk.shape[0]
    ch = RET_CHUNK
    log_gamma = jnp.log(1.0 - 2.0 ** (-5.0 - jnp.arange(C_HEADS, dtype=F32)))
    pos = jnp.arange(ch, dtype=F32)
    rel = pos[:, None] - pos[None, :]
    dmask = jnp.where(rel >= 0, jnp.exp(jnp.maximum(rel, 0.0)[None] * log_gamma[:, None, None]), 0.0)
    qdec = jnp.exp((pos + 1.0)[None, :] * log_gamma[:, None])[:, :, None]
    kdec = jnp.exp((ch - 1.0 - pos)[None, :] * log_gamma[:, None])[:, :, None]
    cdec = jnp.broadcast_to(jnp.exp(ch * log_gamma)[:, None, None], (C_HEADS, 1, C_V_DIM))
    return pl.pallas_call(
        _ret_kernel,
        name="retention",
        out_shape=jax.ShapeDtypeStruct((tp, C_HEADS * C_V_DIM), BF16),
        grid=(C_HEADS, tp // ch),
        in_specs=[
            pl.BlockSpec((ch, C_QK_DIM), lambda h, c: (c, h)),
            pl.BlockSpec((ch, C_QK_DIM), lambda h, c: (c, C_HEADS + h)),
            pl.BlockSpec((ch, C_V_DIM), lambda h, c: (c, h)),
            pl.BlockSpec((ch, C_V_DIM), lambda h, c: (c, C_HEADS + h)),
            pl.BlockSpec((ch, C_QK_DIM // 2), lambda h, c: (c, 0)),
            pl.BlockSpec((ch, C_QK_DIM // 2), lambda h, c: (c, 0)),
            pl.BlockSpec((None, ch, ch), lambda h, c: (h, 0, 0)),
            pl.BlockSpec((None, ch, 1), lambda h, c: (h, 0, 0)),
            pl.BlockSpec((None, ch, 1), lambda h, c: (h, 0, 0)),
            pl.BlockSpec((None, 1, C_V_DIM), lambda h, c: (h, 0, 0)),
        ],
        out_specs=pl.BlockSpec((ch, C_V_DIM), lambda h, c: (c, h)),
        scratch_shapes=[pltpu.VMEM((C_QK_DIM, C_V_DIM), F32)],
        compiler_params=_params("arbitrary", "arbitrary"),
    )(r_qk, r_qk, r_vg, r_vg, cos, sin, dmask, qdec, kdec, cdec)


def _hybrid_mixer(u_hi, u_lo, w_in, q_norm, w_uq, w_qidx, kv_norm, w_ukv, kidx_norm,
                  conv_w, a_log, dt_bias, o_norm, w_out, *, top_k, pad, tm, tm_wide):
    tp = u_hi.shape[0]
    o = [0]
    for s in (A_Q_RANK, A_KV_RANK, IDX_DIM, IDX_HEADS, 3 * B_HEADS * B_HEAD_DIM,
              B_HEADS * B_HEAD_DIM, B_HEADS, B_HEADS):
        o.append(o[-1] + s)
    col = lambda i: w_in[:, o[i]:o[i + 1]]
    n_small = IDX_DIM + IDX_HEADS + 2 * B_HEADS
    w1 = jnp.concatenate([col(0), col(2), col(3), col(6), col(7),
                          jnp.zeros((w_in.shape[0], LANES - n_small), w_in.dtype)], axis=1)
    w2 = jnp.concatenate([col(4), col(5), col(1)], axis=1).astype(BF16)
    p1t, small = _matmul("hy_in_x3", (u_hi, u_lo), _hilo(w1), F32, tm=tm, tn=w1.shape[1],
                         epilogue="transposed")
    p2 = _matmul("hy_in", (u_hi,), (w2,), F32, tm=tm_wide, tn=MXU_DIM)

    cqt_hi, cqt_lo = _rms_t(p1t, q_norm, width=A_Q_RANK, row_block=0, tc=tm, lo=True)
    qi3t = _matmul("dsa_qidx", _hilo(w_qidx.T), (cqt_hi, cqt_lo), BF16, tm=MXU_DIM, tn=tm,
                   epilogue="split3")
    q_scale = A_HEAD_DIM ** -0.5 * math.log2(math.e)
    qt = _matmul("dsa_q", ((w_uq.T * q_scale).astype(BF16),), (cqt_hi,), BF16, tm=MXU_DIM, tn=tm)
    k3 = _kidx(small, kidx_norm, tm=tm)
    wt = p1t[A_Q_RANK + SM_WIDX:A_Q_RANK + SM_BETA]
    ckv_block = (p2.shape[1] - A_KV_RANK) // A_KV_RANK
    kvn, kvnt = _rms_both(p2, kv_norm, width=A_KV_RANK, col_block=ckv_block, tm=tm)
    w_kv = w_ukv.reshape(A_KV_RANK, A_HEADS, 2, A_HEAD_DIM)
    w_k = w_kv[:, :, 0].reshape(A_KV_RANK, A_HEADS * A_HEAD_DIM).astype(BF16)
    w_vt = w_kv[:, :, 1].reshape(A_KV_RANK, A_HEADS * A_HEAD_DIM).T.astype(BF16)
    k = _matmul("dsa_k", (kvn,), (w_k,), BF16, tm=tm, tn=MXU_DIM)
    vt = _matmul("dsa_vt", (w_vt,), (kvnt,), BF16, tm=MXU_DIM, tn=tm)
    o_a = _dsa(qt, qi3t, wt, k3, k, vt, top_k=top_k, pad=pad, tq=MXU_DIM, tk=tm)

    o_b = _gdn(p2, small, conv_w, a_log, dt_bias, o_norm, pad=pad)

    return _matmul_w32("hy_out", [o_a, o_b], w_out[None], 0, F32, tm=tm, tn=2 * MXU_DIM)


def _retention_mixer(u, w_in, w_out, *, pad, tm, tm_wide):
    tp = u.shape[0]
    qk_w = C_HEADS * C_QK_DIM
    half = C_QK_DIM // 2
    within = jnp.concatenate([jnp.arange(0, C_QK_DIM, 2), jnp.arange(1, C_QK_DIM, 2)])
    perm = (jnp.arange(2 * C_HEADS)[:, None] * C_QK_DIM + within[None, :]).reshape(-1)
    w_qk = w_in[:, perm].astype(BF16)
    r_qk = _matmul("ret_in_qk", (u,), (w_qk,), BF16, tm=tm, tn=2 * MXU_DIM)
    r_vg = _matmul_w32("ret_in_vg", [u], w_in[None], 0, BF16, tm=tm_wide, tn=2 * MXU_DIM,
                       col0=2 * qk_w)
    inv = 1.0 / (ROT_BASE ** jnp.linspace(0.0, 1.0, half, dtype=F32))
    posn = (jnp.arange(tp) - pad).astype(F32)
    ang = posn[:, None] * inv[None, :]
    og = _retention(r_qk, r_vg, jnp.cos(ang), jnp.sin(ang))
    return _matmul_w32("ret_out", [og], w_out[None], 0, F32, tm=tm, tn=2 * MXU_DIM)


def _ffn(u, w_gate, w_up, w_down, layer, *, tm, tm_wide):
    act = _ffn_act(u, w_gate, w_up, layer, tm=tm_wide, tn=2 * MXU_DIM)
    return _matmul_w32("ffn_down", [act], w_down, layer, F32, tm=tm, tn=MXU_DIM)


def kernel(x, meta_tokens, mix_norm_pre, mix_norm_post, ffn_norm_pre, ffn_norm_post, hy_w_in, dsa_q_norm, dsa_w_uq, dsa_w_qidx, dsa_kv_norm, dsa_w_ukv, dsa_kidx_norm, gdn_conv_w, gdn_a_log, gdn_dt_bias, gdn_o_norm, hy_w_out, ret_w_in, ret_w_out, ffn_w_gate, ffn_w_up, ffn_w_down):
    b, seq, d = x.shape
    assert b == 1 and mix_norm_pre.shape[0] == 2
    pad = (-N_META) % ROW_ALIGN
    tp = pad + N_META + seq
    tm = ROW_TILE
    assert tp % tm == 0
    tm_wide = WIDE_ROW_TILE if tp % WIDE_ROW_TILE == 0 else tm
    top_k = min(TOPK_MAX, seq // TOPK_FRAC)
    h = jnp.concatenate([jnp.zeros((pad, d), F32), meta_tokens.astype(F32), x[0].astype(F32)], axis=0)

    u_hi, u_lo = _rms(h, mix_norm_pre[0], width=d, col_block=0, tm=tm, lo=True)
    mix = _hybrid_mixer(u_hi, u_lo, hy_w_in[0], dsa_q_norm[0], dsa_w_uq[0], dsa_w_qidx[0],
                        dsa_kv_norm[0], dsa_w_ukv[0], dsa_kidx_norm[0], gdn_conv_w[0],
                        gdn_a_log[0], gdn_dt_bias[0], gdn_o_norm[0], hy_w_out[0],
                        top_k=top_k, pad=pad, tm=tm, tm_wide=tm_wide)
    h, u = _resid_norm(h, mix, mix_norm_post[0], ffn_norm_pre[0], tm=tm)
    f = _ffn(u, ffn_w_gate, ffn_w_up, ffn_w_down, 0, tm=tm, tm_wide=tm_wide)
    h, u = _resid_norm(h, f, ffn_norm_post[0], mix_norm_pre[1], tm=tm)
    mix = _retention_mixer(u, ret_w_in[0], ret_w_out[0], pad=pad, tm=tm, tm_wide=tm_wide)
    h, u = _resid_norm(h, mix, mix_norm_post[1], ffn_norm_pre[1], tm=tm)
    f = _ffn(u, ffn_w_gate, ffn_w_up, ffn_w_down, 1, tm=tm, tm_wide=tm_wide)
    h = _resid_norm(h, f, ffn_norm_post[1], None, tm=tm)
    return h[pad + N_META:][None].astype(x.dtype)
```

```python
import functools
import math

import jax
import jax.numpy as jnp
from jax import lax
from jax.experimental import pallas as pl
from jax.experimental.pallas import tpu as pltpu

F32 = jnp.float32
BF16 = jnp.bfloat16

N_META = 16
NORM_EPS = 1e-6

A_HEADS = 8
A_HEAD_DIM = 128
A_Q_RANK = 512
A_KV_RANK = 256
IDX_HEADS = 16
IDX_DIM = 64
TOPK_MAX = 256
TOPK_FRAC = 4

B_HEADS = 8
B_HEAD_DIM = 128
CONV_K = 4
GDN_CHUNK = 128
DSA_HEADS_PER_STEP = 2
DSA_SEGMENTS = 3
DSA_SUB_BLOCK = 64

C_HEADS = 8
C_QK_DIM = 256
C_V_DIM = 512
ROT_BASE = 10000.0
RET_CHUNK = 256

LANES = 128
SUBLANES = 8
MXU_DIM = 256
ROW_ALIGN = MXU_DIM
ROW_TILE = 3 * MXU_DIM
WIDE_ROW_TILE = 11 * LANES
VMEM_LIMIT = 48 * 1024 * 1024

SM_WIDX = IDX_DIM
SM_BETA = SM_WIDX + IDX_HEADS
SM_A = SM_BETA + B_HEADS

INT_MIN = -2 ** 31
KEY_FLT_MAX = 0x7F7FFFFF
KEY_MIN_NORMAL = 0x00800000
KEY_FLT_LOWEST = (0xFF7FFFFF ^ 0x7FFFFFFF) - 2 ** 32


def _params(*sem):
    return pltpu.CompilerParams(dimension_semantics=sem, vmem_limit_bytes=VMEM_LIMIT)


def _hilo(x):
    hi = x.astype(BF16)
    return hi, (x - hi.astype(F32)).astype(BF16)


def _silu(x):
    return x * jax.nn.sigmoid(x)


def _mm_kernel(*refs, n_parts, epilogue):
    a, b, o = refs[:n_parts], refs[n_parts:2 * n_parts], refs[2 * n_parts:]
    a_hi, b_hi = a[0][...], b[0][...]
    acc = jnp.dot(a_hi, b_hi, preferred_element_type=F32)
    if n_parts == 2:
        acc += jnp.dot(a_hi, b[1][...], preferred_element_type=F32)
        acc += jnp.dot(a[1][...], b_hi, preferred_element_type=F32)
    if epilogue is None:
        o[0][...] = acc.astype(o[0].dtype)
    elif epilogue == "transposed":
        o[0][...] = acc.T
        o[1][...] = acc[:, acc.shape[1] - LANES:]
    elif epilogue == "split3":
        for g in range(acc.shape[0] // IDX_DIM):
            hi, lo = _hilo(acc[g * IDX_DIM:(g + 1) * IDX_DIM, :])
            base = g * MXU_DIM
            o[0][base:base + IDX_DIM, :] = hi
            o[0][base + IDX_DIM:base + 2 * IDX_DIM, :] = lo
            o[0][base + 2 * IDX_DIM:base + 3 * IDX_DIM, :] = hi
            o[0][base + 3 * IDX_DIM:base + MXU_DIM, :] = jnp.zeros_like(hi)


def _matmul(name, a_parts, b_parts, out_dtype, *, tm, tn, epilogue=None):
    m, kdim = a_parts[0].shape
    n = b_parts[0].shape[1]
    assert m % tm == 0 and n % tn == 0, (m, n, tm, tn)
    assert len(a_parts) == len(b_parts)
    if epilogue is None:
        out_shape = [jax.ShapeDtypeStruct((m, n), out_dtype)]
        out_specs = [pl.BlockSpec((tm, tn), lambda i, j: (i, j))]
    elif epilogue == "transposed":
        assert tn == n
        out_shape = [jax.ShapeDtypeStruct((n, m), out_dtype),
                     jax.ShapeDtypeStruct((m, LANES), out_dtype)]
        out_specs = [pl.BlockSpec((n, tm), lambda i, j: (0, i)),
                     pl.BlockSpec((tm, LANES), lambda i, j: (i, 0))]
    else:
        scale = MXU_DIM // IDX_DIM
        out_shape = [jax.ShapeDtypeStruct((m * scale, n), out_dtype)]
        out_specs = [pl.BlockSpec((tm * scale, tn), lambda i, j: (i, j))]
    out = pl.pallas_call(
        functools.partial(_mm_kernel, n_parts=len(a_parts), epilogue=epilogue),
        name=name,
        out_shape=out_shape,
        grid=(m // tm, n // tn),
        in_specs=[pl.BlockSpec((tm, kdim), lambda i, j: (i, 0))] * len(a_parts)
        + [pl.BlockSpec((kdim, tn), lambda i, j: (0, j))] * len(b_parts),
        out_specs=out_specs,
        compiler_params=_params("parallel", "parallel"),
    )(*a_parts, *b_parts)
    return out[0] if len(out) == 1 else out


def _mm_w32_kernel(*refs, n_a):
    a, w_ref, o_ref = refs[:n_a], refs[n_a], refs[n_a + 1]
    w = w_ref[...].astype(BF16)
    acc, k0 = None, 0
    for a_ref in a:
        k1 = k0 + a_ref.shape[1]
        part = jnp.dot(a_ref[...], w[k0:k1, :], preferred_element_type=F32)
        acc = part if acc is None else acc + part
        k0 = k1
    o_ref[...] = acc.astype(o_ref.dtype)


def _matmul_w32(name, a_list, w, layer, out_dtype, *, tm, tn, col0=0, n=None):
    m = a_list[0].shape[0]
    kdim = sum(a.shape[1] for a in a_list)
    n = w.shape[2] - col0 if n is None else n
    assert w.shape[1] == kdim and m % tm == 0 and n % tn == 0 and col0 % tn == 0
    jb = col0 // tn
    return pl.pallas_call(
        functools.partial(_mm_w32_kernel, n_a=len(a_list)),
        name=name,
        out_shape=jax.ShapeDtypeStruct((m, n), out_dtype),
        grid=(m // tm, n // tn),
        in_specs=[pl.BlockSpec((tm, a.shape[1]), lambda i, j: (i, 0)) for a in a_list]
        + [pl.BlockSpec((None, kdim, tn), lambda i, j: (layer, 0, j + jb))],
        out_specs=pl.BlockSpec((tm, tn), lambda i, j: (i, j)),
        compiler_params=_params("parallel", "parallel"),
    )(*a_list, w)


def _ffn_act_kernel(x_ref, wg_ref, wu_ref, o_ref):
    x = x_ref[...]
    g = jnp.dot(x, wg_ref[...].astype(BF16), preferred_element_type=F32)
    u = jnp.dot(x, wu_ref[...].astype(BF16), preferred_element_type=F32)
    o_ref[...] = (_silu(g) * u).astype(o_ref.dtype)


def _ffn_act(x, wg, wu, layer, *, tm, tn):
    m, kdim = x.shape
    n = wg.shape[2]
    assert m % tm == 0 and n % tn == 0
    w_spec = pl.BlockSpec((None, kdim, tn), lambda i, j: (layer, 0, j))
    return pl.pallas_call(
        _ffn_act_kernel,
        name="ffn_act",
        out_shape=jax.ShapeDtypeStruct((m, n), BF16),
        grid=(m // tm, n // tn),
        in_specs=[pl.BlockSpec((tm, kdim), lambda i, j: (i, 0)), w_spec, w_spec],
        out_specs=pl.BlockSpec((tm, tn), lambda i, j: (i, j)),
        compiler_params=_params("parallel", "parallel"),
    )(x, wg, wu)


def _rms_rows(x):
    return x * lax.rsqrt(jnp.mean(x * x, axis=-1, keepdims=True) + NORM_EPS)


def _rms_kernel(x_ref, g_ref, *o_refs):
    y = _rms_rows(x_ref[...].astype(F32)) * g_ref[...]
    hi = y.astype(BF16)
    o_refs[0][...] = hi
    if len(o_refs) == 2:
        o_refs[1][...] = (y - hi.astype(F32)).astype(BF16)


def _rms(x, gain, *, width, col_block, tm, lo=False):
    m = x.shape[0]
    n_out = 2 if lo else 1
    out = pl.pallas_call(
        _rms_kernel,
        name="rms",
        out_shape=[jax.ShapeDtypeStruct((m, width), BF16)] * n_out,
        grid=(m // tm,),
        in_specs=[pl.BlockSpec((tm, width), lambda i: (i, col_block)),
                  pl.BlockSpec((1, width), lambda i: (0, 0))],
        out_specs=[pl.BlockSpec((tm, width), lambda i: (i, 0))] * n_out,
        compiler_params=_params("parallel"),
    )(x, gain.reshape(1, width).astype(F32))
    return out if lo else out[0]


def _rms_both_kernel(x_ref, g_ref, o_ref, ot_ref):
    y = _rms_rows(x_ref[...]) * g_ref[...]
    o_ref[...] = y.astype(BF16)
    ot_ref[...] = y.T.astype(BF16)


def _rms_both(x, gain, *, width, col_block, tm):
    m = x.shape[0]
    return pl.pallas_call(
        _rms_both_kernel,
        name="rms_both",
        out_shape=[jax.ShapeDtypeStruct((m, width), BF16), jax.ShapeDtypeStruct((width, m), BF16)],
        grid=(m // tm,),
        in_specs=[pl.BlockSpec((tm, width), lambda i: (i, col_block)),
                  pl.BlockSpec((1, width), lambda i: (0, 0))],
        out_specs=[pl.BlockSpec((tm, width), lambda i: (i, 0)),
                   pl.BlockSpec((width, tm), lambda i: (0, i))],
        compiler_params=_params("parallel"),
    )(x, gain.reshape(1, width).astype(F32))


def _rms_t_kernel(x_ref, g_ref, *o_refs):
    x = x_ref[...]
    y = x * lax.rsqrt(jnp.mean(x * x, axis=0, keepdims=True) + NORM_EPS) * g_ref[...]
    hi = y.astype(BF16)
    o_refs[0][...] = hi
    if len(o_refs) == 2:
        o_refs[1][...] = (y - hi.astype(F32)).astype(BF16)


def _rms_t(xt, gain, *, width, row_block, tc, lo=False):
    t = xt.shape[1]
    n_out = 2 if lo else 1
    out = pl.pallas_call(
        _rms_t_kernel,
        name="rms_t",
        out_shape=[jax.ShapeDtypeStruct((width, t), BF16)] * n_out,
        grid=(t // tc,),
        in_specs=[pl.BlockSpec((width, tc), lambda i: (row_block, i)),
                  pl.BlockSpec((width, 1), lambda i: (0, 0))],
        out_specs=[pl.BlockSpec((width, tc), lambda i: (0, i))] * n_out,
        compiler_params=_params("parallel"),
    )(xt, gain.reshape(width, 1).astype(F32))
    return out if lo else out[0]


def _resid_kernel(h_ref, u_ref, gp_ref, *rest, with_next):
    hn = h_ref[...] + _rms_rows(u_ref[...].astype(F32)) * gp_ref[...]
    if with_next:
        gn_ref, hn_ref, un_ref = rest
        un_ref[...] = (_rms_rows(hn) * gn_ref[...]).astype(BF16)
    else:
        (hn_ref,) = rest
    hn_ref[...] = hn.astype(hn_ref.dtype)


def _resid_norm(h, u, g_post, g_next, *, tm, row0=0, out_dtype=F32):
    m, d = h.shape
    assert row0 % tm == 0 and m % tm == 0
    b0 = row0 // tm
    row_in = pl.BlockSpec((tm, d), lambda i: (i + b0, 0))
    row_out = pl.BlockSpec((tm, d), lambda i: (i, 0))
    vec = pl.BlockSpec((1, d), lambda i: (0, 0))
    with_next = g_next is not None
    gains = [g_post.reshape(1, d).astype(F32)]
    out_shape = [jax.ShapeDtypeStruct((m - row0, d), out_dtype)]
    if with_next:
        gains.append(g_next.reshape(1, d).astype(F32))
        out_shape.append(jax.ShapeDtypeStruct((m - row0, d), BF16))
    out = pl.pallas_call(
        functools.partial(_resid_kernel, with_next=with_next),
        name="resid_norm",
        out_shape=out_shape,
        grid=((m - row0) // tm,),
        in_specs=[row_in, row_in] + [vec] * len(gains),
        out_specs=[row_out] * len(out_shape),
        compiler_params=_params("parallel"),
    )(h, u, *gains)
    return out if with_next else out[0]


def _kidx_kernel(sm_ref, g_ref, o_ref):
    sm = sm_ref[...]
    lane = lax.broadcasted_iota(jnp.int32, sm.shape, 1)
    x = jnp.where(lane < IDX_DIM, sm, 0.0)
    ms = jnp.sum(x * x, axis=-1, keepdims=True) * (1.0 / IDX_DIM)
    y = x * lax.rsqrt(ms + NORM_EPS) * g_ref[...]
    hi = y.astype(BF16)
    lo = (y - hi.astype(F32)).astype(BF16)
    y2 = y + pltpu.roll(y, IDX_DIM, axis=1)
    o_ref[:, 0:LANES] = y2.astype(BF16)
    o_ref[:, LANES:2 * LANES] = lo


def _kidx(p1, gain, *, tm):
    m = p1.shape[0]
    g = jnp.zeros((1, LANES), F32).at[0, :IDX_DIM].set(gain.astype(F32))
    return pl.pallas_call(
        _kidx_kernel,
        name="dsa_kidx",
        out_shape=jax.ShapeDtypeStruct((m, 2 * LANES), BF16),
        grid=(m // tm,),
        in_specs=[pl.BlockSpec((tm, LANES), lambda i: (i, 0)),
                  pl.BlockSpec((1, LANES), lambda i: (0, 0))],
        out_specs=pl.BlockSpec((tm, 2 * LANES), lambda i: (i, 0)),
        compiler_params=_params("parallel"),
    )(p1, g)


def _dsa_kernel(qt_ref, qi3t_ref, wt_ref, k3_ref, k_ref, vt_ref, o_ref, sc_scr, acc_scr, sa_scr,
                sb_scr, pa_scr, pb_scr, *, tq, tk, sb, qb0, top_k, n_groups, pad, hps, w_scale,
                pos_bits):
    qi = pl.program_id(0)
    hd = pl.program_id(1)
    q0 = (qi + qb0) * tq
    nkt = (q0 + tq + tk - 1) // tk

    krel = lax.broadcasted_iota(jnp.int32, (tk, tq), 0)

    tp = k_ref.shape[0]

    @pl.when(hd == 0)
    def _select():
        sc_scr[tp:tp + tk, :] = jnp.full((tk, tq), -jnp.inf, F32)
        qpos = q0 + lax.broadcasted_iota(jnp.int32, (tk, tq), 1)

        def score_tile(kt, carry):
            ks = pl.multiple_of(kt * tk, tk)
            k3 = k3_ref[pl.ds(ks, tk), :]
            acc = jnp.zeros((tk, tq), F32)
            for ih in range(IDX_HEADS):
                d = jnp.dot(k3, qi3t_ref[ih * MXU_DIM:(ih + 1) * MXU_DIM, :],
                            preferred_element_type=F32)
                acc = acc + (wt_ref[ih:ih + 1, :] * w_scale) * jnp.maximum(d, 0.0)
            kpos = ks + krel
            ok = (kpos <= qpos) & (kpos >= pad)
            masked = jnp.where(ok, acc, -jnp.inf)
            sc_scr[pl.ds(ks, tk), :] = masked
            gmax[...] = jnp.maximum(gmax[...], masked)
            return carry

        gmax = sa_scr.at[0]
        gmax[...] = jnp.full((tk, tq), -jnp.inf, F32)
        lax.fori_loop(0, nkt, score_tile, 0)

        def count(pred):
            def body(kt, acc):
                ks = pl.multiple_of(kt * tk, tk)
                hit = pred(sc_scr[pl.ds(ks, tk), :], ks).astype(jnp.int32)
                return acc + jnp.sum(hit.reshape(tk // SUBLANES, SUBLANES, tq), axis=0)

            acc = lax.fori_loop(0, nkt, body, jnp.zeros((SUBLANES, tq), jnp.int32))
            return jnp.sum(acc, axis=0, keepdims=True)

        def key_to_f32(t):
            t = jnp.clip(t, jnp.int32(KEY_FLT_LOWEST), jnp.int32(KEY_FLT_MAX))
            return pltpu.bitcast(jnp.where(t >= 0, t, t ^ jnp.int32(0x7FFFFFFF)), F32)

        def f32_to_key(x):
            bits = pltpu.bitcast(x, jnp.int32)
            return jnp.where(bits >= 0, bits, bits ^ jnp.int32(0x7FFFFFFF))

        zero = jnp.zeros((1, tq), jnp.int32)
        g = jnp.max(gmax[...].reshape(tk // n_groups, n_groups, tq), axis=0)
        lo = jnp.min(g, axis=0, keepdims=True)
        hi = jnp.max(g, axis=0, keepdims=True)
        int_min = jnp.int32(INT_MIN)
        lo_u = jnp.where(lo == -jnp.inf, zero, f32_to_key(lo) ^ int_min)
        hi_u = jnp.where(hi == -jnp.inf, zero, f32_to_key(hi) ^ int_min)
        top_bit = jnp.max(31 - lax.clz(lo_u ^ hi_u))
        keep = jnp.minimum(top_bit + 1, 31)
        t_u = lax.shift_left(lax.shift_right_logical(lo_u, keep), keep)
        t_u = jnp.where(top_bit >= 31, zero, t_u)

        def bit_cond(state):
            i, _, n_at = state
            return (i <= top_bit) & (jnp.max(jnp.abs(n_at - top_k)) > 0)

        def bit_body(state):
            i, t_u, n_at = state
            cand_u = t_u | lax.shift_left(jnp.int32(1), top_bit - i)
            cand = cand_u ^ int_min
            cand_f = key_to_f32(cand)
            n = count(lambda s, ks: s >= cand_f)
            ok = (n >= top_k) & (cand <= jnp.int32(KEY_FLT_MAX))
            return i + 1, jnp.where(ok, cand_u, t_u), jnp.where(ok, n, n_at)

        unknown = jnp.full((1, tq), -1, jnp.int32)
        _, t_u, _ = lax.while_loop(bit_cond, bit_body, (jnp.int32(0), t_u, unknown))
        t = t_u ^ int_min
        thr = key_to_f32(t)

        n_ge = count(lambda s, ks: s >= thr)

        @pl.when(jnp.max(n_ge) > top_k)
        def _ties():
            flushed = (t >= jnp.int32(-KEY_MIN_NORMAL)) & (t < jnp.int32(KEY_MIN_NORMAL))
            thr_next = key_to_f32(jnp.where(flushed, jnp.int32(KEY_MIN_NORMAL), t + 1))
            tie = lambda s: (s >= thr) & jnp.logical_not(s >= thr_next)
            want = top_k - count(lambda s, ks: s >= thr_next)

            def pos_body(i, cut):
                cand = cut + lax.shift_left(jnp.int32(1), pos_bits - 1 - i)
                n = count(lambda s, ks: tie(s) & (ks + krel < cand))
                return jnp.where(n < want, cand, cut)

            cut = lax.fori_loop(0, pos_bits, pos_body, zero)

            def drop_tile(kt, carry):
                ks = pl.multiple_of(kt * tk, tk)
                s = sc_scr[pl.ds(ks, tk), :]
                sc_scr[pl.ds(ks, tk), :] = jnp.where(tie(s) & (ks + krel > cut), -jnp.inf, s)
                return carry

            lax.fori_loop(0, nkt, drop_tile, 0)

        def bias_tile(kt, carry):
            ks = pl.multiple_of(kt * tk, tk)
            sc_scr[pl.ds(ks, tk), :] = jnp.where(sc_scr[pl.ds(ks, tk), :] >= thr, 0.0, -jnp.inf)
            return carry

        lax.fori_loop(0, nkt, bias_tile, 0)

    hd_sl = [slice(j * A_HEAD_DIM, (j + 1) * A_HEAD_DIM) for j in range(hps)]
    q = [qt_ref[sl, :] for sl in hd_sl]

    def logits_into(kt, s_ref):
        ks = pl.multiple_of(kt * tk, tk)
        for j in range(hps):
            s_ref[j] = jnp.dot(k_ref[pl.ds(ks, tk), hd_sl[j]], q[j], preferred_element_type=F32)

    fold = lambda x: x.reshape(sb // SUBLANES, SUBLANES, tq)

    def att_tile(kt, kt_next, bias_row, s_cur, s_next, p_ref, carry):
        logits_into(kt_next, s_next)
        ks = pl.multiple_of(kt * tk, tk)

        def masked(j, i):
            r = i * sb
            return s_cur[j, r:r + sb, :] + sc_scr[pl.ds(bias_row + r, sb), :]

        out, alphas = [], []
        for j in range(hps):
            m, l = carry[j]
            mx = jnp.full((SUBLANES, tq), -jnp.inf, F32)
            for i in range(tk // sb):
                mx = jnp.maximum(mx, jnp.max(fold(masked(j, i)), axis=0))
            m_new = jnp.maximum(m, jnp.max(mx, axis=0, keepdims=True))
            m_safe = jnp.where(m_new == -jnp.inf, 0.0, m_new)
            ls = jnp.zeros((SUBLANES, tq), F32)
            for i in range(tk // sb):
                p = jnp.exp2(masked(j, i) - m_safe)
                p_ref[j, i * sb:(i + 1) * sb, :] = p.astype(BF16)
                ls = ls + jnp.sum(fold(p), axis=0)
            alpha = jnp.exp2(m - m_safe)
            out.append((m_new, alpha * l + jnp.sum(ls, axis=0, keepdims=True)))
            alphas.append(alpha)
        pv = [jnp.dot(vt_ref[hd_sl[j], pl.ds(ks, tk)], p_ref[j], preferred_element_type=F32)
              for j in range(hps)]
        for j in range(hps):
            acc_scr[j] = alphas[j] * acc_scr[j] + pv[j]
        return tuple(out)

    def att_pair(i, carry):
        kt0 = 2 * i
        kt1 = jnp.minimum(kt0 + 1, nkt - 1)
        kt2 = jnp.minimum(kt0 + 2, nkt - 1)
        row1 = pl.multiple_of(jnp.where(kt0 + 1 < nkt, kt1 * tk, tp), tk)
        carry = att_tile(kt0, kt1, pl.multiple_of(kt0 * tk, tk), sa_scr, sb_scr, pa_scr, carry)
        return att_tile(kt1, kt2, row1, sb_scr, sa_scr, pb_scr, carry)

    acc_scr[...] = jnp.zeros_like(acc_scr)
    logits_into(0, sa_scr)
    init = (jnp.full((1, tq), -jnp.inf, F32), jnp.zeros((1, tq), F32))
    res = lax.fori_loop(0, (nkt + 1) // 2, att_pair, (init,) * hps)
    for j in range(hps):
        l = res[j][1]
        o_ref[:, hd_sl[j]] = jnp.where(l > 0.0, acc_scr[j] / l, 0.0).T.astype(o_ref.dtype)


def _dsa(qt, qi3t, wt, k3, k, vt, *, top_k, pad, tq, tk):
    tp = k.shape[0]
    assert tp % tq == 0 and tp % tk == 0 and tk >= top_k
    hps = DSA_HEADS_PER_STEP
    hw = hps * A_HEAD_DIM
    n_groups = min(n for n in range(SUBLANES, tk + 1, SUBLANES) if tk % n == 0 and n >= top_k)
    n_tiles = tp // tk
    outs = []
    for seg in range(DSA_SEGMENTS):
        t0, t1 = seg * n_tiles // DSA_SEGMENTS, (seg + 1) * n_tiles // DSA_SEGMENTS
        if t1 == t0:
            continue
        qb0, nq, nk = t0 * tk // tq, (t1 - t0) * tk // tq, t1 * tk
        kern = functools.partial(
            _dsa_kernel, tq=tq, tk=tk, sb=DSA_SUB_BLOCK, qb0=qb0, top_k=top_k,
            n_groups=n_groups, pad=pad, hps=hps, w_scale=(IDX_HEADS * IDX_DIM) ** -0.5,
            pos_bits=tp.bit_length())
        outs.append(pl.pallas_call(
            kern,
            name="dsa",
            out_shape=jax.ShapeDtypeStruct((nq * tq, A_HEADS * A_HEAD_DIM), BF16),
            grid=(nq, A_HEADS // hps),
            in_specs=[
                pl.BlockSpec((hw, tq), lambda i, h, qb0=qb0: (h, i + qb0)),
                pl.BlockSpec((IDX_HEADS * MXU_DIM, tq), lambda i, h, qb0=qb0: (0, i + qb0)),
                pl.BlockSpec((IDX_HEADS, tq), lambda i, h, qb0=qb0: (0, i + qb0)),
                pl.BlockSpec((nk, MXU_DIM), lambda i, h: (0, 0)),
                pl.BlockSpec((nk, hw), lambda i, h: (0, h)),
                pl.BlockSpec((hw, nk), lambda i, h: (h, 0)),
            ],
            out_specs=pl.BlockSpec((tq, hw), lambda i, h: (i, h)),
            scratch_shapes=[pltpu.VMEM((nk + tk, tq), F32),
                            pltpu.VMEM((hps, A_HEAD_DIM, tq), F32)]
            + [pltpu.VMEM((hps, tk, tq), F32)] * 2 + [pltpu.VMEM((hps, tk, tq), BF16)] * 2,
            compiler_params=_params("arbitrary", "arbitrary"),
        )(qt, qi3t, wt, k3, k, vt))
    return outs[0] if len(outs) == 1 else jnp.concatenate(outs, axis=0)


def _dot3(a, b, dims=(((1,), (0,)), ((), ()))):
    ah, al = _hilo(a)
    bh, bl = _hilo(b)
    f = lambda x, y: lax.dot_general(x, y, dims, preferred_element_type=F32)
    return f(ah, bh) + f(ah, bl) + f(al, bh)


_NN = (((1,), (0,)), ((), ()))
_NT = (((1,), (1,)), ((), ()))
_TN = (((0,), (0,)), ((), ()))


def _mm(a, b, dims=_NN):
    return lax.dot_general(a.astype(BF16), b.astype(BF16), dims, preferred_element_type=F32)


def _gdn_kernel(qkv_ref, z_ref, sm_ref, convw_ref, alog_ref, dtb_ref, onorm_ref, o_ref,
                xbuf, s_scr, *, pad):
    c = pl.program_id(0)
    ch = GDN_CHUNK
    hist = SUBLANES

    @pl.when(c == 0)
    def _():
        xbuf[0:hist, :] = jnp.zeros((hist, xbuf.shape[1]), F32)
        s_scr[...] = jnp.zeros_like(s_scr)

    x = qkv_ref[...]
    xbuf[hist:hist + ch, :] = x
    w = convw_ref[...]
    y = x * w[CONV_K - 1:CONV_K, :]
    for j in range(CONV_K - 1):
        y = y + xbuf[hist - (CONV_K - 1) + j:hist - (CONV_K - 1) + j + ch, :] * w[j:j + 1, :]
    xbuf[0:hist, :] = x[ch - hist:ch, :]
    y = _silu(y)

    sm = sm_ref[...]
    row = c * ch + lax.broadcasted_iota(jnp.int32, sm.shape, 0)
    beta = jax.nn.sigmoid(sm)
    zsm = sm + dtb_ref[...]
    softplus = jnp.maximum(zsm, 0.0) + jnp.log(1.0 + jnp.exp(-jnp.abs(zsm)))
    g = jnp.where(row >= pad, -jnp.exp(alog_ref[...]) * softplus, 0.0)

    ri = lax.broadcasted_iota(jnp.int32, (ch, ch), 0)
    ci = lax.broadcasted_iota(jnp.int32, (ch, ch), 1)
    lower = ri >= ci
    strict = ri > ci
    tri = jnp.where(lower, 1.0, 0.0).astype(BF16)
    g1 = g.astype(BF16)
    r1 = g - g1.astype(F32)
    g2 = r1.astype(BF16)
    g3 = (r1 - g2.astype(F32)).astype(BF16)
    gcum = (jnp.dot(tri, g1, preferred_element_type=F32)
            + jnp.dot(tri, g2, preferred_element_type=F32)
            + jnp.dot(tri, g3, preferred_element_type=F32))
    gcum_t = gcum.T
    eye = jnp.where(ri == ci, 1.0, 0.0).astype(F32)

    bw = B_HEADS * B_HEAD_DIM
    hs = range(B_HEADS)
    sl = [slice(h * B_HEAD_DIM, (h + 1) * B_HEAD_DIM) for h in hs]
    part = lambda j: [y[:, j * bw + h * B_HEAD_DIM:j * bw + (h + 1) * B_HEAD_DIM] for h in hs]
    l2n = lambda x: x * lax.rsqrt(jnp.sum(x * x, axis=-1, keepdims=True) + NORM_EPS)
    q = [l2n(x) * B_HEAD_DIM ** -0.5 for x in part(0)]
    k = [l2n(x) for x in part(1)]
    v = part(2)
    b = [beta[:, SM_BETA + h:SM_BETA + h + 1] for h in hs]
    gc = [gcum[:, SM_A + h:SM_A + h + 1] for h in hs]
    gr = [gcum_t[SM_A + h:SM_A + h + 1, :] for h in hs]
    gl = [x[ch - 1:ch, :] for x in gc]
    eg = [jnp.exp(x) for x in gc]
    decay = [jnp.where(lower, jnp.exp(jnp.where(lower, gc[h] - gr[h], 0.0)), 0.0) for h in hs]
    kb = [k[h] * b[h] for h in hs]
    a_mat = [jnp.where(strict, _dot3(kb[h], k[h], _NT) * decay[h], 0.0) for h in hs]
    t_mat = [eye - a for a in a_mat]
    pw = a_mat
    for _ in range(ch.bit_length() - 2):
        pw = [_dot3(p, p) for p in pw]
        t_mat = [t_mat[h] + _dot3(t_mat[h], pw[h]) for h in hs]
    u = [_mm(t_mat[h], v[h] * b[h]) for h in hs]
    wm = [_mm(t_mat[h], kb[h] * eg[h]) for h in hs]
    qk = [jnp.where(lower, _mm(q[h], k[h], _NT) * decay[h], 0.0) for h in hs]
    state = [s_scr[h] for h in hs]
    v_new = [u[h] - _mm(wm[h], state[h]) for h in hs]
    o = [_mm(q[h] * eg[h], state[h]) + _mm(qk[h], v_new[h]) for h in hs]
    kd = [k[h] * jnp.exp(gl[h] - gc[h]) for h in hs]
    new_state = [state[h] * jnp.exp(gl[h]) + _mm(kd[h], v_new[h], _TN) for h in hs]
    for h in hs:
        s_scr[h] = new_state[h]
    for h in hs:
        o_h = _rms_rows(o[h]) * onorm_ref[...] * _silu(z_ref[:, sl[h]])
        o_ref[:, sl[h]] = o_h.astype(o_ref.dtype)


def _gdn(p2, p1, conv_w, a_log, dt_bias, o_norm, *, pad):
    tp = p2.shape[0]
    bw = B_HEADS * B_HEAD_DIM
    alog = jnp.zeros((1, LANES), F32).at[0, SM_A:SM_A + B_HEADS].set(a_log.astype(F32))
    dtb = jnp.zeros((1, LANES), F32).at[0, SM_A:SM_A + B_HEADS].set(dt_bias.astype(F32))
    ch = GDN_CHUNK
    return pl.pallas_call(
        functools.partial(_gdn_kernel, pad=pad),
        name="gdn",
        out_shape=jax.ShapeDtypeStruct((tp, bw), BF16),
        grid=(tp // ch,),
        in_specs=[
            pl.BlockSpec((ch, 3 * bw), lambda c: (c, 0)),
            pl.BlockSpec((ch, bw), lambda c: (c, 3)),
            pl.BlockSpec((ch, LANES), lambda c: (c, 0)),
            pl.BlockSpec((CONV_K, 3 * bw), lambda c: (0, 0)),
            pl.BlockSpec((1, LANES), lambda c: (0, 0)),
            pl.BlockSpec((1, LANES), lambda c: (0, 0)),
            pl.BlockSpec((1, B_HEAD_DIM), lambda c: (0, 0)),
        ],
        out_specs=pl.BlockSpec((ch, bw), lambda c: (c, 0)),
        scratch_shapes=[pltpu.VMEM((SUBLANES + ch, 3 * bw), F32),
                        pltpu.VMEM((B_HEADS, B_HEAD_DIM, B_HEAD_DIM), F32)],
        compiler_params=_params("arbitrary"),
    )(p2, p2, p1, conv_w.astype(F32), alog, dtb, o_norm.reshape(1, B_HEAD_DIM).astype(F32))


def _ret_kernel(q_ref, k_ref, v_ref, g_ref, cos_ref, sin_ref, dmask_ref, qdec_ref, kdec_ref,
                cdec_ref, o_ref, r_scr):
    @pl.when(pl.program_id(1) == 0)
    def _():
        r_scr[...] = jnp.zeros_like(r_scr)

    cos, sin = cos_ref[...], sin_ref[...]
    even = lax.broadcasted_iota(jnp.int32, cos.shape, 1) % 2 == 0

    def rot(x):
        partner = jnp.where(even, pltpu.roll(x, C_QK_DIM - 1, axis=1), pltpu.roll(x, 1, axis=1))
        return x * cos + partner * sin

    q = rot(q_ref[...].astype(F32))
    k = rot(k_ref[...].astype(F32)) * C_QK_DIM ** -0.5
    v = v_ref[...]
    s = lax.dot_general(q.astype(BF16), k.astype(BF16), _NT, preferred_element_type=F32)
    s = s * dmask_ref[...]
    r = r_scr[...]
    o = jnp.dot(s.astype(BF16), v, preferred_element_type=F32)
    o = o + jnp.dot((q * qdec_ref[...]).astype(BF16), r.astype(BF16), preferred_element_type=F32)
    kd = (k * kdec_ref[...]).astype(BF16)
    r_scr[...] = r * cdec_ref[...] + lax.dot_general(kd, v, _TN, preferred_element_type=F32)
    gate = g_ref[...].astype(F32)
    o_ref[...] = (_silu(gate) * _rms_rows(o)).astype(o_ref.dtype)


def _retention(r_all, cos, sin):
    tp = r_all.shape[0]
    ch = RET_CHUNK
    v0 = 2 * C_HEADS * C_QK_DIM // C_V_DIM
    log_gamma = jnp.log(1.0 - 2.0 ** (-5.0 - jnp.arange(C_HEADS, dtype=F32)))
    pos = jnp.arange(ch, dtype=F32)
    rel = pos[:, None] - pos[None, :]
    dmask = jnp.where(rel >= 0, jnp.exp(jnp.maximum(rel, 0.0)[None] * log_gamma[:, None, None]), 0.0)
    qdec = jnp.exp((pos + 1.0)[None, :] * log_gamma[:, None])[:, :, None]
    kdec = jnp.exp((ch - 1.0 - pos)[None, :] * log_gamma[:, None])[:, :, None]
    cdec = jnp.broadcast_to(jnp.exp(ch * log_gamma)[:, None, None], (C_HEADS, 1, C_V_DIM))
    return pl.pallas_call(
        _ret_kernel,
        name="retention",
        out_shape=jax.ShapeDtypeStruct((tp, C_HEADS * C_V_DIM), BF16),
        grid=(C_HEADS, tp // ch),
        in_specs=[
            pl.BlockSpec((ch, C_QK_DIM), lambda h, c: (c, h)),
            pl.BlockSpec((ch, C_QK_DIM), lambda h, c: (c, C_HEADS + h)),
            pl.BlockSpec((ch, C_V_DIM), lambda h, c: (c, v0 + h)),
            pl.BlockSpec((ch, C_V_DIM), lambda h, c: (c, v0 + C_HEADS + h)),
            pl.BlockSpec((ch, C_QK_DIM), lambda h, c: (c, 0)),
            pl.BlockSpec((ch, C_QK_DIM), lambda h, c: (c, 0)),
            pl.BlockSpec((None, ch, ch), lambda h, c: (h, 0, 0)),
            pl.BlockSpec((None, ch, 1), lambda h, c: (h, 0, 0)),
            pl.BlockSpec((None, ch, 1), lambda h, c: (h, 0, 0)),
            pl.BlockSpec((None, 1, C_V_DIM), lambda h, c: (h, 0, 0)),
        ],
        out_specs=pl.BlockSpec((ch, C_V_DIM), lambda h, c: (c, h)),
        scratch_shapes=[pltpu.VMEM((C_QK_DIM, C_V_DIM), F32)],
        compiler_params=_params("arbitrary", "arbitrary"),
    )(r_all, r_all, r_all, r_all, cos, sin, dmask, qdec, kdec, cdec)


def _hybrid_mixer(u_hi, u_lo, w_in, q_norm, w_uq, w_qidx, kv_norm, w_ukv, kidx_norm,
                  conv_w, a_log, dt_bias, o_norm, w_out, *, top_k, pad, tm, tm_wide):
    tp = u_hi.shape[0]
    o = [0]
    for s in (A_Q_RANK, A_KV_RANK, IDX_DIM, IDX_HEADS, 3 * B_HEADS * B_HEAD_DIM,
              B_HEADS * B_HEAD_DIM, B_HEADS, B_HEADS):
        o.append(o[-1] + s)
    col = lambda i: w_in[:, o[i]:o[i + 1]]
    n_small = IDX_DIM + IDX_HEADS + 2 * B_HEADS
    w1 = jnp.concatenate([col(0), col(2), col(3), col(6), col(7),
                          jnp.zeros((w_in.shape[0], LANES - n_small), w_in.dtype)], axis=1)
    w2 = jnp.concatenate([col(4), col(5), col(1)], axis=1).astype(BF16)
    p1t, small = _matmul("hy_in_x3", (u_hi, u_lo), _hilo(w1), F32, tm=tm, tn=w1.shape[1],
                         epilogue="transposed")
    p2 = _matmul("hy_in", (u_hi,), (w2,), F32, tm=tm_wide, tn=MXU_DIM)

    cqt_hi, cqt_lo = _rms_t(p1t, q_norm, width=A_Q_RANK, row_block=0, tc=tm, lo=True)
    qi3t = _matmul("dsa_qidx", _hilo(w_qidx.T), (cqt_hi, cqt_lo), BF16, tm=MXU_DIM, tn=tm,
                   epilogue="split3")
    q_scale = A_HEAD_DIM ** -0.5 * math.log2(math.e)
    qt = _matmul("dsa_q", ((w_uq.T * q_scale).astype(BF16),), (cqt_hi,), BF16, tm=MXU_DIM, tn=tm)
    k3 = _kidx(small, kidx_norm, tm=tm)
    wt = p1t[A_Q_RANK + SM_WIDX:A_Q_RANK + SM_BETA]
    ckv_block = (p2.shape[1] - A_KV_RANK) // A_KV_RANK
    kvn, kvnt = _rms_both(p2, kv_norm, width=A_KV_RANK, col_block=ckv_block, tm=tm)
    w_kv = w_ukv.reshape(A_KV_RANK, A_HEADS, 2, A_HEAD_DIM)
    w_k = w_kv[:, :, 0].reshape(A_KV_RANK, A_HEADS * A_HEAD_DIM).astype(BF16)
    w_vt = w_kv[:, :, 1].reshape(A_KV_RANK, A_HEADS * A_HEAD_DIM).T.astype(BF16)
    k = _matmul("dsa_k", (kvn,), (w_k,), BF16, tm=tm, tn=MXU_DIM)
    vt = _matmul("dsa_vt", (w_vt,), (kvnt,), BF16, tm=MXU_DIM, tn=tm)
    o_a = _dsa(qt, qi3t, wt, k3, k, vt, top_k=top_k, pad=pad, tq=MXU_DIM, tk=tm)

    o_b = _gdn(p2, small, conv_w, a_log, dt_bias, o_norm, pad=pad)

    return _matmul_w32("hy_out", [o_a, o_b], w_out[None], 0, BF16, tm=tm, tn=2 * MXU_DIM)


def _retention_mixer(u, w_in, w_out, *, pad, tm, tm_wide):
    tp = u.shape[0]
    half = C_QK_DIM // 2
    r_all = _matmul_w32("ret_in", [u], w_in[None], 0, BF16, tm=tm_wide, tn=2 * MXU_DIM)
    inv = 1.0 / (ROT_BASE ** jnp.linspace(0.0, 1.0, half, dtype=F32))
    posn = (jnp.arange(tp) - pad).astype(F32)
    ang = posn[:, None] * inv[None, :]
    cos = jnp.repeat(jnp.cos(ang), 2, axis=1)
    sin = jnp.stack([-jnp.sin(ang), jnp.sin(ang)], axis=-1).reshape(tp, C_QK_DIM)
    og = _retention(r_all, cos, sin)
    return _matmul_w32("ret_out", [og], w_out[None], 0, BF16, tm=tm, tn=2 * MXU_DIM)


def _ffn(u, w_gate, w_up, w_down, layer, *, tm, tm_wide):
    act = _ffn_act(u, w_gate, w_up, layer, tm=tm_wide, tn=2 * MXU_DIM)
    return _matmul_w32("ffn_down", [act], w_down, layer, BF16, tm=tm, tn=MXU_DIM)


def kernel(x, meta_tokens, mix_norm_pre, mix_norm_post, ffn_norm_pre, ffn_norm_post, hy_w_in, dsa_q_norm, dsa_w_uq, dsa_w_qidx, dsa_kv_norm, dsa_w_ukv, dsa_kidx_norm, gdn_conv_w, gdn_a_log, gdn_dt_bias, gdn_o_norm, hy_w_out, ret_w_in, ret_w_out, ffn_w_gate, ffn_w_up, ffn_w_down):
    b, seq, d = x.shape
    assert b == 1 and mix_norm_pre.shape[0] == 2
    pad = (-N_META) % ROW_ALIGN
    tp = pad + N_META + seq
    tm = ROW_TILE
    assert tp % tm == 0
    tm_wide = WIDE_ROW_TILE if tp % WIDE_ROW_TILE == 0 else tm
    top_k = min(TOPK_MAX, seq // TOPK_FRAC)
    h = jnp.concatenate([jnp.zeros((pad, d), F32), meta_tokens.astype(F32), x[0].astype(F32)], axis=0)

    u_hi, u_lo = _rms(h, mix_norm_pre[0], width=d, col_block=0, tm=tm, lo=True)
    mix = _hybrid_mixer(u_hi, u_lo, hy_w_in[0], dsa_q_norm[0], dsa_w_uq[0], dsa_w_qidx[0],
                        dsa_kv_norm[0], dsa_w_ukv[0], dsa_kidx_norm[0], gdn_conv_w[0],
                        gdn_a_log[0], gdn_dt_bias[0], gdn_o_norm[0], hy_w_out[0],
                        top_k=top_k, pad=pad, tm=tm, tm_wide=tm_wide)
    h, u = _resid_norm(h, mix, mix_norm_post[0], ffn_norm_pre[0], tm=tm)
    f = _ffn(u, ffn_w_gate, ffn_w_up, ffn_w_down, 0, tm=tm, tm_wide=tm_wide)
    h, u = _resid_norm(h, f, ffn_norm_post[0], mix_norm_pre[1], tm=tm)
    mix = _retention_mixer(u, ret_w_in[0], ret_w_out[0], pad=pad, tm=tm, tm_wide=tm_wide)
    h, u = _resid_norm(h, mix, mix_norm_post[1], ffn_norm_pre[1], tm=tm)
    f = _ffn(u, ffn_w_gate, ffn_w_up, ffn_w_down, 1, tm=tm, tm_wide=tm_wide)
    out = _resid_norm(h, f, ffn_norm_post[1], None, tm=ROW_ALIGN, row0=pad + N_META,
                      out_dtype=x.dtype)
    return out[None]
```

```python
import functools
import math

import jax
import jax.numpy as jnp
from jax import lax
from jax.experimental import pallas as pl
from jax.experimental.pallas import tpu as pltpu

F32 = jnp.float32
BF16 = jnp.bfloat16

N_META = 16
NORM_EPS = 1e-6

A_HEADS = 8
A_HEAD_DIM = 128
A_Q_RANK = 512
A_KV_RANK = 256
IDX_HEADS = 16
IDX_DIM = 64
TOPK_MAX = 256
TOPK_FRAC = 4

B_HEADS = 8
B_HEAD_DIM = 128
CONV_K = 4
GDN_CHUNK = 128
DSA_HEADS_PER_STEP = 2
DSA_VMEM_LIMIT = 56 * 1024 * 1024
DSA_SUB_BLOCK = 64

C_HEADS = 8
C_QK_DIM = 256
C_V_DIM = 512
ROT_BASE = 10000.0
RET_CHUNK = 256

LANES = 128
SUBLANES = 8
MXU_DIM = 256
ROW_ALIGN = MXU_DIM
ROW_TILE = 3 * MXU_DIM
WIDE_ROW_TILE = 11 * LANES
VMEM_LIMIT = 48 * 1024 * 1024

SM_WIDX = IDX_DIM
SM_BETA = SM_WIDX + IDX_HEADS
SM_A = SM_BETA + B_HEADS

INT_MIN = -2 ** 31
KEY_FLT_MAX = 0x7F7FFFFF
KEY_MIN_NORMAL = 0x00800000
KEY_FLT_LOWEST = (0xFF7FFFFF ^ 0x7FFFFFFF) - 2 ** 32


def _params(*sem, vmem=VMEM_LIMIT):
    return pltpu.CompilerParams(dimension_semantics=sem, vmem_limit_bytes=vmem)


def _hilo(x):
    hi = x.astype(BF16)
    return hi, (x - hi.astype(F32)).astype(BF16)


def _silu(x):
    return x * jax.nn.sigmoid(x)


def _mm_kernel(*refs, n_parts, epilogue):
    a, b, o = refs[:n_parts], refs[n_parts:2 * n_parts], refs[2 * n_parts:]
    a_hi, b_hi = a[0][...], b[0][...]
    acc = jnp.dot(a_hi, b_hi, preferred_element_type=F32)
    if n_parts == 2:
        acc += jnp.dot(a_hi, b[1][...], preferred_element_type=F32)
        acc += jnp.dot(a[1][...], b_hi, preferred_element_type=F32)
    if epilogue is None:
        o[0][...] = acc.astype(o[0].dtype)
    elif epilogue == "transposed":
        o[0][...] = acc.T
        o[1][...] = acc[:, acc.shape[1] - LANES:]
    elif epilogue == "split3":
        for g in range(acc.shape[0] // IDX_DIM):
            hi, lo = _hilo(acc[g * IDX_DIM:(g + 1) * IDX_DIM, :])
            base = g * MXU_DIM
            o[0][base:base + IDX_DIM, :] = hi
            o[0][base + IDX_DIM:base + 2 * IDX_DIM, :] = lo
            o[0][base + 2 * IDX_DIM:base + 3 * IDX_DIM, :] = hi
            o[0][base + 3 * IDX_DIM:base + MXU_DIM, :] = jnp.zeros_like(hi)


def _matmul(name, a_parts, b_parts, out_dtype, *, tm, tn, epilogue=None):
    m, kdim = a_parts[0].shape
    n = b_parts[0].shape[1]
    assert m % tm == 0 and n % tn == 0, (m, n, tm, tn)
    assert len(a_parts) == len(b_parts)
    if epilogue is None:
        out_shape = [jax.ShapeDtypeStruct((m, n), out_dtype)]
        out_specs = [pl.BlockSpec((tm, tn), lambda i, j: (i, j))]
    elif epilogue == "transposed":
        assert tn == n
        out_shape = [jax.ShapeDtypeStruct((n, m), out_dtype),
                     jax.ShapeDtypeStruct((m, LANES), out_dtype)]
        out_specs = [pl.BlockSpec((n, tm), lambda i, j: (0, i)),
                     pl.BlockSpec((tm, LANES), lambda i, j: (i, 0))]
    else:
        scale = MXU_DIM // IDX_DIM
        out_shape = [jax.ShapeDtypeStruct((m * scale, n), out_dtype)]
        out_specs = [pl.BlockSpec((tm * scale, tn), lambda i, j: (i, j))]
    out = pl.pallas_call(
        functools.partial(_mm_kernel, n_parts=len(a_parts), epilogue=epilogue),
        name=name,
        out_shape=out_shape,
        grid=(m // tm, n // tn),
        in_specs=[pl.BlockSpec((tm, kdim), lambda i, j: (i, 0))] * len(a_parts)
        + [pl.BlockSpec((kdim, tn), lambda i, j: (0, j))] * len(b_parts),
        out_specs=out_specs,
        compiler_params=_params("parallel", "parallel"),
    )(*a_parts, *b_parts)
    return out[0] if len(out) == 1 else out


def _mm_w32_kernel(*refs, n_a):
    a, w_ref, o_ref = refs[:n_a], refs[n_a], refs[n_a + 1]
    w = w_ref[...].astype(BF16)
    acc, k0 = None, 0
    for a_ref in a:
        k1 = k0 + a_ref.shape[1]
        part = jnp.dot(a_ref[...], w[k0:k1, :], preferred_element_type=F32)
        acc = part if acc is None else acc + part
        k0 = k1
    o_ref[...] = acc.astype(o_ref.dtype)


def _matmul_w32(name, a_list, w, layer, out_dtype, *, tm, tn, col0=0, n=None):
    m = a_list[0].shape[0]
    kdim = sum(a.shape[1] for a in a_list)
    n = w.shape[2] - col0 if n is None else n
    assert w.shape[1] == kdim and m % tm == 0 and n % tn == 0 and col0 % tn == 0
    jb = col0 // tn
    return pl.pallas_call(
        functools.partial(_mm_w32_kernel, n_a=len(a_list)),
        name=name,
        out_shape=jax.ShapeDtypeStruct((m, n), out_dtype),
        grid=(m // tm, n // tn),
        in_specs=[pl.BlockSpec((tm, a.shape[1]), lambda i, j: (i, 0)) for a in a_list]
        + [pl.BlockSpec((None, kdim, tn), lambda i, j: (layer, 0, j + jb))],
        out_specs=pl.BlockSpec((tm, tn), lambda i, j: (i, j)),
        compiler_params=_params("parallel", "parallel"),
    )(*a_list, w)


def _ffn_act_kernel(x_ref, wg_ref, wu_ref, o_ref):
    x = x_ref[...]
    g = jnp.dot(x, wg_ref[...].astype(BF16), preferred_element_type=F32)
    u = jnp.dot(x, wu_ref[...].astype(BF16), preferred_element_type=F32)
    o_ref[...] = (_silu(g) * u).astype(o_ref.dtype)


def _ffn_act(x, wg, wu, layer, *, tm, tn):
    m, kdim = x.shape
    n = wg.shape[2]
    assert m % tm == 0 and n % tn == 0
    w_spec = pl.BlockSpec((None, kdim, tn), lambda i, j: (layer, 0, j))
    return pl.pallas_call(
        _ffn_act_kernel,
        name="ffn_act",
        out_shape=jax.ShapeDtypeStruct((m, n), BF16),
        grid=(m // tm, n // tn),
        in_specs=[pl.BlockSpec((tm, kdim), lambda i, j: (i, 0)), w_spec, w_spec],
        out_specs=pl.BlockSpec((tm, tn), lambda i, j: (i, j)),
        compiler_params=_params("parallel", "parallel"),
    )(x, wg, wu)


def _rms_rows(x):
    return x * lax.rsqrt(jnp.mean(x * x, axis=-1, keepdims=True) + NORM_EPS)


def _rms_kernel(x_ref, g_ref, *o_refs):
    y = _rms_rows(x_ref[...].astype(F32)) * g_ref[...]
    hi = y.astype(BF16)
    o_refs[0][...] = hi
    if len(o_refs) == 2:
        o_refs[1][...] = (y - hi.astype(F32)).astype(BF16)


def _rms(x, gain, *, width, col_block, tm, lo=False):
    m = x.shape[0]
    n_out = 2 if lo else 1
    out = pl.pallas_call(
        _rms_kernel,
        name="rms",
        out_shape=[jax.ShapeDtypeStruct((m, width), BF16)] * n_out,
        grid=(m // tm,),
        in_specs=[pl.BlockSpec((tm, width), lambda i: (i, col_block)),
                  pl.BlockSpec((1, width), lambda i: (0, 0))],
        out_specs=[pl.BlockSpec((tm, width), lambda i: (i, 0))] * n_out,
        compiler_params=_params("parallel"),
    )(x, gain.reshape(1, width).astype(F32))
    return out if lo else out[0]


def _rms_both_kernel(x_ref, g_ref, o_ref, ot_ref):
    y = _rms_rows(x_ref[...]) * g_ref[...]
    o_ref[...] = y.astype(BF16)
    ot_ref[...] = y.T.astype(BF16)


def _rms_both(x, gain, *, width, col_block, tm):
    m = x.shape[0]
    return pl.pallas_call(
        _rms_both_kernel,
        name="rms_both",
        out_shape=[jax.ShapeDtypeStruct((m, width), BF16), jax.ShapeDtypeStruct((width, m), BF16)],
        grid=(m // tm,),
        in_specs=[pl.BlockSpec((tm, width), lambda i: (i, col_block)),
                  pl.BlockSpec((1, width), lambda i: (0, 0))],
        out_specs=[pl.BlockSpec((tm, width), lambda i: (i, 0)),
                   pl.BlockSpec((width, tm), lambda i: (0, i))],
        compiler_params=_params("parallel"),
    )(x, gain.reshape(1, width).astype(F32))


def _rms_t_kernel(x_ref, g_ref, *o_refs):
    x = x_ref[...]
    y = x * lax.rsqrt(jnp.mean(x * x, axis=0, keepdims=True) + NORM_EPS) * g_ref[...]
    hi = y.astype(BF16)
    o_refs[0][...] = hi
    if len(o_refs) == 2:
        o_refs[1][...] = (y - hi.astype(F32)).astype(BF16)


def _rms_t(xt, gain, *, width, row_block, tc, lo=False):
    t = xt.shape[1]
    n_out = 2 if lo else 1
    out = pl.pallas_call(
        _rms_t_kernel,
        name="rms_t",
        out_shape=[jax.ShapeDtypeStruct((width, t), BF16)] * n_out,
        grid=(t // tc,),
        in_specs=[pl.BlockSpec((width, tc), lambda i: (row_block, i)),
                  pl.BlockSpec((width, 1), lambda i: (0, 0))],
        out_specs=[pl.BlockSpec((width, tc), lambda i: (0, i))] * n_out,
        compiler_params=_params("parallel"),
    )(xt, gain.reshape(width, 1).astype(F32))
    return out if lo else out[0]


def _resid_kernel(h_ref, u_ref, gp_ref, *rest, with_next):
    hn = h_ref[...] + _rms_rows(u_ref[...].astype(F32)) * gp_ref[...]
    if with_next:
        gn_ref, hn_ref, un_ref = rest
        un_ref[...] = (_rms_rows(hn) * gn_ref[...]).astype(BF16)
    else:
        (hn_ref,) = rest
    hn_ref[...] = hn.astype(hn_ref.dtype)


def _resid_norm(h, u, g_post, g_next, *, tm, row0=0, out_dtype=F32):
    m, d = h.shape
    assert row0 % tm == 0 and m % tm == 0
    b0 = row0 // tm
    row_in = pl.BlockSpec((tm, d), lambda i: (i + b0, 0))
    row_out = pl.BlockSpec((tm, d), lambda i: (i, 0))
    vec = pl.BlockSpec((1, d), lambda i: (0, 0))
    with_next = g_next is not None
    gains = [g_post.reshape(1, d).astype(F32)]
    out_shape = [jax.ShapeDtypeStruct((m - row0, d), out_dtype)]
    if with_next:
        gains.append(g_next.reshape(1, d).astype(F32))
        out_shape.append(jax.ShapeDtypeStruct((m - row0, d), BF16))
    out = pl.pallas_call(
        functools.partial(_resid_kernel, with_next=with_next),
        name="resid_norm",
        out_shape=out_shape,
        grid=((m - row0) // tm,),
        in_specs=[row_in, row_in] + [vec] * len(gains),
        out_specs=[row_out] * len(out_shape),
        compiler_params=_params("parallel"),
    )(h, u, *gains)
    return out if with_next else out[0]


def _kidx_kernel(sm_ref, g_ref, o_ref):
    sm = sm_ref[...]
    lane = lax.broadcasted_iota(jnp.int32, sm.shape, 1)
    x = jnp.where(lane < IDX_DIM, sm, 0.0)
    ms = jnp.sum(x * x, axis=-1, keepdims=True) * (1.0 / IDX_DIM)
    y = x * lax.rsqrt(ms + NORM_EPS) * g_ref[...]
    hi = y.astype(BF16)
    lo = (y - hi.astype(F32)).astype(BF16)
    y2 = y + pltpu.roll(y, IDX_DIM, axis=1)
    o_ref[:, 0:LANES] = y2.astype(BF16)
    o_ref[:, LANES:2 * LANES] = lo


def _kidx(p1, gain, *, tm):
    m = p1.shape[0]
    g = jnp.zeros((1, LANES), F32).at[0, :IDX_DIM].set(gain.astype(F32))
    return pl.pallas_call(
        _kidx_kernel,
        name="dsa_kidx",
        out_shape=jax.ShapeDtypeStruct((m, 2 * LANES), BF16),
        grid=(m // tm,),
        in_specs=[pl.BlockSpec((tm, LANES), lambda i: (i, 0)),
                  pl.BlockSpec((1, LANES), lambda i: (0, 0))],
        out_specs=pl.BlockSpec((tm, 2 * LANES), lambda i: (i, 0)),
        compiler_params=_params("parallel"),
    )(p1, g)


def _dsa_kernel(qt_ref, qi3t_ref, wt_ref, k3_ref, k_ref, vt_ref, o_ref, sc_scr, sc16_scr, acc_scr,
                sa_scr, sb_scr, pa_scr, pb_scr, *, tq, tk, sb, top_k, pad, hps, w_scale,
                pos_bits):
    qi = pl.program_id(0)
    hd = pl.program_id(1)
    q0 = qi * tq
    nkt = (q0 + tq + tk - 1) // tk

    krel = lax.broadcasted_iota(jnp.int32, (tk, tq), 0)

    tp = k_ref.shape[0]

    @pl.when(hd == 0)
    def _select():
        sc_scr[tp:tp + tk, :] = jnp.full((tk, tq), -jnp.inf, F32)
        qpos = q0 + lax.broadcasted_iota(jnp.int32, (tk, tq), 1)

        def score_tile(kt, carry):
            ks = pl.multiple_of(kt * tk, tk)
            k3 = k3_ref[pl.ds(ks, tk), :]
            acc = jnp.zeros((tk, tq), F32)
            for ih in range(IDX_HEADS):
                d = jnp.dot(k3, qi3t_ref[ih * MXU_DIM:(ih + 1) * MXU_DIM, :],
                            preferred_element_type=F32)
                acc = acc + (wt_ref[ih:ih + 1, :] * w_scale) * jnp.maximum(d, 0.0)
            kpos = ks + krel
            ok = (kpos <= qpos) & (kpos >= pad)
            masked = jnp.where(ok, acc, -jnp.inf)
            sc_scr[pl.ds(ks, tk), :] = masked
            sc16_scr[pl.ds(ks, tk), :] = masked.astype(BF16)
            return carry

        lax.fori_loop(0, nkt, score_tile, 0)

        def count(pred):
            def body(kt, acc):
                ks = pl.multiple_of(kt * tk, tk)
                hit = pred(sc_scr[pl.ds(ks, tk), :], ks).astype(jnp.int32)
                return acc + jnp.sum(hit.reshape(tk // SUBLANES, SUBLANES, tq), axis=0)

            acc = lax.fori_loop(0, nkt, body, jnp.zeros((SUBLANES, tq), jnp.int32))
            return jnp.sum(acc, axis=0, keepdims=True)

        def key_to_f32(t):
            t = jnp.clip(t, jnp.int32(KEY_FLT_LOWEST), jnp.int32(KEY_FLT_MAX))
            return pltpu.bitcast(jnp.where(t >= 0, t, t ^ jnp.int32(0x7FFFFFFF)), F32)

        def count16(cand_b):
            rows = 2 * SUBLANES
            one, nil = jnp.ones((), BF16), jnp.zeros((), BF16)
            cand_b = jnp.maximum(cand_b, jnp.finfo(BF16).min)

            def body(kt, acc):
                ks = pl.multiple_of(kt * tk, tk)
                hit = jnp.where(sc16_scr[pl.ds(ks, tk), :] >= cand_b, one, nil)
                slabs = [hit[r:r + rows, :] for r in range(0, tk, rows)]
                while len(slabs) > 1:
                    pairs = [a + b for a, b in zip(slabs[0::2], slabs[1::2])]
                    slabs = pairs + slabs[2 * len(pairs):]
                return acc + slabs[0].astype(F32)

            acc = lax.fori_loop(0, nkt, body, jnp.zeros((rows, tq), F32))
            return jnp.sum(acc, axis=0, keepdims=True).astype(jnp.int32)

        zero = jnp.zeros((1, tq), jnp.int32)
        accept = lambda n, cand: (n >= top_k) & (cand <= jnp.int32(KEY_FLT_MAX))
        n0 = count16(jnp.zeros((1, tq), BF16))
        t_hi = jnp.where(n0 >= top_k, zero, jnp.int32(INT_MIN))

        def hi_body(i, t):
            cand = t + lax.shift_left(jnp.int32(1), 30 - i)
            n = count16(key_to_f32(cand).astype(BF16))
            return jnp.where(accept(n, cand), cand, t)

        t_hi = lax.fori_loop(0, 15, hi_body, t_hi)

        base = t_hi - jnp.int32(1 << 15)

        def lo_body(i, off):
            cand = base + (off | lax.shift_left(jnp.int32(1), 16 - i))
            cand_f = key_to_f32(cand)
            n = count(lambda s, ks: s >= cand_f)
            return jnp.where(accept(n, cand), off | lax.shift_left(jnp.int32(1), 16 - i), off)

        off = lax.fori_loop(0, 17, lo_body, zero)
        t = jnp.where(t_hi == jnp.int32(INT_MIN), t_hi, base + off)
        thr = key_to_f32(t)

        n_ge = count(lambda s, ks: s >= thr)

        @pl.when(jnp.max(n_ge) > top_k)
        def _ties():
            flushed = (t >= jnp.int32(-KEY_MIN_NORMAL)) & (t < jnp.int32(KEY_MIN_NORMAL))
            thr_next = key_to_f32(jnp.where(flushed, jnp.int32(KEY_MIN_NORMAL), t + 1))
            tie = lambda s: (s >= thr) & jnp.logical_not(s >= thr_next)
            want = top_k - count(lambda s, ks: s >= thr_next)

            def pos_body(i, cut):
                cand = cut + lax.shift_left(jnp.int32(1), pos_bits - 1 - i)
                n = count(lambda s, ks: tie(s) & (ks + krel < cand))
                return jnp.where(n < want, cand, cut)

            cut = lax.fori_loop(0, pos_bits, pos_body, zero)

            def drop_tile(kt, carry):
                ks = pl.multiple_of(kt * tk, tk)
                s = sc_scr[pl.ds(ks, tk), :]
                sc_scr[pl.ds(ks, tk), :] = jnp.where(tie(s) & (ks + krel > cut), -jnp.inf, s)
                return carry

            lax.fori_loop(0, nkt, drop_tile, 0)

        def bias_tile(kt, carry):
            ks = pl.multiple_of(kt * tk, tk)
            sc_scr[pl.ds(ks, tk), :] = jnp.where(sc_scr[pl.ds(ks, tk), :] >= thr, 0.0, -jnp.inf)
            return carry

        lax.fori_loop(0, nkt, bias_tile, 0)

    hd_sl = [slice(j * A_HEAD_DIM, (j + 1) * A_HEAD_DIM) for j in range(hps)]
    q = [qt_ref[sl, :] for sl in hd_sl]

    def logits_into(kt, s_ref):
        ks = pl.multiple_of(kt * tk, tk)
        for j in range(hps):
            s_ref[j] = jnp.dot(k_ref[pl.ds(ks, tk), hd_sl[j]], q[j], preferred_element_type=F32)

    fold = lambda x: x.reshape(sb // SUBLANES, SUBLANES, tq)

    def att_tile(kt, kt_next, bias_row, s_cur, s_next, p_ref, carry):
        logits_into(kt_next, s_next)
        ks = pl.multiple_of(kt * tk, tk)

        def masked(j, i):
            r = i * sb
            return s_cur[j, r:r + sb, :] + sc_scr[pl.ds(bias_row + r, sb), :]

        out, alphas = [], []
        for j in range(hps):
            m, l = carry[j]
            mx = jnp.full((SUBLANES, tq), -jnp.inf, F32)
            for i in range(tk // sb):
                mx = jnp.maximum(mx, jnp.max(fold(masked(j, i)), axis=0))
            m_new = jnp.maximum(m, jnp.max(mx, axis=0, keepdims=True))
            m_safe = jnp.where(m_new == -jnp.inf, 0.0, m_new)
            ls = jnp.zeros((SUBLANES, tq), F32)
            for i in range(tk // sb):
                p = jnp.exp2(masked(j, i) - m_safe)
                p_ref[j, i * sb:(i + 1) * sb, :] = p.astype(BF16)
                ls = ls + jnp.sum(fold(p), axis=0)
            alpha = jnp.exp2(m - m_safe)
            out.append((m_new, alpha * l + jnp.sum(ls, axis=0, keepdims=True)))
            alphas.append(alpha)
        pv = [jnp.dot(vt_ref[hd_sl[j], pl.ds(ks, tk)], p_ref[j], preferred_element_type=F32)
              for j in range(hps)]
        for j in range(hps):
            acc_scr[j] = alphas[j] * acc_scr[j] + pv[j]
        return tuple(out)

    def att_pair(i, carry):
        kt0 = 2 * i
        kt1 = jnp.minimum(kt0 + 1, nkt - 1)
        kt2 = jnp.minimum(kt0 + 2, nkt - 1)
        row1 = pl.multiple_of(jnp.where(kt0 + 1 < nkt, kt1 * tk, tp), tk)
        carry = att_tile(kt0, kt1, pl.multiple_of(kt0 * tk, tk), sa_scr, sb_scr, pa_scr, carry)
        return att_tile(kt1, kt2, row1, sb_scr, sa_scr, pb_scr, carry)

    acc_scr[...] = jnp.zeros_like(acc_scr)
    logits_into(0, sa_scr)
    init = (jnp.full((1, tq), -jnp.inf, F32), jnp.zeros((1, tq), F32))
    res = lax.fori_loop(0, (nkt + 1) // 2, att_pair, (init,) * hps)
    for j in range(hps):
        l = res[j][1]
        o_ref[:, hd_sl[j]] = jnp.where(l > 0.0, acc_scr[j] / l, 0.0).T.astype(o_ref.dtype)


def _dsa(qt, qi3t, wt, k3, k, vt, *, top_k, pad, tq, tk):
    tp = k.shape[0]
    assert tp % tq == 0 and tp % tk == 0 and tk >= top_k
    hps = DSA_HEADS_PER_STEP
    hw = hps * A_HEAD_DIM
    kern = functools.partial(
        _dsa_kernel, tq=tq, tk=tk, sb=DSA_SUB_BLOCK, top_k=top_k, pad=pad, hps=hps,
        w_scale=(IDX_HEADS * IDX_DIM) ** -0.5, pos_bits=tp.bit_length())
    return pl.pallas_call(
        kern,
        name="dsa",
        out_shape=jax.ShapeDtypeStruct((tp, A_HEADS * A_HEAD_DIM), BF16),
        grid=(tp // tq, A_HEADS // hps),
        in_specs=[
            pl.BlockSpec((hw, tq), lambda i, h: (h, i)),
            pl.BlockSpec((IDX_HEADS * MXU_DIM, tq), lambda i, h: (0, i)),
            pl.BlockSpec((IDX_HEADS, tq), lambda i, h: (0, i)),
            pl.BlockSpec((tp, MXU_DIM), lambda i, h: (0, 0)),
            pl.BlockSpec((tp, hw), lambda i, h: (0, h)),
            pl.BlockSpec((hw, tp), lambda i, h: (h, 0)),
        ],
        out_specs=pl.BlockSpec((tq, hw), lambda i, h: (i, h)),
        scratch_shapes=[pltpu.VMEM((tp + tk, tq), F32), pltpu.VMEM((tp, tq), BF16),
                        pltpu.VMEM((hps, A_HEAD_DIM, tq), F32)]
        + [pltpu.VMEM((hps, tk, tq), F32)] * 2 + [pltpu.VMEM((hps, tk, tq), BF16)] * 2,
        compiler_params=_params("arbitrary", "arbitrary", vmem=DSA_VMEM_LIMIT),
    )(qt, qi3t, wt, k3, k, vt)


def _dot3(a, b, dims=(((1,), (0,)), ((), ()))):
    ah, al = _hilo(a)
    bh, bl = _hilo(b)
    f = lambda x, y: lax.dot_general(x, y, dims, preferred_element_type=F32)
    return f(ah, bh) + f(ah, bl) + f(al, bh)


_NN = (((1,), (0,)), ((), ()))
_NT = (((1,), (1,)), ((), ()))
_TN = (((0,), (0,)), ((), ()))


def _mm(a, b, dims=_NN):
    return lax.dot_general(a.astype(BF16), b.astype(BF16), dims, preferred_element_type=F32)


def _gdn_kernel(qkv_ref, z_ref, sm_ref, convw_ref, alog_ref, dtb_ref, onorm_ref, o_ref,
                xbuf, s_scr, *, pad):
    c = pl.program_id(0)
    ch = GDN_CHUNK
    hist = SUBLANES

    @pl.when(c == 0)
    def _():
        xbuf[0:hist, :] = jnp.zeros((hist, xbuf.shape[1]), F32)
        s_scr[...] = jnp.zeros_like(s_scr)

    x = qkv_ref[...]
    xbuf[hist:hist + ch, :] = x
    w = convw_ref[...]
    y = x * w[CONV_K - 1:CONV_K, :]
    for j in range(CONV_K - 1):
        y = y + xbuf[hist - (CONV_K - 1) + j:hist - (CONV_K - 1) + j + ch, :] * w[j:j + 1, :]
    xbuf[0:hist, :] = x[ch - hist:ch, :]
    y = _silu(y)

    sm = sm_ref[...]
    row = c * ch + lax.broadcasted_iota(jnp.int32, sm.shape, 0)
    beta = jax.nn.sigmoid(sm)
    zsm = sm + dtb_ref[...]
    softplus = jnp.maximum(zsm, 0.0) + jnp.log(1.0 + jnp.exp(-jnp.abs(zsm)))
    g = jnp.where(row >= pad, -jnp.exp(alog_ref[...]) * softplus, 0.0)

    ri = lax.broadcasted_iota(jnp.int32, (ch, ch), 0)
    ci = lax.broadcasted_iota(jnp.int32, (ch, ch), 1)
    lower = ri >= ci
    strict = ri > ci
    tri = jnp.where(lower, 1.0, 0.0).astype(BF16)
    g1 = g.astype(BF16)
    r1 = g - g1.astype(F32)
    g2 = r1.astype(BF16)
    g3 = (r1 - g2.astype(F32)).astype(BF16)
    gcum = (jnp.dot(tri, g1, preferred_element_type=F32)
            + jnp.dot(tri, g2, preferred_element_type=F32)
            + jnp.dot(tri, g3, preferred_element_type=F32))
    gcum_t = gcum.T
    eye = jnp.where(ri == ci, 1.0, 0.0).astype(F32)

    bw = B_HEADS * B_HEAD_DIM
    hs = range(B_HEADS)
    sl = [slice(h * B_HEAD_DIM, (h + 1) * B_HEAD_DIM) for h in hs]
    part = lambda j: [y[:, j * bw + h * B_HEAD_DIM:j * bw + (h + 1) * B_HEAD_DIM] for h in hs]
    l2n = lambda x: x * lax.rsqrt(jnp.sum(x * x, axis=-1, keepdims=True) + NORM_EPS)
    q = [l2n(x) * B_HEAD_DIM ** -0.5 for x in part(0)]
    k = [l2n(x) for x in part(1)]
    v = part(2)
    b = [beta[:, SM_BETA + h:SM_BETA + h + 1] for h in hs]
    gc = [gcum[:, SM_A + h:SM_A + h + 1] for h in hs]
    gr = [gcum_t[SM_A + h:SM_A + h + 1, :] for h in hs]
    gl = [x[ch - 1:ch, :] for x in gc]
    eg = [jnp.exp(x) for x in gc]
    decay = [jnp.where(lower, jnp.exp(jnp.where(lower, gc[h] - gr[h], 0.0)), 0.0) for h in hs]
    kb = [k[h] * b[h] for h in hs]
    a_mat = [jnp.where(strict, _dot3(kb[h], k[h], _NT) * decay[h], 0.0) for h in hs]
    t_mat = [eye - a for a in a_mat]
    pw = a_mat
    for _ in range(ch.bit_length() - 2):
        pw = [_dot3(p, p) for p in pw]
        t_mat = [t_mat[h] + _dot3(t_mat[h], pw[h]) for h in hs]
    u = [_mm(t_mat[h], v[h] * b[h]) for h in hs]
    wm = [_mm(t_mat[h], kb[h] * eg[h]) for h in hs]
    qk = [jnp.where(lower, _mm(q[h], k[h], _NT) * decay[h], 0.0) for h in hs]
    state = [s_scr[h] for h in hs]
    v_new = [u[h] - _mm(wm[h], state[h]) for h in hs]
    o = [_mm(q[h] * eg[h], state[h]) + _mm(qk[h], v_new[h]) for h in hs]
    kd = [k[h] * jnp.exp(gl[h] - gc[h]) for h in hs]
    new_state = [state[h] * jnp.exp(gl[h]) + _mm(kd[h], v_new[h], _TN) for h in hs]
    for h in hs:
        s_scr[h] = new_state[h]
    for h in hs:
        o_h = _rms_rows(o[h]) * onorm_ref[...] * _silu(z_ref[:, sl[h]])
        o_ref[:, sl[h]] = o_h.astype(o_ref.dtype)


def _gdn(p2, p1, conv_w, a_log, dt_bias, o_norm, *, pad):
    tp = p2.shape[0]
    bw = B_HEADS * B_HEAD_DIM
    alog = jnp.zeros((1, LANES), F32).at[0, SM_A:SM_A + B_HEADS].set(a_log.astype(F32))
    dtb = jnp.zeros((1, LANES), F32).at[0, SM_A:SM_A + B_HEADS].set(dt_bias.astype(F32))
    ch = GDN_CHUNK
    return pl.pallas_call(
        functools.partial(_gdn_kernel, pad=pad),
        name="gdn",
        out_shape=jax.ShapeDtypeStruct((tp, bw), BF16),
        grid=(tp // ch,),
        in_specs=[
            pl.BlockSpec((ch, 3 * bw), lambda c: (c, 0)),
            pl.BlockSpec((ch, bw), lambda c: (c, 3)),
            pl.BlockSpec((ch, LANES), lambda c: (c, 0)),
            pl.BlockSpec((CONV_K, 3 * bw), lambda c: (0, 0)),
            pl.BlockSpec((1, LANES), lambda c: (0, 0)),
            pl.BlockSpec((1, LANES), lambda c: (0, 0)),
            pl.BlockSpec((1, B_HEAD_DIM), lambda c: (0, 0)),
        ],
        out_specs=pl.BlockSpec((ch, bw), lambda c: (c, 0)),
        scratch_shapes=[pltpu.VMEM((SUBLANES + ch, 3 * bw), F32),
                        pltpu.VMEM((B_HEADS, B_HEAD_DIM, B_HEAD_DIM), F32)],
        compiler_params=_params("arbitrary"),
    )(p2, p2, p1, conv_w.astype(F32), alog, dtb, o_norm.reshape(1, B_HEAD_DIM).astype(F32))


def _ret_kernel(q_ref, k_ref, v_ref, g_ref, cos_ref, sin_ref, dmask_ref, qdec_ref, kdec_ref,
                cdec_ref, o_ref, r_scr):
    @pl.when(pl.program_id(1) == 0)
    def _():
        r_scr[...] = jnp.zeros_like(r_scr)

    cos, sin = cos_ref[...], sin_ref[...]
    even = lax.broadcasted_iota(jnp.int32, cos.shape, 1) % 2 == 0

    def rot(x):
        partner = jnp.where(even, pltpu.roll(x, C_QK_DIM - 1, axis=1), pltpu.roll(x, 1, axis=1))
        return x * cos + partner * sin

    q = rot(q_ref[...].astype(F32))
    k = rot(k_ref[...].astype(F32)) * C_QK_DIM ** -0.5
    v = v_ref[...]
    s = lax.dot_general(q.astype(BF16), k.astype(BF16), _NT, preferred_element_type=F32)
    s = s * dmask_ref[...]
    r = r_scr[...]
    o = jnp.dot(s.astype(BF16), v, preferred_element_type=F32)
    o = o + jnp.dot((q * qdec_ref[...]).astype(BF16), r.astype(BF16), preferred_element_type=F32)
    kd = (k * kdec_ref[...]).astype(BF16)
    r_scr[...] = r * cdec_ref[...] + lax.dot_general(kd, v, _TN, preferred_element_type=F32)
    gate = g_ref[...].astype(F32)
    o_ref[...] = (_silu(gate) * _rms_rows(o)).astype(o_ref.dtype)


def _retention(r_all, cos, sin):
    tp = r_all.shape[0]
    ch = RET_CHUNK
    v0 = 2 * C_HEADS * C_QK_DIM // C_V_DIM
    log_gamma = jnp.log(1.0 - 2.0 ** (-5.0 - jnp.arange(C_HEADS, dtype=F32)))
    pos = jnp.arange(ch, dtype=F32)
    rel = pos[:, None] - pos[None, :]
    dmask = jnp.where(rel >= 0, jnp.exp(jnp.maximum(rel, 0.0)[None] * log_gamma[:, None, None]), 0.0)
    qdec = jnp.exp((pos + 1.0)[None, :] * log_gamma[:, None])[:, :, None]
    kdec = jnp.exp((ch - 1.0 - pos)[None, :] * log_gamma[:, None])[:, :, None]
    cdec = jnp.broadcast_to(jnp.exp(ch * log_gamma)[:, None, None], (C_HEADS, 1, C_V_DIM))
    return pl.pallas_call(
        _ret_kernel,
        name="retention",
        out_shape=jax.ShapeDtypeStruct((tp, C_HEADS * C_V_DIM), BF16),
        grid=(C_HEADS, tp // ch),
        in_specs=[
            pl.BlockSpec((ch, C_QK_DIM), lambda h, c: (c, h)),
            pl.BlockSpec((ch, C_QK_DIM), lambda h, c: (c, C_HEADS + h)),
            pl.BlockSpec((ch, C_V_DIM), lambda h, c: (c, v0 + h)),
            pl.BlockSpec((ch, C_V_DIM), lambda h, c: (c, v0 + C_HEADS + h)),
            pl.BlockSpec((ch, C_QK_DIM), lambda h, c: (c, 0)),
            pl.BlockSpec((ch, C_QK_DIM), lambda h, c: (c, 0)),
            pl.BlockSpec((None, ch, ch), lambda h, c: (h, 0, 0)),
            pl.BlockSpec((None, ch, 1), lambda h, c: (h, 0, 0)),
            pl.BlockSpec((None, ch, 1), lambda h, c: (h, 0, 0)),
            pl.BlockSpec((None, 1, C_V_DIM), lambda h, c: (h, 0, 0)),
        ],
        out_specs=pl.BlockSpec((ch, C_V_DIM), lambda h, c: (c, h)),
        scratch_shapes=[pltpu.VMEM((C_QK_DIM, C_V_DIM), F32)],
        compiler_params=_params("arbitrary", "arbitrary"),
    )(r_all, r_all, r_all, r_all, cos, sin, dmask, qdec, kdec, cdec)


def _hybrid_mixer(u_hi, u_lo, w_in, q_norm, w_uq, w_qidx, kv_norm, w_ukv, kidx_norm,
                  conv_w, a_log, dt_bias, o_norm, w_out, *, top_k, pad, tm, tm_wide):
    tp = u_hi.shape[0]
    o = [0]
    for s in (A_Q_RANK, A_KV_RANK, IDX_DIM, IDX_HEADS, 3 * B_HEADS * B_HEAD_DIM,
              B_HEADS * B_HEAD_DIM, B_HEADS, B_HEADS):
        o.append(o[-1] + s)
    col = lambda i: w_in[:, o[i]:o[i + 1]]
    n_small = IDX_DIM + IDX_HEADS + 2 * B_HEADS
    w1 = jnp.concatenate([col(0), col(2), col(3), col(6), col(7),
                          jnp.zeros((w_in.shape[0], LANES - n_small), w_in.dtype)], axis=1)
    w2 = jnp.concatenate([col(4), col(5), col(1)], axis=1).astype(BF16)
    p1t, small = _matmul("hy_in_x3", (u_hi, u_lo), _hilo(w1), F32, tm=tm, tn=w1.shape[1],
                         epilogue="transposed")
    p2 = _matmul("hy_in", (u_hi,), (w2,), F32, tm=tm_wide, tn=MXU_DIM)

    cqt_hi, cqt_lo = _rms_t(p1t, q_norm, width=A_Q_RANK, row_block=0, tc=tm, lo=True)
    qi3t = _matmul("dsa_qidx", _hilo(w_qidx.T), (cqt_hi, cqt_lo), BF16, tm=MXU_DIM, tn=tm,
                   epilogue="split3")
    q_scale = A_HEAD_DIM ** -0.5 * math.log2(math.e)
    qt = _matmul("dsa_q", ((w_uq.T * q_scale).astype(BF16),), (cqt_hi,), BF16, tm=MXU_DIM, tn=tm)
    k3 = _kidx(small, kidx_norm, tm=tm)
    wt = p1t[A_Q_RANK + SM_WIDX:A_Q_RANK + SM_BETA]
    ckv_block = (p2.shape[1] - A_KV_RANK) // A_KV_RANK
    kvn, kvnt = _rms_both(p2, kv_norm, width=A_KV_RANK, col_block=ckv_block, tm=tm)
    w_kv = w_ukv.reshape(A_KV_RANK, A_HEADS, 2, A_HEAD_DIM)
    w_k = w_kv[:, :, 0].reshape(A_KV_RANK, A_HEADS * A_HEAD_DIM).astype(BF16)
    w_vt = w_kv[:, :, 1].reshape(A_KV_RANK, A_HEADS * A_HEAD_DIM).T.astype(BF16)
    k = _matmul("dsa_k", (kvn,), (w_k,), BF16, tm=tm, tn=MXU_DIM)
    vt = _matmul("dsa_vt", (w_vt,), (kvnt,), BF16, tm=MXU_DIM, tn=tm)
    o_a = _dsa(qt, qi3t, wt, k3, k, vt, top_k=top_k, pad=pad, tq=MXU_DIM, tk=tm)

    o_b = _gdn(p2, small, conv_w, a_log, dt_bias, o_norm, pad=pad)

    return _matmul_w32("hy_out", [o_a, o_b], w_out[None], 0, BF16, tm=tm_wide, tn=2 * MXU_DIM)


def _retention_mixer(u, w_in, w_out, *, pad, tm, tm_wide):
    tp = u.shape[0]
    half = C_QK_DIM // 2
    r_all = _matmul_w32("ret_in", [u], w_in[None], 0, BF16, tm=tm_wide, tn=2 * MXU_DIM)
    inv = 1.0 / (ROT_BASE ** jnp.linspace(0.0, 1.0, half, dtype=F32))
    posn = (jnp.arange(tp) - pad).astype(F32)
    ang = posn[:, None] * jnp.repeat(inv, 2)[None, :]
    sign = jnp.tile(jnp.array([-1.0, 1.0], F32), half)
    og = _retention(r_all, jnp.cos(ang), jnp.sin(ang) * sign[None, :])
    return _matmul_w32("ret_out", [og], w_out[None], 0, BF16, tm=tm, tn=2 * MXU_DIM)


def _ffn(u, w_gate, w_up, w_down, layer, *, tm, tm_wide):
    act = _ffn_act(u, w_gate, w_up, layer, tm=tm_wide, tn=2 * MXU_DIM)
    return _matmul_w32("ffn_down", [act], w_down, layer, BF16, tm=tm, tn=MXU_DIM)


def kernel(x, meta_tokens, mix_norm_pre, mix_norm_post, ffn_norm_pre, ffn_norm_post, hy_w_in, dsa_q_norm, dsa_w_uq, dsa_w_qidx, dsa_kv_norm, dsa_w_ukv, dsa_kidx_norm, gdn_conv_w, gdn_a_log, gdn_dt_bias, gdn_o_norm, hy_w_out, ret_w_in, ret_w_out, ffn_w_gate, ffn_w_up, ffn_w_down):
    b, seq, d = x.shape
    assert b == 1 and mix_norm_pre.shape[0] == 2
    pad = (-N_META) % ROW_ALIGN
    tp = pad + N_META + seq
    tm = ROW_TILE
    assert tp % tm == 0
    tm_wide = WIDE_ROW_TILE if tp % WIDE_ROW_TILE == 0 else tm
    top_k = min(TOPK_MAX, seq // TOPK_FRAC)
    h = jnp.concatenate([jnp.zeros((pad, d), F32), meta_tokens.astype(F32), x[0].astype(F32)], axis=0)

    u_hi, u_lo = _rms(h, mix_norm_pre[0], width=d, col_block=0, tm=tm, lo=True)
    mix = _hybrid_mixer(u_hi, u_lo, hy_w_in[0], dsa_q_norm[0], dsa_w_uq[0], dsa_w_qidx[0],
                        dsa_kv_norm[0], dsa_w_ukv[0], dsa_kidx_norm[0], gdn_conv_w[0],
                        gdn_a_log[0], gdn_dt_bias[0], gdn_o_norm[0], hy_w_out[0],
                        top_k=top_k, pad=pad, tm=tm, tm_wide=tm_wide)
    h, u = _resid_norm(h, mix, mix_norm_post[0], ffn_norm_pre[0], tm=tm)
    f = _ffn(u, ffn_w_gate, ffn_w_up, ffn_w_down, 0, tm=tm, tm_wide=tm_wide)
    h, u = _resid_norm(h, f, ffn_norm_post[0], mix_norm_pre[1], tm=tm)
    mix = _retention_mixer(u, ret_w_in[0], ret_w_out[0], pad=pad, tm=tm, tm_wide=tm_wide)
    h, u = _resid_norm(h, mix, mix_norm_post[1], ffn_norm_pre[1], tm=tm)
    f = _ffn(u, ffn_w_gate, ffn_w_up, ffn_w_down, 1, tm=tm, tm_wide=tm_wide)
    out = _resid_norm(h, f, ffn_norm_post[1], None, tm=ROW_ALIGN, row0=pad + N_META,
                      out_dtype=x.dtype)
    return out[None]
```

```python
import functools
import math

import jax
import jax.numpy as jnp
from jax import lax
from jax.experimental import pallas as pl
from jax.experimental.pallas import tpu as pltpu

F32 = jnp.float32
BF16 = jnp.bfloat16

N_META = 16
NORM_EPS = 1e-6

A_HEADS = 8
A_HEAD_DIM = 128
A_Q_RANK = 512
A_KV_RANK = 256
IDX_HEADS = 16
IDX_DIM = 64
TOPK_MAX = 256
TOPK_FRAC = 4

B_HEADS = 8
B_HEAD_DIM = 128
CONV_K = 4
GDN_CHUNK = 128
DSA_HEADS_PER_STEP = 2
DSA_VMEM_LIMIT = 56 * 1024 * 1024
BIG_TILE_VMEM_LIMIT = 56 * 1024 * 1024
DSA_SUB_BLOCK = 64

C_HEADS = 8
C_QK_DIM = 256
C_V_DIM = 512
ROT_BASE = 10000.0
RET_CHUNK = 256

LANES = 128
SUBLANES = 8
MXU_DIM = 256
ROW_ALIGN = MXU_DIM
ROW_TILE = 3 * MXU_DIM
WIDE_ROW_TILE = 11 * LANES
VMEM_LIMIT = 48 * 1024 * 1024

SM_WIDX = IDX_DIM
SM_BETA = SM_WIDX + IDX_HEADS
SM_A = SM_BETA + B_HEADS

INT_MIN = -2 ** 31
KEY_FLT_MAX = 0x7F7FFFFF
KEY_MIN_NORMAL = 0x00800000
KEY_FLT_LOWEST = (0xFF7FFFFF ^ 0x7FFFFFFF) - 2 ** 32


def _params(*sem, vmem=VMEM_LIMIT):
    return pltpu.CompilerParams(dimension_semantics=sem, vmem_limit_bytes=vmem)


def _hilo(x):
    hi = x.astype(BF16)
    return hi, (x - hi.astype(F32)).astype(BF16)


def _silu(x):
    return x * jax.nn.sigmoid(x)


def _mm_kernel(*refs, n_parts, epilogue):
    a, b, o = refs[:n_parts], refs[n_parts:2 * n_parts], refs[2 * n_parts:]
    a_hi, b_hi = a[0][...], b[0][...]
    acc = jnp.dot(a_hi, b_hi, preferred_element_type=F32)
    if n_parts == 2:
        acc += jnp.dot(a_hi, b[1][...], preferred_element_type=F32)
        acc += jnp.dot(a[1][...], b_hi, preferred_element_type=F32)
    if epilogue is None:
        o[0][...] = acc.astype(o[0].dtype)
    elif epilogue == "transposed":
        o[0][...] = acc.T
        o[1][...] = acc[:, acc.shape[1] - LANES:]
    elif epilogue == "split3":
        for g in range(acc.shape[0] // IDX_DIM):
            hi, lo = _hilo(acc[g * IDX_DIM:(g + 1) * IDX_DIM, :])
            base = g * MXU_DIM
            o[0][base:base + IDX_DIM, :] = hi
            o[0][base + IDX_DIM:base + 2 * IDX_DIM, :] = lo
            o[0][base + 2 * IDX_DIM:base + 3 * IDX_DIM, :] = hi
            o[0][base + 3 * IDX_DIM:base + MXU_DIM, :] = jnp.zeros_like(hi)


def _matmul(name, a_parts, b_parts, out_dtype, *, tm, tn, epilogue=None):
    m, kdim = a_parts[0].shape
    n = b_parts[0].shape[1]
    assert m % tm == 0 and n % tn == 0, (m, n, tm, tn)
    assert len(a_parts) == len(b_parts)
    if epilogue is None:
        out_shape = [jax.ShapeDtypeStruct((m, n), out_dtype)]
        out_specs = [pl.BlockSpec((tm, tn), lambda i, j: (i, j))]
    elif epilogue == "transposed":
        assert tn == n
        out_shape = [jax.ShapeDtypeStruct((n, m), out_dtype),
                     jax.ShapeDtypeStruct((m, LANES), out_dtype)]
        out_specs = [pl.BlockSpec((n, tm), lambda i, j: (0, i)),
                     pl.BlockSpec((tm, LANES), lambda i, j: (i, 0))]
    else:
        scale = MXU_DIM // IDX_DIM
        out_shape = [jax.ShapeDtypeStruct((m * scale, n), out_dtype)]
        out_specs = [pl.BlockSpec((tm * scale, tn), lambda i, j: (i, j))]
    out = pl.pallas_call(
        functools.partial(_mm_kernel, n_parts=len(a_parts), epilogue=epilogue),
        name=name,
        out_shape=out_shape,
        grid=(m // tm, n // tn),
        in_specs=[pl.BlockSpec((tm, kdim), lambda i, j: (i, 0))] * len(a_parts)
        + [pl.BlockSpec((kdim, tn), lambda i, j: (0, j))] * len(b_parts),
        out_specs=out_specs,
        compiler_params=_params("parallel", "parallel"),
    )(*a_parts, *b_parts)
    return out[0] if len(out) == 1 else out


def _mm_w32_kernel(*refs, n_a):
    a, w_ref, o_ref = refs[:n_a], refs[n_a], refs[n_a + 1]
    w = w_ref[...].astype(BF16)
    acc, k0 = None, 0
    for a_ref in a:
        k1 = k0 + a_ref.shape[1]
        part = jnp.dot(a_ref[...], w[k0:k1, :], preferred_element_type=F32)
        acc = part if acc is None else acc + part
        k0 = k1
    o_ref[...] = acc.astype(o_ref.dtype)


def _matmul_w32(name, a_list, w, layer, out_dtype, *, tm, tn, col0=0, n=None, vmem=VMEM_LIMIT):
    m = a_list[0].shape[0]
    kdim = sum(a.shape[1] for a in a_list)
    n = w.shape[2] - col0 if n is None else n
    assert w.shape[1] == kdim and m % tm == 0 and n % tn == 0 and col0 % tn == 0
    jb = col0 // tn
    return pl.pallas_call(
        functools.partial(_mm_w32_kernel, n_a=len(a_list)),
        name=name,
        out_shape=jax.ShapeDtypeStruct((m, n), out_dtype),
        grid=(m // tm, n // tn),
        in_specs=[pl.BlockSpec((tm, a.shape[1]), lambda i, j: (i, 0)) for a in a_list]
        + [pl.BlockSpec((None, kdim, tn), lambda i, j: (layer, 0, j + jb))],
        out_specs=pl.BlockSpec((tm, tn), lambda i, j: (i, j)),
        compiler_params=_params("parallel", "parallel", vmem=vmem),
    )(*a_list, w)


def _ffn_act_kernel(x_ref, wg_ref, wu_ref, o_ref):
    x = x_ref[...]
    g = jnp.dot(x, wg_ref[...].astype(BF16), preferred_element_type=F32)
    u = jnp.dot(x, wu_ref[...].astype(BF16), preferred_element_type=F32)
    o_ref[...] = (_silu(g) * u).astype(o_ref.dtype)


def _ffn_act(x, wg, wu, layer, *, tm, tn):
    m, kdim = x.shape
    n = wg.shape[2]
    assert m % tm == 0 and n % tn == 0
    w_spec = pl.BlockSpec((None, kdim, tn), lambda i, j: (layer, 0, j))
    return pl.pallas_call(
        _ffn_act_kernel,
        name="ffn_act",
        out_shape=jax.ShapeDtypeStruct((m, n), BF16),
        grid=(m // tm, n // tn),
        in_specs=[pl.BlockSpec((tm, kdim), lambda i, j: (i, 0)), w_spec, w_spec],
        out_specs=pl.BlockSpec((tm, tn), lambda i, j: (i, j)),
        compiler_params=_params("parallel", "parallel"),
    )(x, wg, wu)


def _rms_rows(x):
    return x * lax.rsqrt(jnp.mean(x * x, axis=-1, keepdims=True) + NORM_EPS)


def _embed_kernel(x_ref, head_ref, g_ref, h_ref, hi_ref, lo_ref):
    h = jnp.where(pl.program_id(0) == 0, head_ref[...], x_ref[...].astype(F32))
    h_ref[...] = h
    y = _rms_rows(h) * g_ref[...]
    hi = y.astype(BF16)
    hi_ref[...] = hi
    lo_ref[...] = (y - hi.astype(F32)).astype(BF16)


def _embed_norm(x, meta, gain, *, pad):
    seq, d = x.shape
    tb = pad + meta.shape[0]
    assert seq % tb == 0
    head = jnp.concatenate([jnp.zeros((pad, d), F32), meta.astype(F32)], axis=0)
    tp = tb + seq
    row = pl.BlockSpec((tb, d), lambda i: (i, 0))
    return pl.pallas_call(
        _embed_kernel,
        name="embed_norm",
        out_shape=[jax.ShapeDtypeStruct((tp, d), F32)] + [jax.ShapeDtypeStruct((tp, d), BF16)] * 2,
        grid=(tp // tb,),
        in_specs=[pl.BlockSpec((tb, d), lambda i: (jnp.maximum(i - 1, 0), 0)),
                  pl.BlockSpec((tb, d), lambda i: (0, 0)),
                  pl.BlockSpec((1, d), lambda i: (0, 0))],
        out_specs=[row, row, row],
        compiler_params=_params("parallel"),
    )(x, head, gain.reshape(1, d).astype(F32))


def _rms_both_kernel(x_ref, g_ref, o_ref, ot_ref):
    y = _rms_rows(x_ref[...]) * g_ref[...]
    o_ref[...] = y.astype(BF16)
    ot_ref[...] = y.T.astype(BF16)


def _rms_both(x, gain, *, width, col_block, tm):
    m = x.shape[0]
    return pl.pallas_call(
        _rms_both_kernel,
        name="rms_both",
        out_shape=[jax.ShapeDtypeStruct((m, width), BF16), jax.ShapeDtypeStruct((width, m), BF16)],
        grid=(m // tm,),
        in_specs=[pl.BlockSpec((tm, width), lambda i: (i, col_block)),
                  pl.BlockSpec((1, width), lambda i: (0, 0))],
        out_specs=[pl.BlockSpec((tm, width), lambda i: (i, 0)),
                   pl.BlockSpec((width, tm), lambda i: (0, i))],
        compiler_params=_params("parallel"),
    )(x, gain.reshape(1, width).astype(F32))


def _rms_t_kernel(x_ref, g_ref, *o_refs):
    x = x_ref[...]
    y = x * lax.rsqrt(jnp.mean(x * x, axis=0, keepdims=True) + NORM_EPS) * g_ref[...]
    hi = y.astype(BF16)
    o_refs[0][...] = hi
    if len(o_refs) == 2:
        o_refs[1][...] = (y - hi.astype(F32)).astype(BF16)


def _rms_t(xt, gain, *, width, row_block, tc, lo=False):
    t = xt.shape[1]
    n_out = 2 if lo else 1
    out = pl.pallas_call(
        _rms_t_kernel,
        name="rms_t",
        out_shape=[jax.ShapeDtypeStruct((width, t), BF16)] * n_out,
        grid=(t // tc,),
        in_specs=[pl.BlockSpec((width, tc), lambda i: (row_block, i)),
                  pl.BlockSpec((width, 1), lambda i: (0, 0))],
        out_specs=[pl.BlockSpec((width, tc), lambda i: (0, i))] * n_out,
        compiler_params=_params("parallel"),
    )(xt, gain.reshape(width, 1).astype(F32))
    return out if lo else out[0]


def _resid_kernel(h_ref, u_ref, gp_ref, *rest, with_next):
    hn = h_ref[...] + _rms_rows(u_ref[...].astype(F32)) * gp_ref[...]
    if with_next:
        gn_ref, hn_ref, un_ref = rest
        un_ref[...] = (_rms_rows(hn) * gn_ref[...]).astype(BF16)
    else:
        (hn_ref,) = rest
    hn_ref[...] = hn.astype(hn_ref.dtype)


def _resid_norm(h, u, g_post, g_next, *, tm, row0=0, out_dtype=F32):
    m, d = h.shape
    assert row0 % tm == 0 and m % tm == 0
    b0 = row0 // tm
    row_in = pl.BlockSpec((tm, d), lambda i: (i + b0, 0))
    row_out = pl.BlockSpec((tm, d), lambda i: (i, 0))
    vec = pl.BlockSpec((1, d), lambda i: (0, 0))
    with_next = g_next is not None
    gains = [g_post.reshape(1, d).astype(F32)]
    out_shape = [jax.ShapeDtypeStruct((m - row0, d), out_dtype)]
    if with_next:
        gains.append(g_next.reshape(1, d).astype(F32))
        out_shape.append(jax.ShapeDtypeStruct((m - row0, d), BF16))
    out = pl.pallas_call(
        functools.partial(_resid_kernel, with_next=with_next),
        name="resid_norm",
        out_shape=out_shape,
        grid=((m - row0) // tm,),
        in_specs=[row_in, row_in] + [vec] * len(gains),
        out_specs=[row_out] * len(out_shape),
        compiler_params=_params("parallel"),
    )(h, u, *gains)
    return out if with_next else out[0]


def _kidx_kernel(sm_ref, g_ref, o_ref):
    sm = sm_ref[...]
    lane = lax.broadcasted_iota(jnp.int32, sm.shape, 1)
    x = jnp.where(lane < IDX_DIM, sm, 0.0)
    ms = jnp.sum(x * x, axis=-1, keepdims=True) * (1.0 / IDX_DIM)
    y = x * lax.rsqrt(ms + NORM_EPS) * g_ref[...]
    hi = y.astype(BF16)
    lo = (y - hi.astype(F32)).astype(BF16)
    y2 = y + pltpu.roll(y, IDX_DIM, axis=1)
    o_ref[:, 0:LANES] = y2.astype(BF16)
    o_ref[:, LANES:2 * LANES] = lo


def _kidx(p1, gain, *, tm):
    m = p1.shape[0]
    g = jnp.zeros((1, LANES), F32).at[0, :IDX_DIM].set(gain.astype(F32))
    return pl.pallas_call(
        _kidx_kernel,
        name="dsa_kidx",
        out_shape=jax.ShapeDtypeStruct((m, 2 * LANES), BF16),
        grid=(m // tm,),
        in_specs=[pl.BlockSpec((tm, LANES), lambda i: (i, 0)),
                  pl.BlockSpec((1, LANES), lambda i: (0, 0))],
        out_specs=pl.BlockSpec((tm, 2 * LANES), lambda i: (i, 0)),
        compiler_params=_params("parallel"),
    )(p1, g)


def _dsa_kernel(qt_ref, qi3t_ref, wt_ref, k3_ref, k_ref, vt_ref, o_ref, sc_scr, sc16_scr, acc_scr,
                sa_scr, sb_scr, pa_scr, pb_scr, *, tq, tk, sb, top_k, pad, hps, w_scale,
                pos_bits):
    qi = pl.program_id(0)
    hd = pl.program_id(1)
    q0 = qi * tq
    nkt = (q0 + tq + tk - 1) // tk

    krel = lax.broadcasted_iota(jnp.int32, (tk, tq), 0)

    tp = k_ref.shape[0]

    @pl.when(hd == 0)
    def _select():
        sc_scr[tp:tp + tk, :] = jnp.full((tk, tq), -jnp.inf, F32)
        qpos = q0 + lax.broadcasted_iota(jnp.int32, (tk, tq), 1)

        def score_tile(kt, carry):
            ks = pl.multiple_of(kt * tk, tk)
            k3 = k3_ref[pl.ds(ks, tk), :]
            acc = jnp.zeros((tk, tq), F32)
            for ih in range(IDX_HEADS):
                d = jnp.dot(k3, qi3t_ref[ih * MXU_DIM:(ih + 1) * MXU_DIM, :],
                            preferred_element_type=F32)
                acc = acc + (wt_ref[ih:ih + 1, :] * w_scale) * jnp.maximum(d, 0.0)
            kpos = ks + krel
            ok = (kpos <= qpos) & (kpos >= pad)
            masked = jnp.where(ok, acc, -jnp.inf)
            sc_scr[pl.ds(ks, tk), :] = masked
            sc16_scr[pl.ds(ks, tk), :] = masked.astype(BF16)
            return carry

        lax.fori_loop(0, nkt, score_tile, 0)

        def count(pred):
            def body(kt, acc):
                ks = pl.multiple_of(kt * tk, tk)
                hit = pred(sc_scr[pl.ds(ks, tk), :], ks).astype(jnp.int32)
                return acc + jnp.sum(hit.reshape(tk // SUBLANES, SUBLANES, tq), axis=0)

            acc = lax.fori_loop(0, nkt, body, jnp.zeros((SUBLANES, tq), jnp.int32))
            return jnp.sum(acc, axis=0, keepdims=True)

        def key_to_f32(t):
            t = jnp.clip(t, jnp.int32(KEY_FLT_LOWEST), jnp.int32(KEY_FLT_MAX))
            return pltpu.bitcast(jnp.where(t >= 0, t, t ^ jnp.int32(0x7FFFFFFF)), F32)

        def count16(cand_b):
            rows = 2 * SUBLANES
            one, nil = jnp.ones((), BF16), jnp.zeros((), BF16)
            cand_b = jnp.maximum(cand_b, jnp.finfo(BF16).min)

            def body(kt, acc):
                ks = pl.multiple_of(kt * tk, tk)
                hit = jnp.where(sc16_scr[pl.ds(ks, tk), :] >= cand_b, one, nil)
                slabs = [hit[r:r + rows, :] for r in range(0, tk, rows)]
                while len(slabs) > 1:
                    pairs = [a + b for a, b in zip(slabs[0::2], slabs[1::2])]
                    slabs = pairs + slabs[2 * len(pairs):]
                return acc + slabs[0].astype(F32)

            acc = lax.fori_loop(0, nkt, body, jnp.zeros((rows, tq), F32))
            return jnp.sum(acc, axis=0, keepdims=True).astype(jnp.int32)

        zero = jnp.zeros((1, tq), jnp.int32)
        accept = lambda n, cand: (n >= top_k) & (cand <= jnp.int32(KEY_FLT_MAX))
        n0 = count16(jnp.zeros((1, tq), BF16))
        t_hi = jnp.where(n0 >= top_k, zero, jnp.int32(INT_MIN))

        def hi_body(i, t):
            cand = t + lax.shift_left(jnp.int32(1), 30 - i)
            n = count16(key_to_f32(cand).astype(BF16))
            return jnp.where(accept(n, cand), cand, t)

        t_hi = lax.fori_loop(0, 15, hi_body, t_hi)

        base = t_hi - jnp.int32(1 << 15)

        def lo_body(i, off):
            cand = base + (off | lax.shift_left(jnp.int32(1), 16 - i))
            cand_f = key_to_f32(cand)
            n = count(lambda s, ks: s >= cand_f)
            return jnp.where(accept(n, cand), off | lax.shift_left(jnp.int32(1), 16 - i), off)

        off = lax.fori_loop(0, 17, lo_body, zero)
        t = jnp.where(t_hi == jnp.int32(INT_MIN), t_hi, base + off)
        thr = key_to_f32(t)

        n_ge = count(lambda s, ks: s >= thr)

        @pl.when(jnp.max(n_ge) > top_k)
        def _ties():
            flushed = (t >= jnp.int32(-KEY_MIN_NORMAL)) & (t < jnp.int32(KEY_MIN_NORMAL))
            thr_next = key_to_f32(jnp.where(flushed, jnp.int32(KEY_MIN_NORMAL), t + 1))
            tie = lambda s: (s >= thr) & jnp.logical_not(s >= thr_next)
            want = top_k - count(lambda s, ks: s >= thr_next)

            def pos_body(i, cut):
                cand = cut + lax.shift_left(jnp.int32(1), pos_bits - 1 - i)
                n = count(lambda s, ks: tie(s) & (ks + krel < cand))
                return jnp.where(n < want, cand, cut)

            cut = lax.fori_loop(0, pos_bits, pos_body, zero)

            def drop_tile(kt, carry):
                ks = pl.multiple_of(kt * tk, tk)
                s = sc_scr[pl.ds(ks, tk), :]
                sc_scr[pl.ds(ks, tk), :] = jnp.where(tie(s) & (ks + krel > cut), -jnp.inf, s)
                return carry

            lax.fori_loop(0, nkt, drop_tile, 0)

        def bias_tile(kt, carry):
            ks = pl.multiple_of(kt * tk, tk)
            sc_scr[pl.ds(ks, tk), :] = jnp.where(sc_scr[pl.ds(ks, tk), :] >= thr, 0.0, -jnp.inf)
            return carry

        lax.fori_loop(0, nkt, bias_tile, 0)

    hd_sl = [slice(j * A_HEAD_DIM, (j + 1) * A_HEAD_DIM) for j in range(hps)]
    q = [qt_ref[sl, :] for sl in hd_sl]

    def logits_into(kt, s_ref):
        ks = pl.multiple_of(kt * tk, tk)
        for j in range(hps):
            s_ref[j] = jnp.dot(k_ref[pl.ds(ks, tk), hd_sl[j]], q[j], preferred_element_type=F32)

    fold = lambda x: x.reshape(sb // SUBLANES, SUBLANES, tq)

    def att_tile(kt, kt_next, bias_row, s_cur, s_next, p_ref, carry):
        logits_into(kt_next, s_next)
        ks = pl.multiple_of(kt * tk, tk)

        def masked(j, i):
            r = i * sb
            return s_cur[j, r:r + sb, :] + sc_scr[pl.ds(bias_row + r, sb), :]

        out, alphas = [], []
        for j in range(hps):
            m, l = carry[j]
            mx = jnp.full((SUBLANES, tq), -jnp.inf, F32)
            for i in range(tk // sb):
                mx = jnp.maximum(mx, jnp.max(fold(masked(j, i)), axis=0))
            m_new = jnp.maximum(m, jnp.max(mx, axis=0, keepdims=True))
            m_safe = jnp.where(m_new == -jnp.inf, 0.0, m_new)
            ls = jnp.zeros((SUBLANES, tq), F32)
            for i in range(tk // sb):
                p = jnp.exp2(masked(j, i) - m_safe)
                p_ref[j, i * sb:(i + 1) * sb, :] = p.astype(BF16)
                ls = ls + jnp.sum(fold(p), axis=0)
            alpha = jnp.exp2(m - m_safe)
            out.append((m_new, alpha * l + jnp.sum(ls, axis=0, keepdims=True)))
            alphas.append(alpha)
        pv = [jnp.dot(vt_ref[hd_sl[j], pl.ds(ks, tk)], p_ref[j], preferred_element_type=F32)
              for j in range(hps)]
        for j in range(hps):
            acc_scr[j] = alphas[j] * acc_scr[j] + pv[j]
        return tuple(out)

    def att_pair(i, carry):
        kt0 = 2 * i
        kt1 = jnp.minimum(kt0 + 1, nkt - 1)
        kt2 = jnp.minimum(kt0 + 2, nkt - 1)
        row1 = pl.multiple_of(jnp.where(kt0 + 1 < nkt, kt1 * tk, tp), tk)
        carry = att_tile(kt0, kt1, pl.multiple_of(kt0 * tk, tk), sa_scr, sb_scr, pa_scr, carry)
        return att_tile(kt1, kt2, row1, sb_scr, sa_scr, pb_scr, carry)

    acc_scr[...] = jnp.zeros_like(acc_scr)
    logits_into(0, sa_scr)
    init = (jnp.full((1, tq), -jnp.inf, F32), jnp.zeros((1, tq), F32))
    res = lax.fori_loop(0, (nkt + 1) // 2, att_pair, (init,) * hps)
    for j in range(hps):
        l = res[j][1]
        o_ref[:, hd_sl[j]] = jnp.where(l > 0.0, acc_scr[j] / l, 0.0).T.astype(o_ref.dtype)


def _dsa(qt, qi3t, wt, k3, k, vt, *, top_k, pad, tq, tk):
    tp = k.shape[0]
    assert tp % tq == 0 and tp % tk == 0 and tk >= top_k
    hps = DSA_HEADS_PER_STEP
    hw = hps * A_HEAD_DIM
    kern = functools.partial(
        _dsa_kernel, tq=tq, tk=tk, sb=DSA_SUB_BLOCK, top_k=top_k, pad=pad, hps=hps,
        w_scale=(IDX_HEADS * IDX_DIM) ** -0.5, pos_bits=tp.bit_length())
    return pl.pallas_call(
        kern,
        name="dsa",
        out_shape=jax.ShapeDtypeStruct((tp, A_HEADS * A_HEAD_DIM), BF16),
        grid=(tp // tq, A_HEADS // hps),
        in_specs=[
            pl.BlockSpec((hw, tq), lambda i, h: (h, i)),
            pl.BlockSpec((IDX_HEADS * MXU_DIM, tq), lambda i, h: (0, i)),
            pl.BlockSpec((IDX_HEADS, tq), lambda i, h: (0, i)),
            pl.BlockSpec((tp, MXU_DIM), lambda i, h: (0, 0)),
            pl.BlockSpec((tp, hw), lambda i, h: (0, h)),
            pl.BlockSpec((hw, tp), lambda i, h: (h, 0)),
        ],
        out_specs=pl.BlockSpec((tq, hw), lambda i, h: (i, h)),
        scratch_shapes=[pltpu.VMEM((tp + tk, tq), F32), pltpu.VMEM((tp, tq), BF16),
                        pltpu.VMEM((hps, A_HEAD_DIM, tq), F32)]
        + [pltpu.VMEM((hps, tk, tq), F32)] * 2 + [pltpu.VMEM((hps, tk, tq), BF16)] * 2,
        compiler_params=_params("arbitrary", "arbitrary", vmem=DSA_VMEM_LIMIT),
    )(qt, qi3t, wt, k3, k, vt)


def _dot3(a, b, dims=(((1,), (0,)), ((), ()))):
    ah, al = _hilo(a)
    bh, bl = _hilo(b)
    f = lambda x, y: lax.dot_general(x, y, dims, preferred_element_type=F32)
    return f(ah, bh) + f(ah, bl) + f(al, bh)


_NN = (((1,), (0,)), ((), ()))
_NT = (((1,), (1,)), ((), ()))
_TN = (((0,), (0,)), ((), ()))


def _mm(a, b, dims=_NN):
    return lax.dot_general(a.astype(BF16), b.astype(BF16), dims, preferred_element_type=F32)


def _gdn_kernel(qkv_ref, z_ref, sm_ref, convw_ref, alog_ref, dtb_ref, onorm_ref, o_ref,
                xbuf, s_scr, *, pad):
    c = pl.program_id(0)
    ch = GDN_CHUNK
    hist = SUBLANES

    @pl.when(c == 0)
    def _():
        xbuf[0:hist, :] = jnp.zeros((hist, xbuf.shape[1]), F32)
        s_scr[...] = jnp.zeros_like(s_scr)

    x = qkv_ref[...]
    xbuf[hist:hist + ch, :] = x
    w = convw_ref[...]
    y = x * w[CONV_K - 1:CONV_K, :]
    for j in range(CONV_K - 1):
        y = y + xbuf[hist - (CONV_K - 1) + j:hist - (CONV_K - 1) + j + ch, :] * w[j:j + 1, :]
    xbuf[0:hist, :] = x[ch - hist:ch, :]
    y = _silu(y)

    sm = sm_ref[...]
    row = c * ch + lax.broadcasted_iota(jnp.int32, sm.shape, 0)
    beta = jax.nn.sigmoid(sm)
    zsm = sm + dtb_ref[...]
    softplus = jnp.maximum(zsm, 0.0) + jnp.log(1.0 + jnp.exp(-jnp.abs(zsm)))
    g = jnp.where(row >= pad, -jnp.exp(alog_ref[...]) * softplus, 0.0)

    ri = lax.broadcasted_iota(jnp.int32, (ch, ch), 0)
    ci = lax.broadcasted_iota(jnp.int32, (ch, ch), 1)
    lower = ri >= ci
    strict = ri > ci
    tri = jnp.where(lower, 1.0, 0.0).astype(BF16)
    g1 = g.astype(BF16)
    r1 = g - g1.astype(F32)
    g2 = r1.astype(BF16)
    g3 = (r1 - g2.astype(F32)).astype(BF16)
    gcum = (jnp.dot(tri, g1, preferred_element_type=F32)
            + jnp.dot(tri, g2, preferred_element_type=F32)
            + jnp.dot(tri, g3, preferred_element_type=F32))
    gcum_t = gcum.T
    eye = jnp.where(ri == ci, 1.0, 0.0).astype(F32)

    bw = B_HEADS * B_HEAD_DIM
    hs = range(B_HEADS)
    sl = [slice(h * B_HEAD_DIM, (h + 1) * B_HEAD_DIM) for h in hs]
    part = lambda j: [y[:, j * bw + h * B_HEAD_DIM:j * bw + (h + 1) * B_HEAD_DIM] for h in hs]
    l2n = lambda x: x * lax.rsqrt(jnp.sum(x * x, axis=-1, keepdims=True) + NORM_EPS)
    q = [l2n(x) * B_HEAD_DIM ** -0.5 for x in part(0)]
    k = [l2n(x) for x in part(1)]
    v = part(2)
    b = [beta[:, SM_BETA + h:SM_BETA + h + 1] for h in hs]
    gc = [gcum[:, SM_A + h:SM_A + h + 1] for h in hs]
    gr = [gcum_t[SM_A + h:SM_A + h + 1, :] for h in hs]
    gl = [x[ch - 1:ch, :] for x in gc]
    eg = [jnp.exp(x) for x in gc]
    decay = [jnp.where(lower, jnp.exp(jnp.where(lower, gc[h] - gr[h], 0.0)), 0.0) for h in hs]
    kb = [k[h] * b[h] for h in hs]
    a_mat = [jnp.where(strict, _dot3(kb[h], k[h], _NT) * decay[h], 0.0) for h in hs]
    t_mat = [eye - a for a in a_mat]
    pw = a_mat
    for _ in range(ch.bit_length() - 2):
        pw = [_dot3(p, p) for p in pw]
        t_mat = [t_mat[h] + _dot3(t_mat[h], pw[h]) for h in hs]
    u = [_mm(t_mat[h], v[h] * b[h]) for h in hs]
    wm = [_mm(t_mat[h], kb[h] * eg[h]) for h in hs]
    qk = [jnp.where(lower, _mm(q[h], k[h], _NT) * decay[h], 0.0) for h in hs]
    state = [s_scr[h] for h in hs]
    v_new = [u[h] - _mm(wm[h], state[h]) for h in hs]
    o = [_mm(q[h] * eg[h], state[h]) + _mm(qk[h], v_new[h]) for h in hs]
    kd = [k[h] * jnp.exp(gl[h] - gc[h]) for h in hs]
    new_state = [state[h] * jnp.exp(gl[h]) + _mm(kd[h], v_new[h], _TN) for h in hs]
    for h in hs:
        s_scr[h] = new_state[h]
    for h in hs:
        o_h = _rms_rows(o[h]) * onorm_ref[...] * _silu(z_ref[:, sl[h]])
        o_ref[:, sl[h]] = o_h.astype(o_ref.dtype)


def _gdn(p2, p1, conv_w, a_log, dt_bias, o_norm, *, pad):
    tp = p2.shape[0]
    bw = B_HEADS * B_HEAD_DIM
    alog = jnp.zeros((1, LANES), F32).at[0, SM_A:SM_A + B_HEADS].set(a_log.astype(F32))
    dtb = jnp.zeros((1, LANES), F32).at[0, SM_A:SM_A + B_HEADS].set(dt_bias.astype(F32))
    ch = GDN_CHUNK
    return pl.pallas_call(
        functools.partial(_gdn_kernel, pad=pad),
        name="gdn",
        out_shape=jax.ShapeDtypeStruct((tp, bw), BF16),
        grid=(tp // ch,),
        in_specs=[
            pl.BlockSpec((ch, 3 * bw), lambda c: (c, 0)),
            pl.BlockSpec((ch, bw), lambda c: (c, 3)),
            pl.BlockSpec((ch, LANES), lambda c: (c, 0)),
            pl.BlockSpec((CONV_K, 3 * bw), lambda c: (0, 0)),
            pl.BlockSpec((1, LANES), lambda c: (0, 0)),
            pl.BlockSpec((1, LANES), lambda c: (0, 0)),
            pl.BlockSpec((1, B_HEAD_DIM), lambda c: (0, 0)),
        ],
        out_specs=pl.BlockSpec((ch, bw), lambda c: (c, 0)),
        scratch_shapes=[pltpu.VMEM((SUBLANES + ch, 3 * bw), F32),
                        pltpu.VMEM((B_HEADS, B_HEAD_DIM, B_HEAD_DIM), F32)],
        compiler_params=_params("arbitrary"),
    )(p2, p2, p1, conv_w.astype(F32), alog, dtb, o_norm.reshape(1, B_HEAD_DIM).astype(F32))


def _ret_kernel(q_ref, k_ref, v_ref, g_ref, cos_ref, sin_ref, dmask_ref, qdec_ref, kdec_ref,
                cdec_ref, o_ref, r_scr):
    @pl.when(pl.program_id(1) == 0)
    def _():
        r_scr[...] = jnp.zeros_like(r_scr)

    cos, sin = cos_ref[...], sin_ref[...]
    even = lax.broadcasted_iota(jnp.int32, cos.shape, 1) % 2 == 0

    def rot(x):
        partner = jnp.where(even, pltpu.roll(x, C_QK_DIM - 1, axis=1), pltpu.roll(x, 1, axis=1))
        return x * cos + partner * sin

    q = rot(q_ref[...].astype(F32))
    k = rot(k_ref[...].astype(F32)) * C_QK_DIM ** -0.5
    v = v_ref[...]
    s = lax.dot_general(q.astype(BF16), k.astype(BF16), _NT, preferred_element_type=F32)
    s = s * dmask_ref[...]
    r = r_scr[...]
    o = jnp.dot(s.astype(BF16), v, preferred_element_type=F32)
    o = o + jnp.dot((q * qdec_ref[...]).astype(BF16), r.astype(BF16), preferred_element_type=F32)
    kd = (k * kdec_ref[...]).astype(BF16)
    r_scr[...] = r * cdec_ref[...] + lax.dot_general(kd, v, _TN, preferred_element_type=F32)
    gate = g_ref[...].astype(F32)
    o_ref[...] = (_silu(gate) * _rms_rows(o)).astype(o_ref.dtype)


def _retention(r_all, cos, sin):
    tp = r_all.shape[0]
    ch = RET_CHUNK
    v0 = 2 * C_HEADS * C_QK_DIM // C_V_DIM
    log_gamma = jnp.log(1.0 - 2.0 ** (-5.0 - jnp.arange(C_HEADS, dtype=F32)))
    pos = jnp.arange(ch, dtype=F32)
    rel = pos[:, None] - pos[None, :]
    dmask = jnp.where(rel >= 0, jnp.exp(jnp.maximum(rel, 0.0)[None] * log_gamma[:, None, None]), 0.0)
    qdec = jnp.exp((pos + 1.0)[None, :] * log_gamma[:, None])[:, :, None]
    kdec = jnp.exp((ch - 1.0 - pos)[None, :] * log_gamma[:, None])[:, :, None]
    cdec = jnp.broadcast_to(jnp.exp(ch * log_gamma)[:, None, None], (C_HEADS, 1, C_V_DIM))
    return pl.pallas_call(
        _ret_kernel,
        name="retention",
        out_shape=jax.ShapeDtypeStruct((tp, C_HEADS * C_V_DIM), BF16),
        grid=(C_HEADS, tp // ch),
        in_specs=[
            pl.BlockSpec((ch, C_QK_DIM), lambda h, c: (c, h)),
            pl.BlockSpec((ch, C_QK_DIM), lambda h, c: (c, C_HEADS + h)),
            pl.BlockSpec((ch, C_V_DIM), lambda h, c: (c, v0 + h)),
            pl.BlockSpec((ch, C_V_DIM), lambda h, c: (c, v0 + C_HEADS + h)),
            pl.BlockSpec((ch, C_QK_DIM), lambda h, c: (c, 0)),
            pl.BlockSpec((ch, C_QK_DIM), lambda h, c: (c, 0)),
            pl.BlockSpec((None, ch, ch), lambda h, c: (h, 0, 0)),
            pl.BlockSpec((None, ch, 1), lambda h, c: (h, 0, 0)),
            pl.BlockSpec((None, ch, 1), lambda h, c: (h, 0, 0)),
            pl.BlockSpec((None, 1, C_V_DIM), lambda h, c: (h, 0, 0)),
        ],
        out_specs=pl.BlockSpec((ch, C_V_DIM), lambda h, c: (c, h)),
        scratch_shapes=[pltpu.VMEM((C_QK_DIM, C_V_DIM), F32)],
        compiler_params=_params("arbitrary", "arbitrary"),
    )(r_all, r_all, r_all, r_all, cos, sin, dmask, qdec, kdec, cdec)


def _hybrid_mixer(u_hi, u_lo, w_in, q_norm, w_uq, w_qidx, kv_norm, w_ukv, kidx_norm,
                  conv_w, a_log, dt_bias, o_norm, w_out, *, top_k, pad, tm, tm_wide):
    tp = u_hi.shape[0]
    o = [0]
    for s in (A_Q_RANK, A_KV_RANK, IDX_DIM, IDX_HEADS, 3 * B_HEADS * B_HEAD_DIM,
              B_HEADS * B_HEAD_DIM, B_HEADS, B_HEADS):
        o.append(o[-1] + s)
    col = lambda i: w_in[:, o[i]:o[i + 1]]
    n_small = IDX_DIM + IDX_HEADS + 2 * B_HEADS
    w1 = jnp.concatenate([col(0), col(2), col(3), col(6), col(7),
                          jnp.zeros((w_in.shape[0], LANES - n_small), w_in.dtype)], axis=1)
    w2 = jnp.concatenate([col(4), col(5), col(1)], axis=1).astype(BF16)
    p1t, small = _matmul("hy_in_x3", (u_hi, u_lo), _hilo(w1), F32, tm=tm, tn=w1.shape[1],
                         epilogue="transposed")
    p2 = _matmul("hy_in", (u_hi,), (w2,), F32, tm=tm_wide, tn=MXU_DIM)

    cqt_hi, cqt_lo = _rms_t(p1t, q_norm, width=A_Q_RANK, row_block=0, tc=tm, lo=True)
    up_rows = w_uq.shape[1]
    qi3t = _matmul("dsa_qidx", _hilo(w_qidx.T), (cqt_hi, cqt_lo), BF16, tm=up_rows, tn=tm,
                   epilogue="split3")
    q_scale = A_HEAD_DIM ** -0.5 * math.log2(math.e)
    qt = _matmul("dsa_q", ((w_uq.T * q_scale).astype(BF16),), (cqt_hi,), BF16, tm=up_rows, tn=tm)
    k3 = _kidx(small, kidx_norm, tm=tm)
    wt = p1t[A_Q_RANK + SM_WIDX:A_Q_RANK + SM_BETA]
    ckv_block = (p2.shape[1] - A_KV_RANK) // A_KV_RANK
    kvn, kvnt = _rms_both(p2, kv_norm, width=A_KV_RANK, col_block=ckv_block, tm=tm)
    w_kv = w_ukv.reshape(A_KV_RANK, A_HEADS, 2, A_HEAD_DIM)
    w_k = w_kv[:, :, 0].reshape(A_KV_RANK, A_HEADS * A_HEAD_DIM).astype(BF16)
    w_vt = w_kv[:, :, 1].reshape(A_KV_RANK, A_HEADS * A_HEAD_DIM).T.astype(BF16)
    k = _matmul("dsa_k", (kvn,), (w_k,), BF16, tm=tm, tn=up_rows)
    vt = _matmul("dsa_vt", (w_vt,), (kvnt,), BF16, tm=up_rows, tn=tm)
    o_a = _dsa(qt, qi3t, wt, k3, k, vt, top_k=top_k, pad=pad, tq=MXU_DIM, tk=tm)

    o_b = _gdn(p2, small, conv_w, a_log, dt_bias, o_norm, pad=pad)

    return _matmul_w32("hy_out", [o_a, o_b], w_out[None], 0, BF16, tm=tm_wide, tn=2 * MXU_DIM)


def _retention_mixer(u, w_in, w_out, *, pad, tm, tm_wide):
    tp = u.shape[0]
    half = C_QK_DIM // 2
    r_all = _matmul_w32("ret_in", [u], w_in[None], 0, BF16, tm=tm_wide, tn=2 * MXU_DIM)
    inv = 1.0 / (ROT_BASE ** jnp.linspace(0.0, 1.0, half, dtype=F32))
    posn = (jnp.arange(tp) - pad).astype(F32)
    ang = posn[:, None] * jnp.repeat(inv, 2)[None, :]
    sign = jnp.tile(jnp.array([-1.0, 1.0], F32), half)
    og = _retention(r_all, jnp.cos(ang), jnp.sin(ang) * sign[None, :])
    return _matmul_w32("ret_out", [og], w_out[None], 0, BF16, tm=tm_wide, tn=2 * MXU_DIM,
                       vmem=BIG_TILE_VMEM_LIMIT)


def _ffn(u, w_gate, w_up, w_down, layer, *, tm, tm_wide):
    act = _ffn_act(u, w_gate, w_up, layer, tm=tm_wide, tn=2 * MXU_DIM)
    return _matmul_w32("ffn_down", [act], w_down, layer, BF16, tm=tm, tn=2 * MXU_DIM,
                       vmem=BIG_TILE_VMEM_LIMIT)


def kernel(x, meta_tokens, mix_norm_pre, mix_norm_post, ffn_norm_pre, ffn_norm_post, hy_w_in, dsa_q_norm, dsa_w_uq, dsa_w_qidx, dsa_kv_norm, dsa_w_ukv, dsa_kidx_norm, gdn_conv_w, gdn_a_log, gdn_dt_bias, gdn_o_norm, hy_w_out, ret_w_in, ret_w_out, ffn_w_gate, ffn_w_up, ffn_w_down):
    b, seq, d = x.shape
    assert b == 1 and mix_norm_pre.shape[0] == 2
    pad = (-N_META) % ROW_ALIGN
    tp = pad + N_META + seq
    tm = ROW_TILE
    assert tp % tm == 0
    tm_wide = WIDE_ROW_TILE if tp % WIDE_ROW_TILE == 0 else tm
    top_k = min(TOPK_MAX, seq // TOPK_FRAC)
    h, u_hi, u_lo = _embed_norm(x[0], meta_tokens, mix_norm_pre[0], pad=pad)
    mix = _hybrid_mixer(u_hi, u_lo, hy_w_in[0], dsa_q_norm[0], dsa_w_uq[0], dsa_w_qidx[0],
                        dsa_kv_norm[0], dsa_w_ukv[0], dsa_kidx_norm[0], gdn_conv_w[0],
                        gdn_a_log[0], gdn_dt_bias[0], gdn_o_norm[0], hy_w_out[0],
                        top_k=top_k, pad=pad, tm=tm, tm_wide=tm_wide)
    h, u = _resid_norm(h, mix, mix_norm_post[0], ffn_norm_pre[0], tm=tm)
    f = _ffn(u, ffn_w_gate, ffn_w_up, ffn_w_down, 0, tm=tm, tm_wide=tm_wide)
    h, u = _resid_norm(h, f, ffn_norm_post[0], mix_norm_pre[1], tm=tm)
    mix = _retention_mixer(u, ret_w_in[0], ret_w_out[0], pad=pad, tm=tm, tm_wide=tm_wide)
    h, u = _resid_norm(h, mix, mix_norm_post[1], ffn_norm_pre[1], tm=tm)
    f = _ffn(u, ffn_w_gate, ffn_w_up, ffn_w_down, 1, tm=tm, tm_wide=tm_wide)
    out = _resid_norm(h, f, ffn_norm_post[1], None, tm=ROW_ALIGN, row0=pad + N_META,
                      out_dtype=x.dtype)
    return out[None]
```

```python
import functools
import math

import jax
import jax.numpy as jnp
from jax import lax
from jax.experimental import pallas as pl
from jax.experimental.pallas import tpu as pltpu

F32 = jnp.float32
BF16 = jnp.bfloat16

N_META = 16
NORM_EPS = 1e-6

A_HEADS = 8
A_HEAD_DIM = 128
A_Q_RANK = 512
A_KV_RANK = 256
IDX_HEADS = 16
IDX_DIM = 64
TOPK_MAX = 256
TOPK_FRAC = 4

B_HEADS = 8
B_HEAD_DIM = 128
CONV_K = 4
GDN_CHUNK = 128
DSA_HEADS_PER_STEP = 2
DSA_VMEM_LIMIT = 56 * 1024 * 1024
BIG_TILE_VMEM_LIMIT = 56 * 1024 * 1024
DSA_SUB_BLOCK = 64

C_HEADS = 8
C_QK_DIM = 256
C_V_DIM = 512
ROT_BASE = 10000.0
RET_CHUNK = 384

LANES = 128
SUBLANES = 8
MXU_DIM = 256
ROW_ALIGN = MXU_DIM
ROW_TILE = 3 * MXU_DIM
WIDE_ROW_TILE = 11 * LANES
VMEM_LIMIT = 48 * 1024 * 1024

SM_WIDX = IDX_DIM
SM_BETA = SM_WIDX + IDX_HEADS
SM_A = SM_BETA + B_HEADS

INT_MIN = -2 ** 31
KEY_FLT_MAX = 0x7F7FFFFF
KEY_MIN_NORMAL = 0x00800000
KEY_FLT_LOWEST = (0xFF7FFFFF ^ 0x7FFFFFFF) - 2 ** 32


def _params(*sem, vmem=VMEM_LIMIT):
    return pltpu.CompilerParams(dimension_semantics=sem, vmem_limit_bytes=vmem)


def _hilo(x):
    hi = x.astype(BF16)
    return hi, (x - hi.astype(F32)).astype(BF16)


def _silu(x):
    return x * jax.nn.sigmoid(x)


def _mm_kernel(*refs, n_parts, epilogue):
    a, b, o = refs[:n_parts], refs[n_parts:2 * n_parts], refs[2 * n_parts:]
    a_hi, b_hi = a[0][...], b[0][...]
    acc = jnp.dot(a_hi, b_hi, preferred_element_type=F32)
    if n_parts == 2:
        acc += jnp.dot(a_hi, b[1][...], preferred_element_type=F32)
        acc += jnp.dot(a[1][...], b_hi, preferred_element_type=F32)
    if epilogue is None:
        o[0][...] = acc.astype(o[0].dtype)
    elif epilogue == "transposed":
        o[0][...] = acc.T
        o[1][...] = acc[:, acc.shape[1] - LANES:]
    elif epilogue == "split3":
        for g in range(acc.shape[0] // IDX_DIM):
            hi, lo = _hilo(acc[g * IDX_DIM:(g + 1) * IDX_DIM, :])
            base = g * MXU_DIM
            o[0][base:base + IDX_DIM, :] = hi
            o[0][base + IDX_DIM:base + 2 * IDX_DIM, :] = lo
            o[0][base + 2 * IDX_DIM:base + 3 * IDX_DIM, :] = hi
            o[0][base + 3 * IDX_DIM:base + MXU_DIM, :] = jnp.zeros_like(hi)


def _matmul(name, a_parts, b_parts, out_dtype, *, tm, tn, epilogue=None):
    m, kdim = a_parts[0].shape
    n = b_parts[0].shape[1]
    assert m % tm == 0 and n % tn == 0, (m, n, tm, tn)
    assert len(a_parts) == len(b_parts)
    if epilogue is None:
        out_shape = [jax.ShapeDtypeStruct((m, n), out_dtype)]
        out_specs = [pl.BlockSpec((tm, tn), lambda i, j: (i, j))]
    elif epilogue == "transposed":
        assert tn == n
        out_shape = [jax.ShapeDtypeStruct((n, m), out_dtype),
                     jax.ShapeDtypeStruct((m, LANES), out_dtype)]
        out_specs = [pl.BlockSpec((n, tm), lambda i, j: (0, i)),
                     pl.BlockSpec((tm, LANES), lambda i, j: (i, 0))]
    else:
        scale = MXU_DIM // IDX_DIM
        out_shape = [jax.ShapeDtypeStruct((m * scale, n), out_dtype)]
        out_specs = [pl.BlockSpec((tm * scale, tn), lambda i, j: (i, j))]
    out = pl.pallas_call(
        functools.partial(_mm_kernel, n_parts=len(a_parts), epilogue=epilogue),
        name=name,
        out_shape=out_shape,
        grid=(m // tm, n // tn),
        in_specs=[pl.BlockSpec((tm, kdim), lambda i, j: (i, 0))] * len(a_parts)
        + [pl.BlockSpec((kdim, tn), lambda i, j: (0, j))] * len(b_parts),
        out_specs=out_specs,
        compiler_params=_params("parallel", "parallel"),
    )(*a_parts, *b_parts)
    return out[0] if len(out) == 1 else out


def _mm_w32_kernel(*refs, n_a):
    a, w_ref, o_ref = refs[:n_a], refs[n_a], refs[n_a + 1]
    w = w_ref[...].astype(BF16)
    acc, k0 = None, 0
    for a_ref in a:
        k1 = k0 + a_ref.shape[1]
        part = jnp.dot(a_ref[...], w[k0:k1, :], preferred_element_type=F32)
        acc = part if acc is None else acc + part
        k0 = k1
    o_ref[...] = acc.astype(o_ref.dtype)


def _matmul_w32(name, a_list, w, layer, out_dtype, *, tm, tn, col0=0, n=None, vmem=VMEM_LIMIT):
    m = a_list[0].shape[0]
    kdim = sum(a.shape[1] for a in a_list)
    n = w.shape[2] - col0 if n is None else n
    assert w.shape[1] == kdim and m % tm == 0 and n % tn == 0 and col0 % tn == 0
    jb = col0 // tn
    return pl.pallas_call(
        functools.partial(_mm_w32_kernel, n_a=len(a_list)),
        name=name,
        out_shape=jax.ShapeDtypeStruct((m, n), out_dtype),
        grid=(m // tm, n // tn),
        in_specs=[pl.BlockSpec((tm, a.shape[1]), lambda i, j: (i, 0)) for a in a_list]
        + [pl.BlockSpec((None, kdim, tn), lambda i, j: (layer, 0, j + jb))],
        out_specs=pl.BlockSpec((tm, tn), lambda i, j: (i, j)),
        compiler_params=_params("parallel", "parallel", vmem=vmem),
    )(*a_list, w)


def _ffn_act_kernel(x_ref, wg_ref, wu_ref, o_ref):
    x = x_ref[...]
    g = jnp.dot(x, wg_ref[...].astype(BF16), preferred_element_type=F32)
    u = jnp.dot(x, wu_ref[...].astype(BF16), preferred_element_type=F32)
    o_ref[...] = (_silu(g) * u).astype(o_ref.dtype)


def _ffn_act(x, wg, wu, layer, *, tm, tn):
    m, kdim = x.shape
    n = wg.shape[2]
    assert m % tm == 0 and n % tn == 0
    w_spec = pl.BlockSpec((None, kdim, tn), lambda i, j: (layer, 0, j))
    return pl.pallas_call(
        _ffn_act_kernel,
        name="ffn_act",
        out_shape=jax.ShapeDtypeStruct((m, n), BF16),
        grid=(m // tm, n // tn),
        in_specs=[pl.BlockSpec((tm, kdim), lambda i, j: (i, 0)), w_spec, w_spec],
        out_specs=pl.BlockSpec((tm, tn), lambda i, j: (i, j)),
        compiler_params=_params("parallel", "parallel"),
    )(x, wg, wu)


def _rms_rows(x):
    return x * lax.rsqrt(jnp.mean(x * x, axis=-1, keepdims=True) + NORM_EPS)


def _embed_kernel(x_ref, head_ref, g_ref, h_ref, hi_ref, lo_ref):
    h = jnp.where(pl.program_id(0) == 0, head_ref[...], x_ref[...].astype(F32))
    h_ref[...] = h
    y = _rms_rows(h) * g_ref[...]
    hi = y.astype(BF16)
    hi_ref[...] = hi
    lo_ref[...] = (y - hi.astype(F32)).astype(BF16)


def _embed_norm(x, meta, gain, *, pad):
    seq, d = x.shape
    tb = pad + meta.shape[0]
    assert seq % tb == 0
    head = jnp.concatenate([jnp.zeros((pad, d), F32), meta.astype(F32)], axis=0)
    tp = tb + seq
    row = pl.BlockSpec((tb, d), lambda i: (i, 0))
    return pl.pallas_call(
        _embed_kernel,
        name="embed_norm",
        out_shape=[jax.ShapeDtypeStruct((tp, d), F32)] + [jax.ShapeDtypeStruct((tp, d), BF16)] * 2,
        grid=(tp // tb,),
        in_specs=[pl.BlockSpec((tb, d), lambda i: (jnp.maximum(i - 1, 0), 0)),
                  pl.BlockSpec((tb, d), lambda i: (0, 0)),
                  pl.BlockSpec((1, d), lambda i: (0, 0))],
        out_specs=[row, row, row],
        compiler_params=_params("parallel"),
    )(x, head, gain.reshape(1, d).astype(F32))


def _rms_both_kernel(x_ref, g_ref, o_ref, ot_ref):
    y = _rms_rows(x_ref[...]) * g_ref[...]
    o_ref[...] = y.astype(BF16)
    ot_ref[...] = y.T.astype(BF16)


def _rms_both(x, gain, *, width, col_block, tm):
    m = x.shape[0]
    return pl.pallas_call(
        _rms_both_kernel,
        name="rms_both",
        out_shape=[jax.ShapeDtypeStruct((m, width), BF16), jax.ShapeDtypeStruct((width, m), BF16)],
        grid=(m // tm,),
        in_specs=[pl.BlockSpec((tm, width), lambda i: (i, col_block)),
                  pl.BlockSpec((1, width), lambda i: (0, 0))],
        out_specs=[pl.BlockSpec((tm, width), lambda i: (i, 0)),
                   pl.BlockSpec((width, tm), lambda i: (0, i))],
        compiler_params=_params("parallel"),
    )(x, gain.reshape(1, width).astype(F32))


def _rms_t_kernel(x_ref, g_ref, *o_refs):
    x = x_ref[...]
    y = x * lax.rsqrt(jnp.mean(x * x, axis=0, keepdims=True) + NORM_EPS) * g_ref[...]
    hi = y.astype(BF16)
    o_refs[0][...] = hi
    if len(o_refs) == 2:
        o_refs[1][...] = (y - hi.astype(F32)).astype(BF16)


def _rms_t(xt, gain, *, width, row_block, tc, lo=False):
    t = xt.shape[1]
    n_out = 2 if lo else 1
    out = pl.pallas_call(
        _rms_t_kernel,
        name="rms_t",
        out_shape=[jax.ShapeDtypeStruct((width, t), BF16)] * n_out,
        grid=(t // tc,),
        in_specs=[pl.BlockSpec((width, tc), lambda i: (row_block, i)),
                  pl.BlockSpec((width, 1), lambda i: (0, 0))],
        out_specs=[pl.BlockSpec((width, tc), lambda i: (0, i))] * n_out,
        compiler_params=_params("parallel"),
    )(xt, gain.reshape(width, 1).astype(F32))
    return out if lo else out[0]


def _resid_kernel(h_ref, u_ref, gp_ref, *rest, with_next):
    hn = h_ref[...] + _rms_rows(u_ref[...].astype(F32)) * gp_ref[...]
    if with_next:
        gn_ref, hn_ref, un_ref = rest
        un_ref[...] = (_rms_rows(hn) * gn_ref[...]).astype(BF16)
    else:
        (hn_ref,) = rest
    hn_ref[...] = hn.astype(hn_ref.dtype)


def _resid_norm(h, u, g_post, g_next, *, tm, row0=0, out_dtype=F32):
    m, d = h.shape
    assert row0 % tm == 0 and m % tm == 0
    b0 = row0 // tm
    row_in = pl.BlockSpec((tm, d), lambda i: (i + b0, 0))
    row_out = pl.BlockSpec((tm, d), lambda i: (i, 0))
    vec = pl.BlockSpec((1, d), lambda i: (0, 0))
    with_next = g_next is not None
    gains = [g_post.reshape(1, d).astype(F32)]
    out_shape = [jax.ShapeDtypeStruct((m - row0, d), out_dtype)]
    if with_next:
        gains.append(g_next.reshape(1, d).astype(F32))
        out_shape.append(jax.ShapeDtypeStruct((m - row0, d), BF16))
    out = pl.pallas_call(
        functools.partial(_resid_kernel, with_next=with_next),
        name="resid_norm",
        out_shape=out_shape,
        grid=((m - row0) // tm,),
        in_specs=[row_in, row_in] + [vec] * len(gains),
        out_specs=[row_out] * len(out_shape),
        compiler_params=_params("parallel"),
    )(h, u, *gains)
    return out if with_next else out[0]


def _kidx_kernel(sm_ref, g_ref, o_ref):
    sm = sm_ref[...]
    lane = lax.broadcasted_iota(jnp.int32, sm.shape, 1)
    x = jnp.where(lane < IDX_DIM, sm, 0.0)
    ms = jnp.sum(x * x, axis=-1, keepdims=True) * (1.0 / IDX_DIM)
    y = x * lax.rsqrt(ms + NORM_EPS) * g_ref[...]
    hi = y.astype(BF16)
    lo = (y - hi.astype(F32)).astype(BF16)
    y2 = y + pltpu.roll(y, IDX_DIM, axis=1)
    o_ref[:, 0:LANES] = y2.astype(BF16)
    o_ref[:, LANES:2 * LANES] = lo


def _kidx(p1, gain, *, tm):
    m = p1.shape[0]
    g = jnp.zeros((1, LANES), F32).at[0, :IDX_DIM].set(gain.astype(F32))
    return pl.pallas_call(
        _kidx_kernel,
        name="dsa_kidx",
        out_shape=jax.ShapeDtypeStruct((m, 2 * LANES), BF16),
        grid=(m // tm,),
        in_specs=[pl.BlockSpec((tm, LANES), lambda i: (i, 0)),
                  pl.BlockSpec((1, LANES), lambda i: (0, 0))],
        out_specs=pl.BlockSpec((tm, 2 * LANES), lambda i: (i, 0)),
        compiler_params=_params("parallel"),
    )(p1, g)


def _dsa_kernel(qt_ref, qi3t_ref, wt_ref, k3_ref, k_ref, vt_ref, o_ref, sc_scr, sc16_scr, acc_scr,
                sa_scr, sb_scr, pa_scr, pb_scr, *, tq, tk, sb, top_k, pad, hps, w_scale,
                pos_bits):
    qi = pl.program_id(0)
    hd = pl.program_id(1)
    q0 = qi * tq
    nkt = (q0 + tq + tk - 1) // tk

    krel = lax.broadcasted_iota(jnp.int32, (tk, tq), 0)

    tp = k_ref.shape[0]

    @pl.when(hd == 0)
    def _select():
        sc_scr[tp:tp + tk, :] = jnp.full((tk, tq), -jnp.inf, F32)
        qpos = q0 + lax.broadcasted_iota(jnp.int32, (tk, tq), 1)

        def score_tile(kt, carry):
            ks = pl.multiple_of(kt * tk, tk)
            k3 = k3_ref[pl.ds(ks, tk), :]
            acc = jnp.zeros((tk, tq), F32)
            for ih in range(IDX_HEADS):
                d = jnp.dot(k3, qi3t_ref[ih * MXU_DIM:(ih + 1) * MXU_DIM, :],
                            preferred_element_type=F32)
                acc = acc + (wt_ref[ih:ih + 1, :] * w_scale) * jnp.maximum(d, 0.0)
            kpos = ks + krel
            ok = (kpos <= qpos) & (kpos >= pad)
            masked = jnp.where(ok, acc, -jnp.inf)
            sc_scr[pl.ds(ks, tk), :] = masked
            sc16_scr[pl.ds(ks, tk), :] = masked.astype(BF16)
            return carry

        lax.fori_loop(0, nkt, score_tile, 0)

        def count(pred):
            def body(kt, acc):
                ks = pl.multiple_of(kt * tk, tk)
                hit = pred(sc_scr[pl.ds(ks, tk), :], ks).astype(jnp.int32)
                return acc + jnp.sum(hit.reshape(tk // SUBLANES, SUBLANES, tq), axis=0)

            acc = lax.fori_loop(0, nkt, body, jnp.zeros((SUBLANES, tq), jnp.int32))
            return jnp.sum(acc, axis=0, keepdims=True)

        def key_to_f32(t):
            t = jnp.clip(t, jnp.int32(KEY_FLT_LOWEST), jnp.int32(KEY_FLT_MAX))
            return pltpu.bitcast(jnp.where(t >= 0, t, t ^ jnp.int32(0x7FFFFFFF)), F32)

        def count16(cand_b):
            rows = 2 * SUBLANES
            one, nil = jnp.ones((), BF16), jnp.zeros((), BF16)
            cand_b = jnp.maximum(cand_b, jnp.finfo(BF16).min)

            def body(kt, acc):
                ks = pl.multiple_of(kt * tk, tk)
                hit = jnp.where(sc16_scr[pl.ds(ks, tk), :] >= cand_b, one, nil)
                slabs = [hit[r:r + rows, :] for r in range(0, tk, rows)]
                while len(slabs) > 1:
                    pairs = [a + b for a, b in zip(slabs[0::2], slabs[1::2])]
                    slabs = pairs + slabs[2 * len(pairs):]
                return acc + slabs[0].astype(F32)

            acc = lax.fori_loop(0, nkt, body, jnp.zeros((rows, tq), F32))
            return jnp.sum(acc, axis=0, keepdims=True).astype(jnp.int32)

        zero = jnp.zeros((1, tq), jnp.int32)
        accept = lambda n, cand: (n >= top_k) & (cand <= jnp.int32(KEY_FLT_MAX))
        n0 = count16(jnp.zeros((1, tq), BF16))
        t_hi = jnp.where(n0 >= top_k, zero, jnp.int32(INT_MIN))

        def hi_body(i, t):
            cand = t + lax.shift_left(jnp.int32(1), 30 - i)
            n = count16(key_to_f32(cand).astype(BF16))
            return jnp.where(accept(n, cand), cand, t)

        t_hi = lax.fori_loop(0, 15, hi_body, t_hi)

        base = t_hi - jnp.int32(1 << 15)

        def lo_body(i, off):
            cand = base + (off | lax.shift_left(jnp.int32(1), 16 - i))
            cand_f = key_to_f32(cand)
            n = count(lambda s, ks: s >= cand_f)
            return jnp.where(accept(n, cand), off | lax.shift_left(jnp.int32(1), 16 - i), off)

        off = lax.fori_loop(0, 17, lo_body, zero)
        t = jnp.where(t_hi == jnp.int32(INT_MIN), t_hi, base + off)
        thr = key_to_f32(t)

        n_ge = count(lambda s, ks: s >= thr)

        @pl.when(jnp.max(n_ge) > top_k)
        def _ties():
            flushed = (t >= jnp.int32(-KEY_MIN_NORMAL)) & (t < jnp.int32(KEY_MIN_NORMAL))
            thr_next = key_to_f32(jnp.where(flushed, jnp.int32(KEY_MIN_NORMAL), t + 1))
            tie = lambda s: (s >= thr) & jnp.logical_not(s >= thr_next)
            want = top_k - count(lambda s, ks: s >= thr_next)

            def pos_body(i, cut):
                cand = cut + lax.shift_left(jnp.int32(1), pos_bits - 1 - i)
                n = count(lambda s, ks: tie(s) & (ks + krel < cand))
                return jnp.where(n < want, cand, cut)

            cut = lax.fori_loop(0, pos_bits, pos_body, zero)

            def drop_tile(kt, carry):
                ks = pl.multiple_of(kt * tk, tk)
                s = sc_scr[pl.ds(ks, tk), :]
                sc_scr[pl.ds(ks, tk), :] = jnp.where(tie(s) & (ks + krel > cut), -jnp.inf, s)
                return carry

            lax.fori_loop(0, nkt, drop_tile, 0)

        def bias_tile(kt, carry):
            ks = pl.multiple_of(kt * tk, tk)
            sc_scr[pl.ds(ks, tk), :] = jnp.where(sc_scr[pl.ds(ks, tk), :] >= thr, 0.0, -jnp.inf)
            return carry

        lax.fori_loop(0, nkt, bias_tile, 0)

    hd_sl = [slice(j * A_HEAD_DIM, (j + 1) * A_HEAD_DIM) for j in range(hps)]
    q = [qt_ref[sl, :] for sl in hd_sl]

    def logits_into(kt, s_ref):
        ks = pl.multiple_of(kt * tk, tk)
        for j in range(hps):
            s_ref[j] = jnp.dot(k_ref[pl.ds(ks, tk), hd_sl[j]], q[j], preferred_element_type=F32)

    fold = lambda x: x.reshape(sb // SUBLANES, SUBLANES, tq)

    def att_tile(kt, kt_next, bias_row, s_cur, s_next, p_ref, carry):
        logits_into(kt_next, s_next)
        ks = pl.multiple_of(kt * tk, tk)

        def masked(j, i):
            r = i * sb
            return s_cur[j, r:r + sb, :] + sc_scr[pl.ds(bias_row + r, sb), :]

        out, alphas = [], []
        for j in range(hps):
            m, l = carry[j]
            mx = jnp.full((SUBLANES, tq), -jnp.inf, F32)
            for i in range(tk // sb):
                mx = jnp.maximum(mx, jnp.max(fold(masked(j, i)), axis=0))
            m_new = jnp.maximum(m, jnp.max(mx, axis=0, keepdims=True))
            m_safe = jnp.where(m_new == -jnp.inf, 0.0, m_new)
            for i in range(tk // sb):
                p = jnp.exp2(masked(j, i) - m_safe)
                p_ref[j, i * sb:(i + 1) * sb, :] = p.astype(BF16)
            alphas.append(jnp.exp2(m - m_safe))
            out.append(m_new)
        ones = jnp.ones((2 * SUBLANES, tk), BF16)
        pv = [jnp.dot(jnp.concatenate([vt_ref[hd_sl[j], pl.ds(ks, tk)], ones], axis=0), p_ref[j],
                      preferred_element_type=F32) for j in range(hps)]
        for j in range(hps):
            acc_scr[j] = alphas[j] * acc_scr[j] + pv[j][:A_HEAD_DIM, :]
        return tuple((out[j], alphas[j] * carry[j][1] + pv[j][A_HEAD_DIM:A_HEAD_DIM + 1, :])
                     for j in range(hps))

    def att_pair(i, carry):
        kt0 = 2 * i
        kt1 = jnp.minimum(kt0 + 1, nkt - 1)
        kt2 = jnp.minimum(kt0 + 2, nkt - 1)
        row1 = pl.multiple_of(jnp.where(kt0 + 1 < nkt, kt1 * tk, tp), tk)
        carry = att_tile(kt0, kt1, pl.multiple_of(kt0 * tk, tk), sa_scr, sb_scr, pa_scr, carry)
        return att_tile(kt1, kt2, row1, sb_scr, sa_scr, pb_scr, carry)

    acc_scr[...] = jnp.zeros_like(acc_scr)
    logits_into(0, sa_scr)
    init = (jnp.full((1, tq), -jnp.inf, F32), jnp.zeros((1, tq), F32))
    res = lax.fori_loop(0, (nkt + 1) // 2, att_pair, (init,) * hps)
    for j in range(hps):
        l = res[j][1]
        o_ref[:, hd_sl[j]] = jnp.where(l > 0.0, acc_scr[j] / l, 0.0).T.astype(o_ref.dtype)


def _dsa(qt, qi3t, wt, k3, k, vt, *, top_k, pad, tq, tk):
    tp = k.shape[0]
    assert tp % tq == 0 and tp % tk == 0 and tk >= top_k
    hps = DSA_HEADS_PER_STEP
    hw = hps * A_HEAD_DIM
    kern = functools.partial(
        _dsa_kernel, tq=tq, tk=tk, sb=DSA_SUB_BLOCK, top_k=top_k, pad=pad, hps=hps,
        w_scale=(IDX_HEADS * IDX_DIM) ** -0.5, pos_bits=tp.bit_length())
    return pl.pallas_call(
        kern,
        name="dsa",
        out_shape=jax.ShapeDtypeStruct((tp, A_HEADS * A_HEAD_DIM), BF16),
        grid=(tp // tq, A_HEADS // hps),
        in_specs=[
            pl.BlockSpec((hw, tq), lambda i, h: (h, i)),
            pl.BlockSpec((IDX_HEADS * MXU_DIM, tq), lambda i, h: (0, i)),
            pl.BlockSpec((IDX_HEADS, tq), lambda i, h: (0, i)),
            pl.BlockSpec((tp, MXU_DIM), lambda i, h: (0, 0)),
            pl.BlockSpec((tp, hw), lambda i, h: (0, h)),
            pl.BlockSpec((hw, tp), lambda i, h: (h, 0)),
        ],
        out_specs=pl.BlockSpec((tq, hw), lambda i, h: (i, h)),
        scratch_shapes=[pltpu.VMEM((tp + tk, tq), F32), pltpu.VMEM((tp, tq), BF16),
                        pltpu.VMEM((hps, A_HEAD_DIM, tq), F32)]
        + [pltpu.VMEM((hps, tk, tq), F32)] * 2 + [pltpu.VMEM((hps, tk, tq), BF16)] * 2,
        compiler_params=_params("arbitrary", "arbitrary", vmem=DSA_VMEM_LIMIT),
    )(qt, qi3t, wt, k3, k, vt)


def _dot3(a, b, dims=(((1,), (0,)), ((), ()))):
    ah, al = _hilo(a)
    bh, bl = _hilo(b)
    f = lambda x, y: lax.dot_general(x, y, dims, preferred_element_type=F32)
    return f(ah, bh) + f(ah, bl) + f(al, bh)


_NN = (((1,), (0,)), ((), ()))
_NT = (((1,), (1,)), ((), ()))
_TN = (((0,), (0,)), ((), ()))


def _mm(a, b, dims=_NN):
    return lax.dot_general(a.astype(BF16), b.astype(BF16), dims, preferred_element_type=F32)


def _gdn_kernel(qkv_ref, z_ref, sm_ref, convw_ref, alog_ref, dtb_ref, onorm_ref, o_ref,
                xbuf, s_scr, *, pad):
    c = pl.program_id(0)
    ch = GDN_CHUNK
    hist = SUBLANES

    @pl.when(c == 0)
    def _():
        xbuf[0:hist, :] = jnp.zeros((hist, xbuf.shape[1]), F32)
        s_scr[...] = jnp.zeros_like(s_scr)

    x = qkv_ref[...]
    xbuf[hist:hist + ch, :] = x
    w = convw_ref[...]
    y = x * w[CONV_K - 1:CONV_K, :]
    for j in range(CONV_K - 1):
        y = y + xbuf[hist - (CONV_K - 1) + j:hist - (CONV_K - 1) + j + ch, :] * w[j:j + 1, :]
    xbuf[0:hist, :] = x[ch - hist:ch, :]
    y = _silu(y)

    sm = sm_ref[...]
    row = c * ch + lax.broadcasted_iota(jnp.int32, sm.shape, 0)
    beta = jax.nn.sigmoid(sm)
    zsm = sm + dtb_ref[...]
    softplus = jnp.maximum(zsm, 0.0) + jnp.log(1.0 + jnp.exp(-jnp.abs(zsm)))
    g = jnp.where(row >= pad, -jnp.exp(alog_ref[...]) * softplus, 0.0)

    ri = lax.broadcasted_iota(jnp.int32, (ch, ch), 0)
    ci = lax.broadcasted_iota(jnp.int32, (ch, ch), 1)
    lower = ri >= ci
    strict = ri > ci
    tri = jnp.where(lower, 1.0, 0.0).astype(BF16)
    g1 = g.astype(BF16)
    r1 = g - g1.astype(F32)
    g2 = r1.astype(BF16)
    g3 = (r1 - g2.astype(F32)).astype(BF16)
    gcum = (jnp.dot(tri, g1, preferred_element_type=F32)
            + jnp.dot(tri, g2, preferred_element_type=F32)
            + jnp.dot(tri, g3, preferred_element_type=F32))
    gcum_t = gcum.T
    eye = jnp.where(ri == ci, 1.0, 0.0).astype(F32)

    bw = B_HEADS * B_HEAD_DIM
    hs = range(B_HEADS)
    sl = [slice(h * B_HEAD_DIM, (h + 1) * B_HEAD_DIM) for h in hs]
    part = lambda j: [y[:, j * bw + h * B_HEAD_DIM:j * bw + (h + 1) * B_HEAD_DIM] for h in hs]
    l2n = lambda x: x * lax.rsqrt(jnp.sum(x * x, axis=-1, keepdims=True) + NORM_EPS)
    q = [l2n(x) * B_HEAD_DIM ** -0.5 for x in part(0)]
    k = [l2n(x) for x in part(1)]
    v = part(2)
    b = [beta[:, SM_BETA + h:SM_BETA + h + 1] for h in hs]
    gc = [gcum[:, SM_A + h:SM_A + h + 1] for h in hs]
    gr = [gcum_t[SM_A + h:SM_A + h + 1, :] for h in hs]
    gl = [x[ch - 1:ch, :] for x in gc]
    eg = [jnp.exp(x) for x in gc]
    decay = [jnp.where(lower, jnp.exp(jnp.where(lower, gc[h] - gr[h], 0.0)), 0.0) for h in hs]
    kb = [k[h] * b[h] for h in hs]
    a_mat = [jnp.where(strict, _mm(kb[h], k[h], _NT) * decay[h], 0.0) for h in hs]
    t_mat = [eye - a for a in a_mat]
    pw = a_mat
    for _ in range(ch.bit_length() - 2):
        pw = [_dot3(p, p) for p in pw]
        t_mat = [t_mat[h] + _dot3(t_mat[h], pw[h]) for h in hs]
    u = [_mm(t_mat[h], v[h] * b[h]) for h in hs]
    wm = [_mm(t_mat[h], kb[h] * eg[h]) for h in hs]
    qk = [jnp.where(lower, _mm(q[h], k[h], _NT) * decay[h], 0.0) for h in hs]
    state = [s_scr[h] for h in hs]
    v_new = [u[h] - _mm(wm[h], state[h]) for h in hs]
    o = [_mm(q[h] * eg[h], state[h]) + _mm(qk[h], v_new[h]) for h in hs]
    kd = [k[h] * jnp.exp(gl[h] - gc[h]) for h in hs]
    new_state = [state[h] * jnp.exp(gl[h]) + _mm(kd[h], v_new[h], _TN) for h in hs]
    for h in hs:
        s_scr[h] = new_state[h]
    for h in hs:
        o_h = _rms_rows(o[h]) * onorm_ref[...] * _silu(z_ref[:, sl[h]])
        o_ref[:, sl[h]] = o_h.astype(o_ref.dtype)


def _gdn(p2, p1, conv_w, a_log, dt_bias, o_norm, *, pad):
    tp = p2.shape[0]
    bw = B_HEADS * B_HEAD_DIM
    alog = jnp.zeros((1, LANES), F32).at[0, SM_A:SM_A + B_HEADS].set(a_log.astype(F32))
    dtb = jnp.zeros((1, LANES), F32).at[0, SM_A:SM_A + B_HEADS].set(dt_bias.astype(F32))
    ch = GDN_CHUNK
    return pl.pallas_call(
        functools.partial(_gdn_kernel, pad=pad),
        name="gdn",
        out_shape=jax.ShapeDtypeStruct((tp, bw), BF16),
        grid=(tp // ch,),
        in_specs=[
            pl.BlockSpec((ch, 3 * bw), lambda c: (c, 0)),
            pl.BlockSpec((ch, bw), lambda c: (c, 3)),
            pl.BlockSpec((ch, LANES), lambda c: (c, 0)),
            pl.BlockSpec((CONV_K, 3 * bw), lambda c: (0, 0)),
            pl.BlockSpec((1, LANES), lambda c: (0, 0)),
            pl.BlockSpec((1, LANES), lambda c: (0, 0)),
            pl.BlockSpec((1, B_HEAD_DIM), lambda c: (0, 0)),
        ],
        out_specs=pl.BlockSpec((ch, bw), lambda c: (c, 0)),
        scratch_shapes=[pltpu.VMEM((SUBLANES + ch, 3 * bw), F32),
                        pltpu.VMEM((B_HEADS, B_HEAD_DIM, B_HEAD_DIM), F32)],
        compiler_params=_params("arbitrary"),
    )(p2, p2, p1, conv_w.astype(F32), alog, dtb, o_norm.reshape(1, B_HEAD_DIM).astype(F32))


def _ret_kernel(q_ref, k_ref, v_ref, g_ref, cos_ref, sin_ref, dmask_ref, qdec_ref, kdec_ref,
                cdec_ref, o_ref, r_scr):
    @pl.when(pl.program_id(1) == 0)
    def _():
        r_scr[...] = jnp.zeros_like(r_scr)

    cos, sin = cos_ref[...], sin_ref[...]
    even = lax.broadcasted_iota(jnp.int32, cos.shape, 1) % 2 == 0

    def rot(x):
        partner = jnp.where(even, pltpu.roll(x, C_QK_DIM - 1, axis=1), pltpu.roll(x, 1, axis=1))
        return x * cos + partner * sin

    q = rot(q_ref[...].astype(F32))
    k = rot(k_ref[...].astype(F32)) * C_QK_DIM ** -0.5
    v = v_ref[...]
    s = lax.dot_general(q.astype(BF16), k.astype(BF16), _NT, preferred_element_type=F32)
    s = s * dmask_ref[...]
    r = r_scr[...]
    o = jnp.dot(s.astype(BF16), v, preferred_element_type=F32)
    o = o + jnp.dot((q * qdec_ref[...]).astype(BF16), r.astype(BF16), preferred_element_type=F32)
    kd = (k * kdec_ref[...]).astype(BF16)
    r_scr[...] = r * cdec_ref[...] + lax.dot_general(kd, v, _TN, preferred_element_type=F32)
    gate = g_ref[...].astype(F32)
    o_ref[...] = (_silu(gate) * _rms_rows(o)).astype(o_ref.dtype)


def _retention(r_all, cos, sin):
    tp = r_all.shape[0]
    ch = RET_CHUNK
    v0 = 2 * C_HEADS * C_QK_DIM // C_V_DIM
    log_gamma = jnp.log(1.0 - 2.0 ** (-5.0 - jnp.arange(C_HEADS, dtype=F32)))
    pos = jnp.arange(ch, dtype=F32)
    rel = pos[:, None] - pos[None, :]
    dmask = jnp.where(rel >= 0, jnp.exp(jnp.maximum(rel, 0.0)[None] * log_gamma[:, None, None]), 0.0)
    qdec = jnp.exp((pos + 1.0)[None, :] * log_gamma[:, None])[:, :, None]
    kdec = jnp.exp((ch - 1.0 - pos)[None, :] * log_gamma[:, None])[:, :, None]
    cdec = jnp.broadcast_to(jnp.exp(ch * log_gamma)[:, None, None], (C_HEADS, 1, C_V_DIM))
    return pl.pallas_call(
        _ret_kernel,
        name="retention",
        out_shape=jax.ShapeDtypeStruct((tp, C_HEADS * C_V_DIM), BF16),
        grid=(C_HEADS, tp // ch),
        in_specs=[
            pl.BlockSpec((ch, C_QK_DIM), lambda h, c: (c, h)),
            pl.BlockSpec((ch, C_QK_DIM), lambda h, c: (c, C_HEADS + h)),
            pl.BlockSpec((ch, C_V_DIM), lambda h, c: (c, v0 + h)),
            pl.BlockSpec((ch, C_V_DIM), lambda h, c: (c, v0 + C_HEADS + h)),
            pl.BlockSpec((ch, C_QK_DIM), lambda h, c: (c, 0)),
            pl.BlockSpec((ch, C_QK_DIM), lambda h, c: (c, 0)),
            pl.BlockSpec((None, ch, ch), lambda h, c: (h, 0, 0)),
            pl.BlockSpec((None, ch, 1), lambda h, c: (h, 0, 0)),
            pl.BlockSpec((None, ch, 1), lambda h, c: (h, 0, 0)),
            pl.BlockSpec((None, 1, C_V_DIM), lambda h, c: (h, 0, 0)),
        ],
        out_specs=pl.BlockSpec((ch, C_V_DIM), lambda h, c: (c, h)),
        scratch_shapes=[pltpu.VMEM((C_QK_DIM, C_V_DIM), F32)],
        compiler_params=_params("arbitrary", "arbitrary"),
    )(r_all, r_all, r_all, r_all, cos, sin, dmask, qdec, kdec, cdec)


def _hybrid_mixer(u_hi, u_lo, w_in, q_norm, w_uq, w_qidx, kv_norm, w_ukv, kidx_norm,
                  conv_w, a_log, dt_bias, o_norm, w_out, *, top_k, pad, tm, tm_wide):
    tp = u_hi.shape[0]
    o = [0]
    for s in (A_Q_RANK, A_KV_RANK, IDX_DIM, IDX_HEADS, 3 * B_HEADS * B_HEAD_DIM,
              B_HEADS * B_HEAD_DIM, B_HEADS, B_HEADS):
        o.append(o[-1] + s)
    col = lambda i: w_in[:, o[i]:o[i + 1]]
    n_small = IDX_DIM + IDX_HEADS + 2 * B_HEADS
    w1 = jnp.concatenate([col(0), col(2), col(3), col(6), col(7),
                          jnp.zeros((w_in.shape[0], LANES - n_small), w_in.dtype)], axis=1)
    tn2 = 2 * MXU_DIM
    n2 = o[6] - o[4] + A_KV_RANK
    w2 = jnp.concatenate([col(4), col(5), col(1),
                          jnp.zeros((w_in.shape[0], -n2 % tn2), w_in.dtype)], axis=1).astype(BF16)
    p1t, small = _matmul("hy_in_x3", (u_hi, u_lo), _hilo(w1), F32, tm=tm, tn=w1.shape[1],
                         epilogue="transposed")
    p2 = _matmul("hy_in", (u_hi,), (w2,), F32, tm=tm_wide, tn=tn2)

    cqt_hi, cqt_lo = _rms_t(p1t, q_norm, width=A_Q_RANK, row_block=0, tc=tm, lo=True)
    up_rows = w_uq.shape[1]
    qi3t = _matmul("dsa_qidx", _hilo(w_qidx.T), (cqt_hi, cqt_lo), BF16, tm=up_rows, tn=tm,
                   epilogue="split3")
    q_scale = A_HEAD_DIM ** -0.5 * math.log2(math.e)
    qt = _matmul("dsa_q", ((w_uq.T * q_scale).astype(BF16),), (cqt_hi,), BF16, tm=up_rows, tn=tm)
    k3 = _kidx(small, kidx_norm, tm=tm)
    wt = p1t[A_Q_RANK + SM_WIDX:A_Q_RANK + SM_BETA]
    ckv_block = (o[6] - o[4]) // A_KV_RANK
    kvn, kvnt = _rms_both(p2, kv_norm, width=A_KV_RANK, col_block=ckv_block, tm=tm)
    w_kv = w_ukv.reshape(A_KV_RANK, A_HEADS, 2, A_HEAD_DIM)
    w_k = w_kv[:, :, 0].reshape(A_KV_RANK, A_HEADS * A_HEAD_DIM).astype(BF16)
    w_vt = w_kv[:, :, 1].reshape(A_KV_RANK, A_HEADS * A_HEAD_DIM).T.astype(BF16)
    k = _matmul("dsa_k", (kvn,), (w_k,), BF16, tm=tm, tn=up_rows)
    vt = _matmul("dsa_vt", (w_vt,), (kvnt,), BF16, tm=up_rows, tn=tm)
    o_a = _dsa(qt, qi3t, wt, k3, k, vt, top_k=top_k, pad=pad, tq=MXU_DIM, tk=tm)

    o_b = _gdn(p2, small, conv_w, a_log, dt_bias, o_norm, pad=pad)

    return _matmul_w32("hy_out", [o_a, o_b], w_out[None], 0, BF16, tm=tm_wide, tn=2 * MXU_DIM)


def _retention_mixer(u, w_in, w_out, *, pad, tm, tm_wide):
    tp = u.shape[0]
    half = C_QK_DIM // 2
    r_all = _matmul_w32("ret_in", [u], w_in[None], 0, BF16, tm=tm_wide, tn=2 * MXU_DIM)
    inv = 1.0 / (ROT_BASE ** jnp.linspace(0.0, 1.0, half, dtype=F32))
    posn = (jnp.arange(tp) - pad).astype(F32)
    ang = posn[:, None] * jnp.repeat(inv, 2)[None, :]
    sign = jnp.tile(jnp.array([-1.0, 1.0], F32), half)
    og = _retention(r_all, jnp.cos(ang), jnp.sin(ang) * sign[None, :])
    return _matmul_w32("ret_out", [og], w_out[None], 0, BF16, tm=tm_wide, tn=2 * MXU_DIM,
                       vmem=BIG_TILE_VMEM_LIMIT)


def _ffn(u, w_gate, w_up, w_down, layer, *, tm, tm_wide):
    act = _ffn_act(u, w_gate, w_up, layer, tm=tm_wide, tn=2 * MXU_DIM)
    return _matmul_w32("ffn_down", [act], w_down, layer, BF16, tm=tm, tn=2 * MXU_DIM,
                       vmem=BIG_TILE_VMEM_LIMIT)


def kernel(x, meta_tokens, mix_norm_pre, mix_norm_post, ffn_norm_pre, ffn_norm_post, hy_w_in, dsa_q_norm, dsa_w_uq, dsa_w_qidx, dsa_kv_norm, dsa_w_ukv, dsa_kidx_norm, gdn_conv_w, gdn_a_log, gdn_dt_bias, gdn_o_norm, hy_w_out, ret_w_in, ret_w_out, ffn_w_gate, ffn_w_up, ffn_w_down):
    b, seq, d = x.shape
    assert b == 1 and mix_norm_pre.shape[0] == 2
    pad = (-N_META) % ROW_ALIGN
    tp = pad + N_META + seq
    tm = ROW_TILE
    assert tp % tm == 0
    tm_wide = WIDE_ROW_TILE if tp % WIDE_ROW_TILE == 0 else tm
    top_k = min(TOPK_MAX, seq // TOPK_FRAC)
    h, u_hi, u_lo = _embed_norm(x[0], meta_tokens, mix_norm_pre[0], pad=pad)
    mix = _hybrid_mixer(u_hi, u_lo, hy_w_in[0], dsa_q_norm[0], dsa_w_uq[0], dsa_w_qidx[0],
                        dsa_kv_norm[0], dsa_w_ukv[0], dsa_kidx_norm[0], gdn_conv_w[0],
                        gdn_a_log[0], gdn_dt_bias[0], gdn_o_norm[0], hy_w_out[0],
                        top_k=top_k, pad=pad, tm=tm, tm_wide=tm_wide)
    h, u = _resid_norm(h, mix, mix_norm_post[0], ffn_norm_pre[0], tm=tm)
    f = _ffn(u, ffn_w_gate, ffn_w_up, ffn_w_down, 0, tm=tm, tm_wide=tm_wide)
    h, u = _resid_norm(h, f, ffn_norm_post[0], mix_norm_pre[1], tm=tm)
    mix = _retention_mixer(u, ret_w_in[0], ret_w_out[0], pad=pad, tm=tm, tm_wide=tm_wide)
    h, u = _resid_norm(h, mix, mix_norm_post[1], ffn_norm_pre[1], tm=tm)
    f = _ffn(u, ffn_w_gate, ffn_w_up, ffn_w_down, 1, tm=tm, tm_wide=tm_wide)
    out = _resid_norm(h, f, ffn_norm_post[1], None, tm=ROW_ALIGN, row0=pad + N_META,
                      out_dtype=x.dtype)
    return out[None]
```

```python
import functools
import math

import jax
import jax.numpy as jnp
from jax import lax
from jax.experimental import pallas as pl
from jax.experimental.pallas import tpu as pltpu

F32 = jnp.float32
BF16 = jnp.bfloat16

N_META = 16
NORM_EPS = 1e-6

A_HEADS = 8
A_HEAD_DIM = 128
A_Q_RANK = 512
A_KV_RANK = 256
IDX_HEADS = 16
IDX_DIM = 64
TOPK_MAX = 256
TOPK_FRAC = 4

B_HEADS = 8
B_HEAD_DIM = 128
CONV_K = 4
GDN_CHUNK = 128
DSA_HEADS_PER_STEP = 2
DSA_VMEM_LIMIT = 56 * 1024 * 1024
BIG_TILE_VMEM_LIMIT = 56 * 1024 * 1024
DSA_SEGMENTS = 3
DSA_SUB_BLOCK = 64

C_HEADS = 8
C_QK_DIM = 256
C_V_DIM = 512
ROT_BASE = 10000.0
RET_CHUNK = 384

LANES = 128
SUBLANES = 8
MXU_DIM = 256
ROW_ALIGN = MXU_DIM
ROW_TILE = 3 * MXU_DIM
WIDE_ROW_TILE = 11 * LANES
VMEM_LIMIT = 48 * 1024 * 1024

SM_WIDX = IDX_DIM
SM_BETA = SM_WIDX + IDX_HEADS
SM_A = SM_BETA + B_HEADS

INT_MIN = -2 ** 31
KEY_FLT_MAX = 0x7F7FFFFF
KEY_MIN_NORMAL = 0x00800000
KEY_FLT_LOWEST = (0xFF7FFFFF ^ 0x7FFFFFFF) - 2 ** 32


def _params(*sem, vmem=VMEM_LIMIT):
    return pltpu.CompilerParams(dimension_semantics=sem, vmem_limit_bytes=vmem)


def _hilo(x):
    hi = x.astype(BF16)
    return hi, (x - hi.astype(F32)).astype(BF16)


def _silu(x):
    return x * jax.nn.sigmoid(x)


def _mm_kernel(*refs, n_parts, epilogue):
    a, b, o = refs[:n_parts], refs[n_parts:2 * n_parts], refs[2 * n_parts:]
    a_hi, b_hi = a[0][...], b[0][...]
    acc = jnp.dot(a_hi, b_hi, preferred_element_type=F32)
    if n_parts == 2:
        acc += jnp.dot(a_hi, b[1][...], preferred_element_type=F32)
        acc += jnp.dot(a[1][...], b_hi, preferred_element_type=F32)
    if epilogue is None:
        o[0][...] = acc.astype(o[0].dtype)
    elif epilogue == "transposed":
        o[0][...] = acc.T
        o[1][...] = acc[:, acc.shape[1] - LANES:]
    elif epilogue == "split3":
        for g in range(acc.shape[0] // IDX_DIM):
            hi, lo = _hilo(acc[g * IDX_DIM:(g + 1) * IDX_DIM, :])
            base = g * MXU_DIM
            o[0][base:base + IDX_DIM, :] = hi
            o[0][base + IDX_DIM:base + 2 * IDX_DIM, :] = lo
            o[0][base + 2 * IDX_DIM:base + 3 * IDX_DIM, :] = hi
            o[0][base + 3 * IDX_DIM:base + MXU_DIM, :] = jnp.zeros_like(hi)


def _matmul(name, a_parts, b_parts, out_dtype, *, tm, tn, epilogue=None):
    m, kdim = a_parts[0].shape
    n = b_parts[0].shape[1]
    assert m % tm == 0 and n % tn == 0, (m, n, tm, tn)
    assert len(a_parts) == len(b_parts)
    if epilogue is None:
        out_shape = [jax.ShapeDtypeStruct((m, n), out_dtype)]
        out_specs = [pl.BlockSpec((tm, tn), lambda i, j: (i, j))]
    elif epilogue == "transposed":
        assert tn == n
        out_shape = [jax.ShapeDtypeStruct((n, m), out_dtype),
                     jax.ShapeDtypeStruct((m, LANES), out_dtype)]
        out_specs = [pl.BlockSpec((n, tm), lambda i, j: (0, i)),
                     pl.BlockSpec((tm, LANES), lambda i, j: (i, 0))]
    else:
        scale = MXU_DIM // IDX_DIM
        out_shape = [jax.ShapeDtypeStruct((m * scale, n), out_dtype)]
        out_specs = [pl.BlockSpec((tm * scale, tn), lambda i, j: (i, j))]
    out = pl.pallas_call(
        functools.partial(_mm_kernel, n_parts=len(a_parts), epilogue=epilogue),
        name=name,
        out_shape=out_shape,
        grid=(m // tm, n // tn),
        in_specs=[pl.BlockSpec((tm, kdim), lambda i, j: (i, 0))] * len(a_parts)
        + [pl.BlockSpec((kdim, tn), lambda i, j: (0, j))] * len(b_parts),
        out_specs=out_specs,
        compiler_params=_params("parallel", "parallel"),
    )(*a_parts, *b_parts)
    return out[0] if len(out) == 1 else out


def _mm_w32_kernel(*refs, n_a):
    a, w_ref, o_ref = refs[:n_a], refs[n_a], refs[n_a + 1]
    w = w_ref[...].astype(BF16)
    acc, k0 = None, 0
    for a_ref in a:
        k1 = k0 + a_ref.shape[1]
        part = jnp.dot(a_ref[...], w[k0:k1, :], preferred_element_type=F32)
        acc = part if acc is None else acc + part
        k0 = k1
    o_ref[...] = acc.astype(o_ref.dtype)


def _matmul_w32(name, a_list, w, layer, out_dtype, *, tm, tn, col0=0, n=None, vmem=VMEM_LIMIT):
    m = a_list[0].shape[0]
    kdim = sum(a.shape[1] for a in a_list)
    n = w.shape[2] - col0 if n is None else n
    assert w.shape[1] == kdim and m % tm == 0 and n % tn == 0 and col0 % tn == 0
    jb = col0 // tn
    return pl.pallas_call(
        functools.partial(_mm_w32_kernel, n_a=len(a_list)),
        name=name,
        out_shape=jax.ShapeDtypeStruct((m, n), out_dtype),
        grid=(m // tm, n // tn),
        in_specs=[pl.BlockSpec((tm, a.shape[1]), lambda i, j: (i, 0)) for a in a_list]
        + [pl.BlockSpec((None, kdim, tn), lambda i, j: (layer, 0, j + jb))],
        out_specs=pl.BlockSpec((tm, tn), lambda i, j: (i, j)),
        compiler_params=_params("parallel", "parallel", vmem=vmem),
    )(*a_list, w)


def _ffn_act_kernel(x_ref, wg_ref, wu_ref, o_ref):
    x = x_ref[...]
    g = jnp.dot(x, wg_ref[...].astype(BF16), preferred_element_type=F32)
    u = jnp.dot(x, wu_ref[...].astype(BF16), preferred_element_type=F32)
    o_ref[...] = (_silu(g) * u).astype(o_ref.dtype)


def _ffn_act(x, wg, wu, layer, *, tm, tn):
    m, kdim = x.shape
    n = wg.shape[2]
    assert m % tm == 0 and n % tn == 0
    w_spec = pl.BlockSpec((None, kdim, tn), lambda i, j: (layer, 0, j))
    return pl.pallas_call(
        _ffn_act_kernel,
        name="ffn_act",
        out_shape=jax.ShapeDtypeStruct((m, n), BF16),
        grid=(m // tm, n // tn),
        in_specs=[pl.BlockSpec((tm, kdim), lambda i, j: (i, 0)), w_spec, w_spec],
        out_specs=pl.BlockSpec((tm, tn), lambda i, j: (i, j)),
        compiler_params=_params("parallel", "parallel"),
    )(x, wg, wu)


def _rms_rows(x):
    return x * lax.rsqrt(jnp.mean(x * x, axis=-1, keepdims=True) + NORM_EPS)


def _embed_kernel(x_ref, head_ref, g_ref, h_ref, hi_ref, lo_ref):
    h = jnp.where(pl.program_id(0) == 0, head_ref[...], x_ref[...].astype(F32))
    h_ref[...] = h
    y = _rms_rows(h) * g_ref[...]
    hi = y.astype(BF16)
    hi_ref[...] = hi
    lo_ref[...] = (y - hi.astype(F32)).astype(BF16)


def _embed_norm(x, meta, gain, *, pad):
    seq, d = x.shape
    tb = pad + meta.shape[0]
    assert seq % tb == 0
    head = jnp.concatenate([jnp.zeros((pad, d), F32), meta.astype(F32)], axis=0)
    tp = tb + seq
    row = pl.BlockSpec((tb, d), lambda i: (i, 0))
    return pl.pallas_call(
        _embed_kernel,
        name="embed_norm",
        out_shape=[jax.ShapeDtypeStruct((tp, d), F32)] + [jax.ShapeDtypeStruct((tp, d), BF16)] * 2,
        grid=(tp // tb,),
        in_specs=[pl.BlockSpec((tb, d), lambda i: (jnp.maximum(i - 1, 0), 0)),
                  pl.BlockSpec((tb, d), lambda i: (0, 0)),
                  pl.BlockSpec((1, d), lambda i: (0, 0))],
        out_specs=[row, row, row],
        compiler_params=_params("parallel"),
    )(x, head, gain.reshape(1, d).astype(F32))


def _rms_both_kernel(x_ref, g_ref, o_ref, ot_ref):
    y = _rms_rows(x_ref[...]) * g_ref[...]
    o_ref[...] = y.astype(BF16)
    ot_ref[...] = y.T.astype(BF16)


def _rms_both(x, gain, *, width, col_block, tm):
    m = x.shape[0]
    return pl.pallas_call(
        _rms_both_kernel,
        name="rms_both",
        out_shape=[jax.ShapeDtypeStruct((m, width), BF16), jax.ShapeDtypeStruct((width, m), BF16)],
        grid=(m // tm,),
        in_specs=[pl.BlockSpec((tm, width), lambda i: (i, col_block)),
                  pl.BlockSpec((1, width), lambda i: (0, 0))],
        out_specs=[pl.BlockSpec((tm, width), lambda i: (i, 0)),
                   pl.BlockSpec((width, tm), lambda i: (0, i))],
        compiler_params=_params("parallel"),
    )(x, gain.reshape(1, width).astype(F32))


def _rms_t_kernel(x_ref, g_ref, *o_refs):
    x = x_ref[...]
    y = x * lax.rsqrt(jnp.mean(x * x, axis=0, keepdims=True) + NORM_EPS) * g_ref[...]
    hi = y.astype(BF16)
    o_refs[0][...] = hi
    if len(o_refs) == 2:
        o_refs[1][...] = (y - hi.astype(F32)).astype(BF16)


def _rms_t(xt, gain, *, width, row_block, tc, lo=False):
    t = xt.shape[1]
    n_out = 2 if lo else 1
    out = pl.pallas_call(
        _rms_t_kernel,
        name="rms_t",
        out_shape=[jax.ShapeDtypeStruct((width, t), BF16)] * n_out,
        grid=(t // tc,),
        in_specs=[pl.BlockSpec((width, tc), lambda i: (row_block, i)),
                  pl.BlockSpec((width, 1), lambda i: (0, 0))],
        out_specs=[pl.BlockSpec((width, tc), lambda i: (0, i))] * n_out,
        compiler_params=_params("parallel"),
    )(xt, gain.reshape(width, 1).astype(F32))
    return out if lo else out[0]


def _resid_kernel(h_ref, u_ref, gp_ref, *rest, with_next):
    hn = h_ref[...] + _rms_rows(u_ref[...].astype(F32)) * gp_ref[...]
    if with_next:
        gn_ref, hn_ref, un_ref = rest
        un_ref[...] = (_rms_rows(hn) * gn_ref[...]).astype(BF16)
    else:
        (hn_ref,) = rest
    hn_ref[...] = hn.astype(hn_ref.dtype)


def _resid_norm(h, u, g_post, g_next, *, tm, row0=0, out_dtype=F32):
    m, d = h.shape
    assert row0 % tm == 0 and m % tm == 0
    b0 = row0 // tm
    row_in = pl.BlockSpec((tm, d), lambda i: (i + b0, 0))
    row_out = pl.BlockSpec((tm, d), lambda i: (i, 0))
    vec = pl.BlockSpec((1, d), lambda i: (0, 0))
    with_next = g_next is not None
    gains = [g_post.reshape(1, d).astype(F32)]
    out_shape = [jax.ShapeDtypeStruct((m - row0, d), out_dtype)]
    if with_next:
        gains.append(g_next.reshape(1, d).astype(F32))
        out_shape.append(jax.ShapeDtypeStruct((m - row0, d), BF16))
    out = pl.pallas_call(
        functools.partial(_resid_kernel, with_next=with_next),
        name="resid_norm",
        out_shape=out_shape,
        grid=((m - row0) // tm,),
        in_specs=[row_in, row_in] + [vec] * len(gains),
        out_specs=[row_out] * len(out_shape),
        compiler_params=_params("parallel"),
    )(h, u, *gains)
    return out if with_next else out[0]


def _kidx_kernel(sm_ref, g_ref, o_ref):
    sm = sm_ref[...]
    lane = lax.broadcasted_iota(jnp.int32, sm.shape, 1)
    x = jnp.where(lane < IDX_DIM, sm, 0.0)
    ms = jnp.sum(x * x, axis=-1, keepdims=True) * (1.0 / IDX_DIM)
    y = x * lax.rsqrt(ms + NORM_EPS) * g_ref[...]
    hi = y.astype(BF16)
    lo = (y - hi.astype(F32)).astype(BF16)
    y2 = y + pltpu.roll(y, IDX_DIM, axis=1)
    o_ref[:, 0:LANES] = y2.astype(BF16)
    o_ref[:, LANES:2 * LANES] = lo


def _kidx(p1, gain, *, tm):
    m = p1.shape[0]
    g = jnp.zeros((1, LANES), F32).at[0, :IDX_DIM].set(gain.astype(F32))
    return pl.pallas_call(
        _kidx_kernel,
        name="dsa_kidx",
        out_shape=jax.ShapeDtypeStruct((m, 2 * LANES), BF16),
        grid=(m // tm,),
        in_specs=[pl.BlockSpec((tm, LANES), lambda i: (i, 0)),
                  pl.BlockSpec((1, LANES), lambda i: (0, 0))],
        out_specs=pl.BlockSpec((tm, 2 * LANES), lambda i: (i, 0)),
        compiler_params=_params("parallel"),
    )(p1, g)


def _dsa_kernel(qt_ref, qi3t_ref, wt_ref, k3_ref, k_ref, vt_ref, o_ref, sc_scr, sc16_scr, acc_scr,
                sa_scr, sb_scr, pa_scr, pb_scr, *, tq, tk, sb, qb0, top_k, pad, hps, w_scale,
                pos_bits):
    qi = pl.program_id(0)
    hd = pl.program_id(1)
    q0 = (qi + qb0) * tq
    nkt = (q0 + tq + tk - 1) // tk

    krel = lax.broadcasted_iota(jnp.int32, (tk, tq), 0)

    tp = k_ref.shape[0]

    @pl.when(hd == 0)
    def _select():
        sc_scr[tp:tp + tk, :] = jnp.full((tk, tq), -jnp.inf, F32)
        qpos = q0 + lax.broadcasted_iota(jnp.int32, (tk, tq), 1)

        def score_tile(kt, carry):
            ks = pl.multiple_of(kt * tk, tk)
            k3 = k3_ref[pl.ds(ks, tk), :]
            acc = jnp.zeros((tk, tq), F32)
            for ih in range(IDX_HEADS):
                d = jnp.dot(k3, qi3t_ref[ih * MXU_DIM:(ih + 1) * MXU_DIM, :],
                            preferred_element_type=F32)
                acc = acc + (wt_ref[ih:ih + 1, :] * w_scale) * jnp.maximum(d, 0.0)
            kpos = ks + krel
            ok = (kpos <= qpos) & (kpos >= pad)
            masked = jnp.where(ok, acc, -jnp.inf)
            sc_scr[pl.ds(ks, tk), :] = masked
            sc16_scr[pl.ds(ks, tk), :] = masked.astype(BF16)
            return carry

        lax.fori_loop(0, nkt, score_tile, 0)

        def count(pred):
            def body(kt, acc):
                ks = pl.multiple_of(kt * tk, tk)
                hit = pred(sc_scr[pl.ds(ks, tk), :], ks).astype(jnp.int32)
                return acc + jnp.sum(hit.reshape(tk // SUBLANES, SUBLANES, tq), axis=0)

            acc = lax.fori_loop(0, nkt, body, jnp.zeros((SUBLANES, tq), jnp.int32))
            return jnp.sum(acc, axis=0, keepdims=True)

        def key_to_f32(t):
            t = jnp.clip(t, jnp.int32(KEY_FLT_LOWEST), jnp.int32(KEY_FLT_MAX))
            return pltpu.bitcast(jnp.where(t >= 0, t, t ^ jnp.int32(0x7FFFFFFF)), F32)

        def count16(cand_b):
            rows = 2 * SUBLANES
            one, nil = jnp.ones((), BF16), jnp.zeros((), BF16)
            cand_b = jnp.maximum(cand_b, jnp.finfo(BF16).min)

            def body(kt, acc):
                ks = pl.multiple_of(kt * tk, tk)
                hit = jnp.where(sc16_scr[pl.ds(ks, tk), :] >= cand_b, one, nil)
                slabs = [hit[r:r + rows, :] for r in range(0, tk, rows)]
                while len(slabs) > 1:
                    pairs = [a + b for a, b in zip(slabs[0::2], slabs[1::2])]
                    slabs = pairs + slabs[2 * len(pairs):]
                return acc + slabs[0].astype(F32)

            acc = lax.fori_loop(0, nkt, body, jnp.zeros((rows, tq), F32))
            return jnp.sum(acc, axis=0, keepdims=True).astype(jnp.int32)

        zero = jnp.zeros((1, tq), jnp.int32)
        accept = lambda n, cand: (n >= top_k) & (cand <= jnp.int32(KEY_FLT_MAX))
        n0 = count16(jnp.zeros((1, tq), BF16))
        t_hi = jnp.where(n0 >= top_k, zero, jnp.int32(INT_MIN))

        def hi_body(i, t):
            cand = t + lax.shift_left(jnp.int32(1), 30 - i)
            n = count16(key_to_f32(cand).astype(BF16))
            return jnp.where(accept(n, cand), cand, t)

        t_hi = lax.fori_loop(0, 15, hi_body, t_hi)

        base = t_hi - jnp.int32(1 << 15)

        def lo_body(i, state):
            off, n_at = state
            bit = lax.shift_left(jnp.int32(1), 16 - i)
            cand = base + (off | bit)
            cand_f = key_to_f32(cand)
            n = count(lambda s, ks: s >= cand_f)
            ok = accept(n, cand)
            return jnp.where(ok, off | bit, off), jnp.where(ok, n, n_at)

        off, n_at = lax.fori_loop(0, 17, lo_body, (zero, zero + (top_k + 1)))
        few = t_hi == jnp.int32(INT_MIN)
        t = jnp.where(few, t_hi, base + off)
        thr = key_to_f32(t)

        n_ge = jnp.where(few, zero, n_at)

        @pl.when(jnp.max(n_ge) > top_k)
        def _ties():
            flushed = (t >= jnp.int32(-KEY_MIN_NORMAL)) & (t < jnp.int32(KEY_MIN_NORMAL))
            thr_next = key_to_f32(jnp.where(flushed, jnp.int32(KEY_MIN_NORMAL), t + 1))
            tie = lambda s: (s >= thr) & jnp.logical_not(s >= thr_next)
            want = top_k - count(lambda s, ks: s >= thr_next)

            def pos_body(i, cut):
                cand = cut + lax.shift_left(jnp.int32(1), pos_bits - 1 - i)
                n = count(lambda s, ks: tie(s) & (ks + krel < cand))
                return jnp.where(n < want, cand, cut)

            cut = lax.fori_loop(0, pos_bits, pos_body, zero)

            def drop_tile(kt, carry):
                ks = pl.multiple_of(kt * tk, tk)
                s = sc_scr[pl.ds(ks, tk), :]
                sc_scr[pl.ds(ks, tk), :] = jnp.where(tie(s) & (ks + krel > cut), -jnp.inf, s)
                return carry

            lax.fori_loop(0, nkt, drop_tile, 0)

        def bias_tile(kt, carry):
            ks = pl.multiple_of(kt * tk, tk)
            sc_scr[pl.ds(ks, tk), :] = jnp.where(sc_scr[pl.ds(ks, tk), :] >= thr, 0.0, -jnp.inf)
            return carry

        lax.fori_loop(0, nkt, bias_tile, 0)

    hd_sl = [slice(j * A_HEAD_DIM, (j + 1) * A_HEAD_DIM) for j in range(hps)]
    q = [qt_ref[sl, :] for sl in hd_sl]

    def logits_into(kt, s_ref):
        ks = pl.multiple_of(kt * tk, tk)
        for j in range(hps):
            s_ref[j] = jnp.dot(k_ref[pl.ds(ks, tk), hd_sl[j]], q[j], preferred_element_type=F32)

    fold = lambda x: x.reshape(sb // SUBLANES, SUBLANES, tq)

    def att_tile(kt, kt_next, bias_row, s_cur, s_next, p_ref, carry):
        logits_into(kt_next, s_next)
        ks = pl.multiple_of(kt * tk, tk)

        def masked(j, i):
            r = i * sb
            return s_cur[j, r:r + sb, :] + sc_scr[pl.ds(bias_row + r, sb), :]

        out, alphas = [], []
        for j in range(hps):
            m, l = carry[j]
            mx = jnp.full((SUBLANES, tq), -jnp.inf, F32)
            for i in range(tk // sb):
                mx = jnp.maximum(mx, jnp.max(fold(masked(j, i)), axis=0))
            m_new = jnp.maximum(m, jnp.max(mx, axis=0, keepdims=True))
            m_safe = jnp.where(m_new == -jnp.inf, 0.0, m_new)
            for i in range(tk // sb):
                p = jnp.exp2(masked(j, i) - m_safe)
                p_ref[j, i * sb:(i + 1) * sb, :] = p.astype(BF16)
            alphas.append(jnp.exp2(m - m_safe))
            out.append(m_new)
        ones = jnp.ones((2 * SUBLANES, tk), BF16)
        pv = [jnp.dot(jnp.concatenate([vt_ref[hd_sl[j], pl.ds(ks, tk)], ones], axis=0), p_ref[j],
                      preferred_element_type=F32) for j in range(hps)]
        for j in range(hps):
            acc_scr[j] = alphas[j] * acc_scr[j] + pv[j][:A_HEAD_DIM, :]
        return tuple((out[j], alphas[j] * carry[j][1] + pv[j][A_HEAD_DIM:A_HEAD_DIM + 1, :])
                     for j in range(hps))

    def att_pair(i, carry):
        kt0 = 2 * i
        kt1 = jnp.minimum(kt0 + 1, nkt - 1)
        kt2 = jnp.minimum(kt0 + 2, nkt - 1)
        row1 = pl.multiple_of(jnp.where(kt0 + 1 < nkt, kt1 * tk, tp), tk)
        carry = att_tile(kt0, kt1, pl.multiple_of(kt0 * tk, tk), sa_scr, sb_scr, pa_scr, carry)
        return att_tile(kt1, kt2, row1, sb_scr, sa_scr, pb_scr, carry)

    acc_scr[...] = jnp.zeros_like(acc_scr)
    logits_into(0, sa_scr)
    init = (jnp.full((1, tq), -jnp.inf, F32), jnp.zeros((1, tq), F32))
    res = lax.fori_loop(0, (nkt + 1) // 2, att_pair, (init,) * hps)
    for j in range(hps):
        l = res[j][1]
        o_ref[:, hd_sl[j]] = jnp.where(l > 0.0, acc_scr[j] / l, 0.0).T.astype(o_ref.dtype)


def _dsa(qt, qi3t, wt, k3, k, vt, *, top_k, pad, tq, tk):
    tp = k.shape[0]
    assert tp % tq == 0 and tp % tk == 0 and tk >= top_k
    hps = DSA_HEADS_PER_STEP
    hw = hps * A_HEAD_DIM
    n_tiles = tp // tk
    outs = []
    for seg in range(DSA_SEGMENTS):
        t0, t1 = seg * n_tiles // DSA_SEGMENTS, (seg + 1) * n_tiles // DSA_SEGMENTS
        if t1 == t0:
            continue
        qb0, nq, nk = t0 * tk // tq, (t1 - t0) * tk // tq, t1 * tk
        kern = functools.partial(
            _dsa_kernel, tq=tq, tk=tk, sb=DSA_SUB_BLOCK, qb0=qb0, top_k=top_k, pad=pad, hps=hps,
            w_scale=(IDX_HEADS * IDX_DIM) ** -0.5, pos_bits=tp.bit_length())
        outs.append(pl.pallas_call(
            kern,
            name="dsa",
            out_shape=jax.ShapeDtypeStruct((nq * tq, A_HEADS * A_HEAD_DIM), BF16),
            grid=(nq, A_HEADS // hps),
            in_specs=[
                pl.BlockSpec((hw, tq), lambda i, h, qb0=qb0: (h, i + qb0)),
                pl.BlockSpec((IDX_HEADS * MXU_DIM, tq), lambda i, h, qb0=qb0: (0, i + qb0)),
                pl.BlockSpec((IDX_HEADS, tq), lambda i, h, qb0=qb0: (0, i + qb0)),
                pl.BlockSpec((nk, MXU_DIM), lambda i, h: (0, 0)),
                pl.BlockSpec((nk, hw), lambda i, h: (0, h)),
                pl.BlockSpec((hw, nk), lambda i, h: (h, 0)),
            ],
            out_specs=pl.BlockSpec((tq, hw), lambda i, h: (i, h)),
            scratch_shapes=[pltpu.VMEM((nk + tk, tq), F32), pltpu.VMEM((nk, tq), BF16),
                            pltpu.VMEM((hps, A_HEAD_DIM, tq), F32)]
            + [pltpu.VMEM((hps, tk, tq), F32)] * 2 + [pltpu.VMEM((hps, tk, tq), BF16)] * 2,
            compiler_params=_params("arbitrary", "arbitrary", vmem=DSA_VMEM_LIMIT),
        )(qt, qi3t, wt, k3, k, vt))
    return outs[0] if len(outs) == 1 else jnp.concatenate(outs, axis=0)


def _dot3(a, b, dims=(((1,), (0,)), ((), ()))):
    ah, al = _hilo(a)
    bh, bl = _hilo(b)
    f = lambda x, y: lax.dot_general(x, y, dims, preferred_element_type=F32)
    return f(ah, bh) + f(ah, bl) + f(al, bh)


_NN = (((1,), (0,)), ((), ()))
_NT = (((1,), (1,)), ((), ()))
_TN = (((0,), (0,)), ((), ()))


def _mm(a, b, dims=_NN):
    return lax.dot_general(a.astype(BF16), b.astype(BF16), dims, preferred_element_type=F32)


def _gdn_kernel(qkv_ref, z_ref, sm_ref, convw_ref, alog_ref, dtb_ref, onorm_ref, o_ref,
                xbuf, s_scr, *, pad):
    c = pl.program_id(0)
    ch = GDN_CHUNK
    hist = SUBLANES

    @pl.when(c == 0)
    def _():
        xbuf[0:hist, :] = jnp.zeros((hist, xbuf.shape[1]), F32)
        s_scr[...] = jnp.zeros_like(s_scr)

    x = qkv_ref[...]
    xbuf[hist:hist + ch, :] = x
    w = convw_ref[...]
    y = x * w[CONV_K - 1:CONV_K, :]
    for j in range(CONV_K - 1):
        y = y + xbuf[hist - (CONV_K - 1) + j:hist - (CONV_K - 1) + j + ch, :] * w[j:j + 1, :]
    xbuf[0:hist, :] = x[ch - hist:ch, :]
    y = _silu(y)

    sm = sm_ref[...]
    row = c * ch + lax.broadcasted_iota(jnp.int32, sm.shape, 0)
    beta = jax.nn.sigmoid(sm)
    zsm = sm + dtb_ref[...]
    softplus = jnp.maximum(zsm, 0.0) + jnp.log(1.0 + jnp.exp(-jnp.abs(zsm)))
    g = jnp.where(row >= pad, -jnp.exp(alog_ref[...]) * softplus, 0.0)

    ri = lax.broadcasted_iota(jnp.int32, (ch, ch), 0)
    ci = lax.broadcasted_iota(jnp.int32, (ch, ch), 1)
    lower = ri >= ci
    strict = ri > ci
    tri = jnp.where(lower, 1.0, 0.0).astype(BF16)
    g1 = g.astype(BF16)
    r1 = g - g1.astype(F32)
    g2 = r1.astype(BF16)
    g3 = (r1 - g2.astype(F32)).astype(BF16)
    gcum = (jnp.dot(tri, g1, preferred_element_type=F32)
            + jnp.dot(tri, g2, preferred_element_type=F32)
            + jnp.dot(tri, g3, preferred_element_type=F32))
    gcum_t = gcum.T
    eye = jnp.where(ri == ci, 1.0, 0.0).astype(F32)

    bw = B_HEADS * B_HEAD_DIM
    hs = range(B_HEADS)
    sl = [slice(h * B_HEAD_DIM, (h + 1) * B_HEAD_DIM) for h in hs]
    part = lambda j: [y[:, j * bw + h * B_HEAD_DIM:j * bw + (h + 1) * B_HEAD_DIM] for h in hs]
    l2n = lambda x: x * lax.rsqrt(jnp.sum(x * x, axis=-1, keepdims=True) + NORM_EPS)
    q = [l2n(x) * B_HEAD_DIM ** -0.5 for x in part(0)]
    k = [l2n(x) for x in part(1)]
    v = part(2)
    b = [beta[:, SM_BETA + h:SM_BETA + h + 1] for h in hs]
    gc = [gcum[:, SM_A + h:SM_A + h + 1] for h in hs]
    gr = [gcum_t[SM_A + h:SM_A + h + 1, :] for h in hs]
    gl = [x[ch - 1:ch, :] for x in gc]
    eg = [jnp.exp(x) for x in gc]
    decay = [jnp.where(lower, jnp.exp(jnp.where(lower, gc[h] - gr[h], 0.0)), 0.0) for h in hs]
    kb = [k[h] * b[h] for h in hs]
    a_mat = [jnp.where(strict, _mm(kb[h], k[h], _NT) * decay[h], 0.0) for h in hs]
    t_mat = [eye - a for a in a_mat]
    pw = a_mat
    for _ in range(ch.bit_length() - 2):
        pw = [_dot3(p, p) for p in pw]
        t_mat = [t_mat[h] + _dot3(t_mat[h], pw[h]) for h in hs]
    u = [_mm(t_mat[h], v[h] * b[h]) for h in hs]
    wm = [_mm(t_mat[h], kb[h] * eg[h]) for h in hs]
    qk = [jnp.where(lower, _mm(q[h], k[h], _NT) * decay[h], 0.0) for h in hs]
    state = [s_scr[h] for h in hs]
    v_new = [u[h] - _mm(wm[h], state[h]) for h in hs]
    o = [_mm(q[h] * eg[h], state[h]) + _mm(qk[h], v_new[h]) for h in hs]
    kd = [k[h] * jnp.exp(gl[h] - gc[h]) for h in hs]
    new_state = [state[h] * jnp.exp(gl[h]) + _mm(kd[h], v_new[h], _TN) for h in hs]
    for h in hs:
        s_scr[h] = new_state[h]
    for h in hs:
        o_h = _rms_rows(o[h]) * onorm_ref[...] * _silu(z_ref[:, sl[h]])
        o_ref[:, sl[h]] = o_h.astype(o_ref.dtype)


def _gdn(p2, p1, conv_w, a_log, dt_bias, o_norm, *, pad):
    tp = p2.shape[0]
    bw = B_HEADS * B_HEAD_DIM
    alog = jnp.zeros((1, LANES), F32).at[0, SM_A:SM_A + B_HEADS].set(a_log.astype(F32))
    dtb = jnp.zeros((1, LANES), F32).at[0, SM_A:SM_A + B_HEADS].set(dt_bias.astype(F32))
    ch = GDN_CHUNK
    return pl.pallas_call(
        functools.partial(_gdn_kernel, pad=pad),
        name="gdn",
        out_shape=jax.ShapeDtypeStruct((tp, bw), BF16),
        grid=(tp // ch,),
        in_specs=[
            pl.BlockSpec((ch, 3 * bw), lambda c: (c, 0)),
            pl.BlockSpec((ch, bw), lambda c: (c, 3)),
            pl.BlockSpec((ch, LANES), lambda c: (c, 0)),
            pl.BlockSpec((CONV_K, 3 * bw), lambda c: (0, 0)),
            pl.BlockSpec((1, LANES), lambda c: (0, 0)),
            pl.BlockSpec((1, LANES), lambda c: (0, 0)),
            pl.BlockSpec((1, B_HEAD_DIM), lambda c: (0, 0)),
        ],
        out_specs=pl.BlockSpec((ch, bw), lambda c: (c, 0)),
        scratch_shapes=[pltpu.VMEM((SUBLANES + ch, 3 * bw), F32),
                        pltpu.VMEM((B_HEADS, B_HEAD_DIM, B_HEAD_DIM), F32)],
        compiler_params=_params("arbitrary"),
    )(p2, p2, p1, conv_w.astype(F32), alog, dtb, o_norm.reshape(1, B_HEAD_DIM).astype(F32))


def _ret_kernel(q_ref, k_ref, v_ref, g_ref, cos_ref, sin_ref, dmask_ref, qdec_ref, kdec_ref,
                cdec_ref, o_ref, r_scr):
    @pl.when(pl.program_id(1) == 0)
    def _():
        r_scr[...] = jnp.zeros_like(r_scr)

    cos, sin = cos_ref[...], sin_ref[...]
    even = lax.broadcasted_iota(jnp.int32, cos.shape, 1) % 2 == 0

    def rot(x):
        partner = jnp.where(even, pltpu.roll(x, C_QK_DIM - 1, axis=1), pltpu.roll(x, 1, axis=1))
        return x * cos + partner * sin

    q = rot(q_ref[...].astype(F32))
    k = rot(k_ref[...].astype(F32)) * C_QK_DIM ** -0.5
    v = v_ref[...]
    s = lax.dot_general(q.astype(BF16), k.astype(BF16), _NT, preferred_element_type=F32)
    s = s * dmask_ref[...]
    r = r_scr[...]
    o = jnp.dot(s.astype(BF16), v, preferred_element_type=F32)
    o = o + jnp.dot((q * qdec_ref[...]).astype(BF16), r.astype(BF16), preferred_element_type=F32)
    kd = (k * kdec_ref[...]).astype(BF16)
    r_scr[...] = r * cdec_ref[...] + lax.dot_general(kd, v, _TN, preferred_element_type=F32)
    gate = g_ref[...].astype(F32)
    o_ref[...] = (_silu(gate) * _rms_rows(o)).astype(o_ref.dtype)


def _retention(r_all, cos, sin):
    tp = r_all.shape[0]
    ch = RET_CHUNK
    v0 = 2 * C_HEADS * C_QK_DIM // C_V_DIM
    log_gamma = jnp.log(1.0 - 2.0 ** (-5.0 - jnp.arange(C_HEADS, dtype=F32)))
    pos = jnp.arange(ch, dtype=F32)
    rel = pos[:, None] - pos[None, :]
    dmask = jnp.where(rel >= 0, jnp.exp(jnp.maximum(rel, 0.0)[None] * log_gamma[:, None, None]), 0.0)
    qdec = jnp.exp((pos + 1.0)[None, :] * log_gamma[:, None])[:, :, None]
    kdec = jnp.exp((ch - 1.0 - pos)[None, :] * log_gamma[:, None])[:, :, None]
    cdec = jnp.broadcast_to(jnp.exp(ch * log_gamma)[:, None, None], (C_HEADS, 1, C_V_DIM))
    return pl.pallas_call(
        _ret_kernel,
        name="retention",
        out_shape=jax.ShapeDtypeStruct((tp, C_HEADS * C_V_DIM), BF16),
        grid=(C_HEADS, tp // ch),
        in_specs=[
            pl.BlockSpec((ch, C_QK_DIM), lambda h, c: (c, h)),
            pl.BlockSpec((ch, C_QK_DIM), lambda h, c: (c, C_HEADS + h)),
            pl.BlockSpec((ch, C_V_DIM), lambda h, c: (c, v0 + h)),
            pl.BlockSpec((ch, C_V_DIM), lambda h, c: (c, v0 + C_HEADS + h)),
            pl.BlockSpec((ch, C_QK_DIM), lambda h, c: (c, 0)),
            pl.BlockSpec((ch, C_QK_DIM), lambda h, c: (c, 0)),
            pl.BlockSpec((None, ch, ch), lambda h, c: (h, 0, 0)),
            pl.BlockSpec((None, ch, 1), lambda h, c: (h, 0, 0)),
            pl.BlockSpec((None, ch, 1), lambda h, c: (h, 0, 0)),
            pl.BlockSpec((None, 1, C_V_DIM), lambda h, c: (h, 0, 0)),
        ],
        out_specs=pl.BlockSpec((ch, C_V_DIM), lambda h, c: (c, h)),
        scratch_shapes=[pltpu.VMEM((C_QK_DIM, C_V_DIM), F32)],
        compiler_params=_params("arbitrary", "arbitrary"),
    )(r_all, r_all, r_all, r_all, cos, sin, dmask, qdec, kdec, cdec)


def _hybrid_mixer(u_hi, u_lo, w_in, q_norm, w_uq, w_qidx, kv_norm, w_ukv, kidx_norm,
                  conv_w, a_log, dt_bias, o_norm, w_out, *, top_k, pad, tm, tm_wide):
    tp = u_hi.shape[0]
    o = [0]
    for s in (A_Q_RANK, A_KV_RANK, IDX_DIM, IDX_HEADS, 3 * B_HEADS * B_HEAD_DIM,
              B_HEADS * B_HEAD_DIM, B_HEADS, B_HEADS):
        o.append(o[-1] + s)
    col = lambda i: w_in[:, o[i]:o[i + 1]]
    n_small = IDX_DIM + IDX_HEADS + 2 * B_HEADS
    w1 = jnp.concatenate([col(0), col(2), col(3), col(6), col(7),
                          jnp.zeros((w_in.shape[0], LANES - n_small), w_in.dtype)], axis=1)
    tn2 = 2 * MXU_DIM
    n2 = o[6] - o[4] + A_KV_RANK
    w2 = jnp.concatenate([col(4), col(5), col(1),
                          jnp.zeros((w_in.shape[0], -n2 % tn2), w_in.dtype)], axis=1).astype(BF16)
    p1t, small = _matmul("hy_in_x3", (u_hi, u_lo), _hilo(w1), F32, tm=tm, tn=w1.shape[1],
                         epilogue="transposed")
    p2 = _matmul("hy_in", (u_hi,), (w2,), F32, tm=tm_wide, tn=tn2)

    cqt_hi, cqt_lo = _rms_t(p1t, q_norm, width=A_Q_RANK, row_block=0, tc=tm, lo=True)
    up_rows = w_uq.shape[1]
    qi3t = _matmul("dsa_qidx", _hilo(w_qidx.T), (cqt_hi, cqt_lo), BF16, tm=up_rows, tn=tm,
                   epilogue="split3")
    q_scale = A_HEAD_DIM ** -0.5 * math.log2(math.e)
    qt = _matmul("dsa_q", ((w_uq.T * q_scale).astype(BF16),), (cqt_hi,), BF16, tm=up_rows, tn=tm)
    k3 = _kidx(small, kidx_norm, tm=tm)
    wt = p1t[A_Q_RANK + SM_WIDX:A_Q_RANK + SM_BETA]
    ckv_block = (o[6] - o[4]) // A_KV_RANK
    kvn, kvnt = _rms_both(p2, kv_norm, width=A_KV_RANK, col_block=ckv_block, tm=tm)
    w_kv = w_ukv.reshape(A_KV_RANK, A_HEADS, 2, A_HEAD_DIM)
    w_k = w_kv[:, :, 0].reshape(A_KV_RANK, A_HEADS * A_HEAD_DIM).astype(BF16)
    w_vt = w_kv[:, :, 1].reshape(A_KV_RANK, A_HEADS * A_HEAD_DIM).T.astype(BF16)
    k = _matmul("dsa_k", (kvn,), (w_k,), BF16, tm=tm, tn=up_rows)
    vt = _matmul("dsa_vt", (w_vt,), (kvnt,), BF16, tm=up_rows, tn=tm)
    o_a = _dsa(qt, qi3t, wt, k3, k, vt, top_k=top_k, pad=pad, tq=MXU_DIM, tk=tm)

    o_b = _gdn(p2, small, conv_w, a_log, dt_bias, o_norm, pad=pad)

    return _matmul_w32("hy_out", [o_a, o_b], w_out[None], 0, BF16, tm=tm_wide, tn=2 * MXU_DIM)


def _retention_mixer(u, w_in, w_out, *, pad, tm, tm_wide):
    tp = u.shape[0]
    half = C_QK_DIM // 2
    r_all = _matmul_w32("ret_in", [u], w_in[None], 0, BF16, tm=tm_wide, tn=2 * MXU_DIM)
    inv = 1.0 / (ROT_BASE ** jnp.linspace(0.0, 1.0, half, dtype=F32))
    posn = (jnp.arange(tp) - pad).astype(F32)
    ang = posn[:, None] * jnp.repeat(inv, 2)[None, :]
    sign = jnp.tile(jnp.array([-1.0, 1.0], F32), half)
    og = _retention(r_all, jnp.cos(ang), jnp.sin(ang) * sign[None, :])
    return _matmul_w32("ret_out", [og], w_out[None], 0, BF16, tm=tm_wide, tn=2 * MXU_DIM,
                       vmem=BIG_TILE_VMEM_LIMIT)


def _ffn(u, w_gate, w_up, w_down, layer, *, tm, tm_wide):
    act = _ffn_act(u, w_gate, w_up, layer, tm=tm_wide, tn=2 * MXU_DIM)
    return _matmul_w32("ffn_down", [act], w_down, layer, BF16, tm=tm, tn=2 * MXU_DIM,
                       vmem=BIG_TILE_VMEM_LIMIT)


def kernel(x, meta_tokens, mix_norm_pre, mix_norm_post, ffn_norm_pre, ffn_norm_post, hy_w_in, dsa_q_norm, dsa_w_uq, dsa_w_qidx, dsa_kv_norm, dsa_w_ukv, dsa_kidx_norm, gdn_conv_w, gdn_a_log, gdn_dt_bias, gdn_o_norm, hy_w_out, ret_w_in, ret_w_out, ffn_w_gate, ffn_w_up, ffn_w_down):
    b, seq, d = x.shape
    assert b == 1 and mix_norm_pre.shape[0] == 2
    pad = (-N_META) % ROW_ALIGN
    tp = pad + N_META + seq
    tm = ROW_TILE
    assert tp % tm == 0
    tm_wide = WIDE_ROW_TILE if tp % WIDE_ROW_TILE == 0 else tm
    top_k = min(TOPK_MAX, seq // TOPK_FRAC)
    h, u_hi, u_lo = _embed_norm(x[0], meta_tokens, mix_norm_pre[0], pad=pad)
    mix = _hybrid_mixer(u_hi, u_lo, hy_w_in[0], dsa_q_norm[0], dsa_w_uq[0], dsa_w_qidx[0],
                        dsa_kv_norm[0], dsa_w_ukv[0], dsa_kidx_norm[0], gdn_conv_w[0],
                        gdn_a_log[0], gdn_dt_bias[0], gdn_o_norm[0], hy_w_out[0],
                        top_k=top_k, pad=pad, tm=tm, tm_wide=tm_wide)
    h, u = _resid_norm(h, mix, mix_norm_post[0], ffn_norm_pre[0], tm=tm)
    f = _ffn(u, ffn_w_gate, ffn_w_up, ffn_w_down, 0, tm=tm, tm_wide=tm_wide)
    h, u = _resid_norm(h, f, ffn_norm_post[0], mix_norm_pre[1], tm=tm)
    mix = _retention_mixer(u, ret_w_in[0], ret_w_out[0], pad=pad, tm=tm, tm_wide=tm_wide)
    h, u = _resid_norm(h, mix, mix_norm_post[1], ffn_norm_pre[1], tm=tm)
    f = _ffn(u, ffn_w_gate, ffn_w_up, ffn_w_down, 1, tm=tm, tm_wide=tm_wide)
    out = _resid_norm(h, f, ffn_norm_post[1], None, tm=ROW_ALIGN, row0=pad + N_META,
                      out_dtype=x.dtype)
    return out[None]
```

```python
import functools
import math

import jax
import jax.numpy as jnp
from jax import lax
from jax.experimental import pallas as pl
from jax.experimental.pallas import tpu as pltpu

F32 = jnp.float32
BF16 = jnp.bfloat16

N_META = 16
NORM_EPS = 1e-6

A_HEADS = 8
A_HEAD_DIM = 128
A_Q_RANK = 512
A_KV_RANK = 256
IDX_HEADS = 16
IDX_DIM = 64
TOPK_MAX = 256
TOPK_FRAC = 4

B_HEADS = 8
B_HEAD_DIM = 128
CONV_K = 4
GDN_CHUNK = 128
DSA_HEADS_PER_STEP = 2
DSA_VMEM_LIMIT = 56 * 1024 * 1024
BIG_TILE_VMEM_LIMIT = 56 * 1024 * 1024
DSA_SEGMENTS = 3
DSA_SUB_BLOCK = 64

C_HEADS = 8
C_QK_DIM = 256
C_V_DIM = 512
ROT_BASE = 10000.0
RET_CHUNK = 384

LANES = 128
SUBLANES = 8
MXU_DIM = 256
ROW_ALIGN = MXU_DIM
ROW_TILE = 3 * MXU_DIM
WIDE_ROW_TILE = 11 * LANES
VMEM_LIMIT = 48 * 1024 * 1024

SM_WIDX = IDX_DIM
SM_BETA = SM_WIDX + IDX_HEADS
SM_A = SM_BETA + B_HEADS

INT_MIN = -2 ** 31
KEY_FLT_MAX = 0x7F7FFFFF
KEY_MIN_NORMAL = 0x00800000
KEY_FLT_LOWEST = (0xFF7FFFFF ^ 0x7FFFFFFF) - 2 ** 32


def _params(*sem, vmem=VMEM_LIMIT):
    return pltpu.CompilerParams(dimension_semantics=sem, vmem_limit_bytes=vmem)


def _hilo(x):
    hi = x.astype(BF16)
    return hi, (x - hi.astype(F32)).astype(BF16)


def _silu(x):
    return x * jax.nn.sigmoid(x)


def _mm_kernel(*refs, n_parts, epilogue):
    a, b, o = refs[:n_parts], refs[n_parts:2 * n_parts], refs[2 * n_parts:]
    a_hi, b_hi = a[0][...], b[0][...]
    acc = jnp.dot(a_hi, b_hi, preferred_element_type=F32)
    if n_parts == 2:
        acc += jnp.dot(a_hi, b[1][...], preferred_element_type=F32)
        acc += jnp.dot(a[1][...], b_hi, preferred_element_type=F32)
    if epilogue is None:
        o[0][...] = acc.astype(o[0].dtype)
    elif epilogue == "transposed":
        o[0][...] = acc.T
        o[1][...] = acc[:, acc.shape[1] - LANES:]
    elif epilogue == "split3":
        for g in range(acc.shape[0] // IDX_DIM):
            hi, lo = _hilo(acc[g * IDX_DIM:(g + 1) * IDX_DIM, :])
            base = g * MXU_DIM
            o[0][base:base + IDX_DIM, :] = hi
            o[0][base + IDX_DIM:base + 2 * IDX_DIM, :] = lo
            o[0][base + 2 * IDX_DIM:base + 3 * IDX_DIM, :] = hi
            o[0][base + 3 * IDX_DIM:base + MXU_DIM, :] = jnp.zeros_like(hi)


def _matmul(name, a_parts, b_parts, out_dtype, *, tm, tn, epilogue=None):
    m, kdim = a_parts[0].shape
    n = b_parts[0].shape[1]
    assert m % tm == 0 and n % tn == 0, (m, n, tm, tn)
    assert len(a_parts) == len(b_parts)
    if epilogue is None:
        out_shape = [jax.ShapeDtypeStruct((m, n), out_dtype)]
        out_specs = [pl.BlockSpec((tm, tn), lambda i, j: (i, j))]
    elif epilogue == "transposed":
        assert tn == n
        out_shape = [jax.ShapeDtypeStruct((n, m), out_dtype),
                     jax.ShapeDtypeStruct((m, LANES), out_dtype)]
        out_specs = [pl.BlockSpec((n, tm), lambda i, j: (0, i)),
                     pl.BlockSpec((tm, LANES), lambda i, j: (i, 0))]
    else:
        scale = MXU_DIM // IDX_DIM
        out_shape = [jax.ShapeDtypeStruct((m * scale, n), out_dtype)]
        out_specs = [pl.BlockSpec((tm * scale, tn), lambda i, j: (i, j))]
    out = pl.pallas_call(
        functools.partial(_mm_kernel, n_parts=len(a_parts), epilogue=epilogue),
        name=name,
        out_shape=out_shape,
        grid=(m // tm, n // tn),
        in_specs=[pl.BlockSpec((tm, kdim), lambda i, j: (i, 0))] * len(a_parts)
        + [pl.BlockSpec((kdim, tn), lambda i, j: (0, j))] * len(b_parts),
        out_specs=out_specs,
        compiler_params=_params("parallel", "parallel"),
    )(*a_parts, *b_parts)
    return out[0] if len(out) == 1 else out


def _mm_w32_kernel(*refs, n_a):
    a, w_ref, o_ref = refs[:n_a], refs[n_a], refs[n_a + 1]
    w = w_ref[...].astype(BF16)
    acc, k0 = None, 0
    for a_ref in a:
        k1 = k0 + a_ref.shape[1]
        part = jnp.dot(a_ref[...], w[k0:k1, :], preferred_element_type=F32)
        acc = part if acc is None else acc + part
        k0 = k1
    o_ref[...] = acc.astype(o_ref.dtype)


def _matmul_w32(name, a_list, w, layer, out_dtype, *, tm, tn, col0=0, n=None, vmem=VMEM_LIMIT):
    m = a_list[0].shape[0]
    kdim = sum(a.shape[1] for a in a_list)
    n = w.shape[2] - col0 if n is None else n
    assert w.shape[1] == kdim and m % tm == 0 and n % tn == 0 and col0 % tn == 0
    jb = col0 // tn
    return pl.pallas_call(
        functools.partial(_mm_w32_kernel, n_a=len(a_list)),
        name=name,
        out_shape=jax.ShapeDtypeStruct((m, n), out_dtype),
        grid=(m // tm, n // tn),
        in_specs=[pl.BlockSpec((tm, a.shape[1]), lambda i, j: (i, 0)) for a in a_list]
        + [pl.BlockSpec((None, kdim, tn), lambda i, j: (layer, 0, j + jb))],
        out_specs=pl.BlockSpec((tm, tn), lambda i, j: (i, j)),
        compiler_params=_params("parallel", "parallel", vmem=vmem),
    )(*a_list, w)


def _ffn_act_kernel(x_ref, wg_ref, wu_ref, o_ref):
    x = x_ref[...]
    g = jnp.dot(x, wg_ref[...].astype(BF16), preferred_element_type=F32)
    u = jnp.dot(x, wu_ref[...].astype(BF16), preferred_element_type=F32)
    o_ref[...] = (_silu(g) * u).astype(o_ref.dtype)


def _ffn_act(x, wg, wu, layer, *, tm, tn):
    m, kdim = x.shape
    n = wg.shape[2]
    assert m % tm == 0 and n % tn == 0
    w_spec = pl.BlockSpec((None, kdim, tn), lambda i, j: (layer, 0, j))
    return pl.pallas_call(
        _ffn_act_kernel,
        name="ffn_act",
        out_shape=jax.ShapeDtypeStruct((m, n), BF16),
        grid=(m // tm, n // tn),
        in_specs=[pl.BlockSpec((tm, kdim), lambda i, j: (i, 0)), w_spec, w_spec],
        out_specs=pl.BlockSpec((tm, tn), lambda i, j: (i, j)),
        compiler_params=_params("parallel", "parallel"),
    )(x, wg, wu)


def _rms_rows(x):
    return x * lax.rsqrt(jnp.mean(x * x, axis=-1, keepdims=True) + NORM_EPS)


def _embed_kernel(x_ref, head_ref, g_ref, h_ref, hi_ref, lo_ref):
    h = jnp.where(pl.program_id(0) == 0, head_ref[...], x_ref[...].astype(F32))
    h_ref[...] = h
    y = _rms_rows(h) * g_ref[...]
    hi = y.astype(BF16)
    hi_ref[...] = hi
    lo_ref[...] = (y - hi.astype(F32)).astype(BF16)


def _embed_norm(x, meta, gain, *, pad):
    seq, d = x.shape
    tb = pad + meta.shape[0]
    assert seq % tb == 0
    head = jnp.concatenate([jnp.zeros((pad, d), F32), meta.astype(F32)], axis=0)
    tp = tb + seq
    row = pl.BlockSpec((tb, d), lambda i: (i, 0))
    return pl.pallas_call(
        _embed_kernel,
        name="embed_norm",
        out_shape=[jax.ShapeDtypeStruct((tp, d), F32)] + [jax.ShapeDtypeStruct((tp, d), BF16)] * 2,
        grid=(tp // tb,),
        in_specs=[pl.BlockSpec((tb, d), lambda i: (jnp.maximum(i - 1, 0), 0)),
                  pl.BlockSpec((tb, d), lambda i: (0, 0)),
                  pl.BlockSpec((1, d), lambda i: (0, 0))],
        out_specs=[row, row, row],
        compiler_params=_params("parallel"),
    )(x, head, gain.reshape(1, d).astype(F32))


def _rms_both_kernel(x_ref, g_ref, o_ref, ot_ref):
    y = _rms_rows(x_ref[...]) * g_ref[...]
    o_ref[...] = y.astype(BF16)
    ot_ref[...] = y.T.astype(BF16)


def _rms_both(x, gain, *, width, col_block, tm):
    m = x.shape[0]
    return pl.pallas_call(
        _rms_both_kernel,
        name="rms_both",
        out_shape=[jax.ShapeDtypeStruct((m, width), BF16), jax.ShapeDtypeStruct((width, m), BF16)],
        grid=(m // tm,),
        in_specs=[pl.BlockSpec((tm, width), lambda i: (i, col_block)),
                  pl.BlockSpec((1, width), lambda i: (0, 0))],
        out_specs=[pl.BlockSpec((tm, width), lambda i: (i, 0)),
                   pl.BlockSpec((width, tm), lambda i: (0, i))],
        compiler_params=_params("parallel"),
    )(x, gain.reshape(1, width).astype(F32))


def _rms_t_kernel(x_ref, g_ref, *o_refs):
    x = x_ref[...]
    y = x * lax.rsqrt(jnp.mean(x * x, axis=0, keepdims=True) + NORM_EPS) * g_ref[...]
    hi = y.astype(BF16)
    o_refs[0][...] = hi
    if len(o_refs) == 2:
        o_refs[1][...] = (y - hi.astype(F32)).astype(BF16)


def _rms_t(xt, gain, *, width, row_block, tc, lo=False):
    t = xt.shape[1]
    n_out = 2 if lo else 1
    out = pl.pallas_call(
        _rms_t_kernel,
        name="rms_t",
        out_shape=[jax.ShapeDtypeStruct((width, t), BF16)] * n_out,
        grid=(t // tc,),
        in_specs=[pl.BlockSpec((width, tc), lambda i: (row_block, i)),
                  pl.BlockSpec((width, 1), lambda i: (0, 0))],
        out_specs=[pl.BlockSpec((width, tc), lambda i: (0, i))] * n_out,
        compiler_params=_params("parallel"),
    )(xt, gain.reshape(width, 1).astype(F32))
    return out if lo else out[0]


def _resid_kernel(h_ref, u_ref, gp_ref, *rest, with_next):
    hn = h_ref[...] + _rms_rows(u_ref[...].astype(F32)) * gp_ref[...]
    if with_next:
        gn_ref, hn_ref, un_ref = rest
        un_ref[...] = (_rms_rows(hn) * gn_ref[...]).astype(BF16)
    else:
        (hn_ref,) = rest
    hn_ref[...] = hn.astype(hn_ref.dtype)


def _resid_norm(h, u, g_post, g_next, *, tm, row0=0, out_dtype=F32):
    m, d = h.shape
    assert row0 % tm == 0 and m % tm == 0
    b0 = row0 // tm
    row_in = pl.BlockSpec((tm, d), lambda i: (i + b0, 0))
    row_out = pl.BlockSpec((tm, d), lambda i: (i, 0))
    vec = pl.BlockSpec((1, d), lambda i: (0, 0))
    with_next = g_next is not None
    gains = [g_post.reshape(1, d).astype(F32)]
    out_shape = [jax.ShapeDtypeStruct((m - row0, d), out_dtype)]
    if with_next:
        gains.append(g_next.reshape(1, d).astype(F32))
        out_shape.append(jax.ShapeDtypeStruct((m - row0, d), BF16))
    out = pl.pallas_call(
        functools.partial(_resid_kernel, with_next=with_next),
        name="resid_norm",
        out_shape=out_shape,
        grid=((m - row0) // tm,),
        in_specs=[row_in, row_in] + [vec] * len(gains),
        out_specs=[row_out] * len(out_shape),
        compiler_params=_params("parallel"),
    )(h, u, *gains)
    return out if with_next else out[0]


def _kidx_kernel(sm_ref, g_ref, o_ref):
    sm = sm_ref[...]
    lane = lax.broadcasted_iota(jnp.int32, sm.shape, 1)
    x = jnp.where(lane < IDX_DIM, sm, 0.0)
    ms = jnp.sum(x * x, axis=-1, keepdims=True) * (1.0 / IDX_DIM)
    y = x * lax.rsqrt(ms + NORM_EPS) * g_ref[...]
    hi = y.astype(BF16)
    lo = (y - hi.astype(F32)).astype(BF16)
    y2 = y + pltpu.roll(y, IDX_DIM, axis=1)
    o_ref[:, 0:LANES] = y2.astype(BF16)
    o_ref[:, LANES:2 * LANES] = lo


def _kidx(p1, gain, *, tm):
    m = p1.shape[0]
    g = jnp.zeros((1, LANES), F32).at[0, :IDX_DIM].set(gain.astype(F32))
    return pl.pallas_call(
        _kidx_kernel,
        name="dsa_kidx",
        out_shape=jax.ShapeDtypeStruct((m, 2 * LANES), BF16),
        grid=(m // tm,),
        in_specs=[pl.BlockSpec((tm, LANES), lambda i: (i, 0)),
                  pl.BlockSpec((1, LANES), lambda i: (0, 0))],
        out_specs=pl.BlockSpec((tm, 2 * LANES), lambda i: (i, 0)),
        compiler_params=_params("parallel"),
    )(p1, g)


def _dsa_kernel(qt_ref, qi3t_ref, wt_ref, k3_ref, k_ref, vt_ref, o_ref, sc_scr, sc16_scr, acc_scr,
                sa_scr, sb_scr, pa_scr, pb_scr, *, tq, tk, sb, qb0, top_k, pad, hps, w_scale,
                pos_bits):
    qi = pl.program_id(0)
    hd = pl.program_id(1)
    q0 = (qi + qb0) * tq
    nkt = (q0 + tq + tk - 1) // tk

    krel = lax.broadcasted_iota(jnp.int32, (tk, tq), 0)

    @pl.when(hd == 0)
    def _select():
        qpos = q0 + lax.broadcasted_iota(jnp.int32, (tk, tq), 1)

        def score_tile(kt, carry):
            ks = pl.multiple_of(kt * tk, tk)
            k3 = k3_ref[pl.ds(ks, tk), :]
            acc = jnp.zeros((tk, tq), F32)
            for ih in range(IDX_HEADS):
                d = jnp.dot(k3, qi3t_ref[ih * MXU_DIM:(ih + 1) * MXU_DIM, :],
                            preferred_element_type=F32)
                acc = acc + (wt_ref[ih:ih + 1, :] * w_scale) * jnp.maximum(d, 0.0)
            kpos = ks + krel
            ok = (kpos <= qpos) & (kpos >= pad)
            masked = jnp.where(ok, acc, -jnp.inf)
            sc_scr[pl.ds(ks, tk), :] = masked
            sc16_scr[pl.ds(ks, tk), :] = masked.astype(BF16)
            return carry

        lax.fori_loop(0, nkt, score_tile, 0)

        def count(pred):
            def body(kt, acc):
                ks = pl.multiple_of(kt * tk, tk)
                hit = pred(sc_scr[pl.ds(ks, tk), :], ks).astype(jnp.int32)
                return acc + jnp.sum(hit.reshape(tk // SUBLANES, SUBLANES, tq), axis=0)

            acc = lax.fori_loop(0, nkt, body, jnp.zeros((SUBLANES, tq), jnp.int32))
            return jnp.sum(acc, axis=0, keepdims=True)

        def key_to_f32(t):
            t = jnp.clip(t, jnp.int32(KEY_FLT_LOWEST), jnp.int32(KEY_FLT_MAX))
            return pltpu.bitcast(jnp.where(t >= 0, t, t ^ jnp.int32(0x7FFFFFFF)), F32)

        def count16(cand_b):
            rows = 2 * SUBLANES
            one, nil = jnp.ones((), BF16), jnp.zeros((), BF16)
            cand_b = jnp.maximum(cand_b, jnp.finfo(BF16).min)

            def body(kt, acc):
                ks = pl.multiple_of(kt * tk, tk)
                hit = jnp.where(sc16_scr[pl.ds(ks, tk), :] >= cand_b, one, nil)
                slabs = [hit[r:r + rows, :] for r in range(0, tk, rows)]
                while len(slabs) > 1:
                    pairs = [a + b for a, b in zip(slabs[0::2], slabs[1::2])]
                    slabs = pairs + slabs[2 * len(pairs):]
                return acc + slabs[0].astype(F32)

            acc = lax.fori_loop(0, nkt, body, jnp.zeros((rows, tq), F32))
            return jnp.sum(acc, axis=0, keepdims=True).astype(jnp.int32)

        zero = jnp.zeros((1, tq), jnp.int32)
        accept = lambda n, cand: (n >= top_k) & (cand <= jnp.int32(KEY_FLT_MAX))
        n0 = count16(jnp.zeros((1, tq), BF16))
        t_hi = jnp.where(n0 >= top_k, zero, jnp.int32(INT_MIN))

        def hi_body(i, t):
            cand = t + lax.shift_left(jnp.int32(1), 30 - i)
            n = count16(key_to_f32(cand).astype(BF16))
            return jnp.where(accept(n, cand), cand, t)

        t_hi = lax.fori_loop(0, 15, hi_body, t_hi)

        base = t_hi - jnp.int32(1 << 15)

        def lo_body(i, state):
            off, n_at = state
            bit = lax.shift_left(jnp.int32(1), 16 - i)
            cand = base + (off | bit)
            cand_f = key_to_f32(cand)
            n = count(lambda s, ks: s >= cand_f)
            ok = accept(n, cand)
            return jnp.where(ok, off | bit, off), jnp.where(ok, n, n_at)

        off, n_at = lax.fori_loop(0, 17, lo_body, (zero, zero + (top_k + 1)))
        few = t_hi == jnp.int32(INT_MIN)
        t = jnp.where(few, t_hi, base + off)
        thr = key_to_f32(t)

        n_ge = jnp.where(few, zero, n_at)

        @pl.when(jnp.max(n_ge) > top_k)
        def _ties():
            flushed = (t >= jnp.int32(-KEY_MIN_NORMAL)) & (t < jnp.int32(KEY_MIN_NORMAL))
            thr_next = key_to_f32(jnp.where(flushed, jnp.int32(KEY_MIN_NORMAL), t + 1))
            tie = lambda s: (s >= thr) & jnp.logical_not(s >= thr_next)
            want = top_k - count(lambda s, ks: s >= thr_next)

            def pos_body(i, cut):
                cand = cut + lax.shift_left(jnp.int32(1), pos_bits - 1 - i)
                n = count(lambda s, ks: tie(s) & (ks + krel < cand))
                return jnp.where(n < want, cand, cut)

            cut = lax.fori_loop(0, pos_bits, pos_body, zero)

            def drop_tile(kt, carry):
                ks = pl.multiple_of(kt * tk, tk)
                s = sc_scr[pl.ds(ks, tk), :]
                sc_scr[pl.ds(ks, tk), :] = jnp.where(tie(s) & (ks + krel > cut), -jnp.inf, s)
                return carry

            lax.fori_loop(0, nkt, drop_tile, 0)

        def bias_tile(kt, carry):
            ks = pl.multiple_of(kt * tk, tk)
            sc_scr[pl.ds(ks, tk), :] = jnp.where(sc_scr[pl.ds(ks, tk), :] >= thr, 0.0, -jnp.inf)
            return carry

        lax.fori_loop(0, nkt, bias_tile, 0)

    hd_sl = [slice(j * A_HEAD_DIM, (j + 1) * A_HEAD_DIM) for j in range(hps)]
    q = [qt_ref[sl, :] for sl in hd_sl]

    def logits_into(kt, s_ref):
        ks = pl.multiple_of(kt * tk, tk)
        for j in range(hps):
            s_ref[j] = jnp.dot(k_ref[pl.ds(ks, tk), hd_sl[j]], q[j], preferred_element_type=F32)

    fold = lambda x: x.reshape(sb // SUBLANES, SUBLANES, tq)

    def att_tile(kt, kt_next, bias_row, s_cur, s_next, p_ref, carry):
        if s_next is not None:
            logits_into(kt_next, s_next)
        ks = pl.multiple_of(kt * tk, tk)

        def masked(j, i):
            r = i * sb
            return s_cur[j, r:r + sb, :] + sc_scr[pl.ds(bias_row + r, sb), :]

        out, alphas = [], []
        for j in range(hps):
            m, l = carry[j]
            mx = jnp.full((SUBLANES, tq), -jnp.inf, F32)
            for i in range(tk // sb):
                mx = jnp.maximum(mx, jnp.max(fold(masked(j, i)), axis=0))
            m_new = jnp.maximum(m, jnp.max(mx, axis=0, keepdims=True))
            m_safe = jnp.where(m_new == -jnp.inf, 0.0, m_new)
            for i in range(tk // sb):
                p = jnp.exp2(masked(j, i) - m_safe)
                p_ref[j, i * sb:(i + 1) * sb, :] = p.astype(BF16)
            alphas.append(jnp.exp2(m - m_safe))
            out.append(m_new)
        ones = jnp.ones((2 * SUBLANES, tk), BF16)
        pv = [jnp.dot(jnp.concatenate([vt_ref[hd_sl[j], pl.ds(ks, tk)], ones], axis=0), p_ref[j],
                      preferred_element_type=F32) for j in range(hps)]
        for j in range(hps):
            acc_scr[j] = alphas[j] * acc_scr[j] + pv[j][:A_HEAD_DIM, :]
        return tuple((out[j], alphas[j] * carry[j][1] + pv[j][A_HEAD_DIM:A_HEAD_DIM + 1, :])
                     for j in range(hps))

    def att_pair(i, carry):
        kt0 = 2 * i
        kt2 = jnp.minimum(kt0 + 2, nkt - 1)
        row = lambda kt: pl.multiple_of(kt * tk, tk)
        carry = att_tile(kt0, kt0 + 1, row(kt0), sa_scr, sb_scr, pa_scr, carry)
        return att_tile(kt0 + 1, kt2, row(kt0 + 1), sb_scr, sa_scr, pb_scr, carry)

    acc_scr[...] = jnp.zeros_like(acc_scr)
    logits_into(0, sa_scr)
    init = (jnp.full((1, tq), -jnp.inf, F32), jnp.zeros((1, tq), F32))
    res = lax.fori_loop(0, nkt // 2, att_pair, (init,) * hps)
    last = nkt - 1
    res = lax.cond(
        nkt % 2 == 1,
        lambda c: att_tile(last, None, pl.multiple_of(last * tk, tk), sa_scr, None, pa_scr, c),
        lambda c: c, res)
    for j in range(hps):
        l = res[j][1]
        o_ref[:, hd_sl[j]] = jnp.where(l > 0.0, acc_scr[j] / l, 0.0).T.astype(o_ref.dtype)


def _dsa(qt, qi3t, wt, k3, k, vt, *, top_k, pad, tq, tk):
    tp = k.shape[0]
    assert tp % tq == 0 and tp % tk == 0 and tk >= top_k
    hps = DSA_HEADS_PER_STEP
    hw = hps * A_HEAD_DIM
    n_tiles = tp // tk
    outs = []
    for seg in range(DSA_SEGMENTS):
        t0, t1 = seg * n_tiles // DSA_SEGMENTS, (seg + 1) * n_tiles // DSA_SEGMENTS
        if t1 == t0:
            continue
        qb0, nq, nk = t0 * tk // tq, (t1 - t0) * tk // tq, t1 * tk
        kern = functools.partial(
            _dsa_kernel, tq=tq, tk=tk, sb=DSA_SUB_BLOCK, qb0=qb0, top_k=top_k, pad=pad, hps=hps,
            w_scale=(IDX_HEADS * IDX_DIM) ** -0.5, pos_bits=tp.bit_length())
        outs.append(pl.pallas_call(
            kern,
            name="dsa",
            out_shape=jax.ShapeDtypeStruct((nq * tq, A_HEADS * A_HEAD_DIM), BF16),
            grid=(nq, A_HEADS // hps),
            in_specs=[
                pl.BlockSpec((hw, tq), lambda i, h, qb0=qb0: (h, i + qb0)),
                pl.BlockSpec((IDX_HEADS * MXU_DIM, tq), lambda i, h, qb0=qb0: (0, i + qb0)),
                pl.BlockSpec((IDX_HEADS, tq), lambda i, h, qb0=qb0: (0, i + qb0)),
                pl.BlockSpec((nk, MXU_DIM), lambda i, h: (0, 0)),
                pl.BlockSpec((nk, hw), lambda i, h: (0, h)),
                pl.BlockSpec((hw, nk), lambda i, h: (h, 0)),
            ],
            out_specs=pl.BlockSpec((tq, hw), lambda i, h: (i, h)),
            scratch_shapes=[pltpu.VMEM((nk, tq), F32), pltpu.VMEM((nk, tq), BF16),
                            pltpu.VMEM((hps, A_HEAD_DIM, tq), F32)]
            + [pltpu.VMEM((hps, tk, tq), F32)] * 2 + [pltpu.VMEM((hps, tk, tq), BF16)] * 2,
            compiler_params=_params("arbitrary", "arbitrary", vmem=DSA_VMEM_LIMIT),
        )(qt, qi3t, wt, k3, k, vt))
    return outs[0] if len(outs) == 1 else jnp.concatenate(outs, axis=0)


def _dot3(a, b, dims=(((1,), (0,)), ((), ()))):
    ah, al = _hilo(a)
    bh, bl = _hilo(b)
    f = lambda x, y: lax.dot_general(x, y, dims, preferred_element_type=F32)
    return f(ah, bh) + f(ah, bl) + f(al, bh)


_NN = (((1,), (0,)), ((), ()))
_NT = (((1,), (1,)), ((), ()))
_TN = (((0,), (0,)), ((), ()))


def _mm(a, b, dims=_NN):
    return lax.dot_general(a.astype(BF16), b.astype(BF16), dims, preferred_element_type=F32)


def _gdn_kernel(qkv_ref, z_ref, sm_ref, convw_ref, alog_ref, dtb_ref, onorm_ref, o_ref,
                xbuf, s_scr, *, pad):
    c = pl.program_id(0)
    ch = GDN_CHUNK
    hist = SUBLANES

    @pl.when(c == 0)
    def _():
        xbuf[0:hist, :] = jnp.zeros((hist, xbuf.shape[1]), F32)
        s_scr[...] = jnp.zeros_like(s_scr)

    x = qkv_ref[...]
    xbuf[hist:hist + ch, :] = x
    w = convw_ref[...]
    y = x * w[CONV_K - 1:CONV_K, :]
    for j in range(CONV_K - 1):
        y = y + xbuf[hist - (CONV_K - 1) + j:hist - (CONV_K - 1) + j + ch, :] * w[j:j + 1, :]
    xbuf[0:hist, :] = x[ch - hist:ch, :]
    y = _silu(y)

    sm = sm_ref[...]
    row = c * ch + lax.broadcasted_iota(jnp.int32, sm.shape, 0)
    beta = jax.nn.sigmoid(sm)
    zsm = sm + dtb_ref[...]
    softplus = jnp.maximum(zsm, 0.0) + jnp.log(1.0 + jnp.exp(-jnp.abs(zsm)))
    g = jnp.where(row >= pad, -jnp.exp(alog_ref[...]) * softplus, 0.0)

    ri = lax.broadcasted_iota(jnp.int32, (ch, ch), 0)
    ci = lax.broadcasted_iota(jnp.int32, (ch, ch), 1)
    lower = ri >= ci
    strict = ri > ci
    tri = jnp.where(lower, 1.0, 0.0).astype(BF16)
    g1 = g.astype(BF16)
    r1 = g - g1.astype(F32)
    g2 = r1.astype(BF16)
    g3 = (r1 - g2.astype(F32)).astype(BF16)
    gcum = (jnp.dot(tri, g1, preferred_element_type=F32)
            + jnp.dot(tri, g2, preferred_element_type=F32)
            + jnp.dot(tri, g3, preferred_element_type=F32))
    gcum_t = gcum.T
    eye = jnp.where(ri == ci, 1.0, 0.0).astype(F32)

    bw = B_HEADS * B_HEAD_DIM
    hs = range(B_HEADS)
    sl = [slice(h * B_HEAD_DIM, (h + 1) * B_HEAD_DIM) for h in hs]
    part = lambda j: [y[:, j * bw + h * B_HEAD_DIM:j * bw + (h + 1) * B_HEAD_DIM] for h in hs]
    l2n = lambda x: x * lax.rsqrt(jnp.sum(x * x, axis=-1, keepdims=True) + NORM_EPS)
    q = [l2n(x) * B_HEAD_DIM ** -0.5 for x in part(0)]
    k = [l2n(x) for x in part(1)]
    v = part(2)
    b = [beta[:, SM_BETA + h:SM_BETA + h + 1] for h in hs]
    gc = [gcum[:, SM_A + h:SM_A + h + 1] for h in hs]
    gr = [gcum_t[SM_A + h:SM_A + h + 1, :] for h in hs]
    gl = [x[ch - 1:ch, :] for x in gc]
    eg = [jnp.exp(x) for x in gc]
    decay = [jnp.where(lower, jnp.exp(jnp.where(lower, gc[h] - gr[h], 0.0)), 0.0) for h in hs]
    kb = [k[h] * b[h] for h in hs]
    a_mat = [jnp.where(strict, _mm(kb[h], k[h], _NT) * decay[h], 0.0) for h in hs]
    t_mat = [eye - a for a in a_mat]
    pw = a_mat
    for _ in range(ch.bit_length() - 2):
        pw = [_dot3(p, p) for p in pw]
        t_mat = [t_mat[h] + _dot3(t_mat[h], pw[h]) for h in hs]
    u = [_mm(t_mat[h], v[h] * b[h]) for h in hs]
    wm = [_mm(t_mat[h], kb[h] * eg[h]) for h in hs]
    qk = [jnp.where(lower, _mm(q[h], k[h], _NT) * decay[h], 0.0) for h in hs]
    state = [s_scr[h] for h in hs]
    v_new = [u[h] - _mm(wm[h], state[h]) for h in hs]
    o = [_mm(q[h] * eg[h], state[h]) + _mm(qk[h], v_new[h]) for h in hs]
    kd = [k[h] * jnp.exp(gl[h] - gc[h]) for h in hs]
    new_state = [state[h] * jnp.exp(gl[h]) + _mm(kd[h], v_new[h], _TN) for h in hs]
    for h in hs:
        s_scr[h] = new_state[h]
    for h in hs:
        o_h = _rms_rows(o[h]) * onorm_ref[...] * _silu(z_ref[:, sl[h]])
        o_ref[:, sl[h]] = o_h.astype(o_ref.dtype)


def _gdn(p2, p1, conv_w, a_log, dt_bias, o_norm, *, pad):
    tp = p2.shape[0]
    bw = B_HEADS * B_HEAD_DIM
    alog = jnp.zeros((1, LANES), F32).at[0, SM_A:SM_A + B_HEADS].set(a_log.astype(F32))
    dtb = jnp.zeros((1, LANES), F32).at[0, SM_A:SM_A + B_HEADS].set(dt_bias.astype(F32))
    ch = GDN_CHUNK
    return pl.pallas_call(
        functools.partial(_gdn_kernel, pad=pad),
        name="gdn",
        out_shape=jax.ShapeDtypeStruct((tp, bw), BF16),
        grid=(tp // ch,),
        in_specs=[
            pl.BlockSpec((ch, 3 * bw), lambda c: (c, 0)),
            pl.BlockSpec((ch, bw), lambda c: (c, 3)),
            pl.BlockSpec((ch, LANES), lambda c: (c, 0)),
            pl.BlockSpec((CONV_K, 3 * bw), lambda c: (0, 0)),
            pl.BlockSpec((1, LANES), lambda c: (0, 0)),
            pl.BlockSpec((1, LANES), lambda c: (0, 0)),
            pl.BlockSpec((1, B_HEAD_DIM), lambda c: (0, 0)),
        ],
        out_specs=pl.BlockSpec((ch, bw), lambda c: (c, 0)),
        scratch_shapes=[pltpu.VMEM((SUBLANES + ch, 3 * bw), F32),
                        pltpu.VMEM((B_HEADS, B_HEAD_DIM, B_HEAD_DIM), F32)],
        compiler_params=_params("arbitrary"),
    )(p2, p2, p1, conv_w.astype(F32), alog, dtb, o_norm.reshape(1, B_HEAD_DIM).astype(F32))


def _ret_kernel(q_ref, k_ref, v_ref, g_ref, cos_ref, sin_ref, dmask_ref, qdec_ref, kdec_ref,
                cdec_ref, o_ref, r_scr):
    @pl.when(pl.program_id(1) == 0)
    def _():
        r_scr[...] = jnp.zeros_like(r_scr)

    cos, sin = cos_ref[...], sin_ref[...]
    even = lax.broadcasted_iota(jnp.int32, cos.shape, 1) % 2 == 0

    def rot(x):
        partner = jnp.where(even, pltpu.roll(x, C_QK_DIM - 1, axis=1), pltpu.roll(x, 1, axis=1))
        return x * cos + partner * sin

    q = rot(q_ref[...].astype(F32))
    k = rot(k_ref[...].astype(F32)) * C_QK_DIM ** -0.5
    v = v_ref[...]
    s = lax.dot_general(q.astype(BF16), k.astype(BF16), _NT, preferred_element_type=F32)
    s = s * dmask_ref[...]
    r = r_scr[...]
    o = jnp.dot(s.astype(BF16), v, preferred_element_type=F32)
    o = o + jnp.dot((q * qdec_ref[...]).astype(BF16), r.astype(BF16), preferred_element_type=F32)
    kd = (k * kdec_ref[...]).astype(BF16)
    r_scr[...] = r * cdec_ref[...] + lax.dot_general(kd, v, _TN, preferred_element_type=F32)
    gate = g_ref[...].astype(F32)
    o_ref[...] = (_silu(gate) * _rms_rows(o)).astype(o_ref.dtype)


def _retention(r_all, cos, sin):
    tp = r_all.shape[0]
    ch = RET_CHUNK
    v0 = 2 * C_HEADS * C_QK_DIM // C_V_DIM
    log_gamma = jnp.log(1.0 - 2.0 ** (-5.0 - jnp.arange(C_HEADS, dtype=F32)))
    pos = jnp.arange(ch, dtype=F32)
    rel = pos[:, None] - pos[None, :]
    dmask = jnp.where(rel >= 0, jnp.exp(jnp.maximum(rel, 0.0)[None] * log_gamma[:, None, None]), 0.0)
    qdec = jnp.exp((pos + 1.0)[None, :] * log_gamma[:, None])[:, :, None]
    kdec = jnp.exp((ch - 1.0 - pos)[None, :] * log_gamma[:, None])[:, :, None]
    cdec = jnp.broadcast_to(jnp.exp(ch * log_gamma)[:, None, None], (C_HEADS, 1, C_V_DIM))
    return pl.pallas_call(
        _ret_kernel,
        name="retention",
        out_shape=jax.ShapeDtypeStruct((tp, C_HEADS * C_V_DIM), BF16),
        grid=(C_HEADS, tp // ch),
        in_specs=[
            pl.BlockSpec((ch, C_QK_DIM), lambda h, c: (c, h)),
            pl.BlockSpec((ch, C_QK_DIM), lambda h, c: (c, C_HEADS + h)),
            pl.BlockSpec((ch, C_V_DIM), lambda h, c: (c, v0 + h)),
            pl.BlockSpec((ch, C_V_DIM), lambda h, c: (c, v0 + C_HEADS + h)),
            pl.BlockSpec((ch, C_QK_DIM), lambda h, c: (c, 0)),
            pl.BlockSpec((ch, C_QK_DIM), lambda h, c: (c, 0)),
            pl.BlockSpec((None, ch, ch), lambda h, c: (h, 0, 0)),
            pl.BlockSpec((None, ch, 1), lambda h, c: (h, 0, 0)),
            pl.BlockSpec((None, ch, 1), lambda h, c: (h, 0, 0)),
            pl.BlockSpec((None, 1, C_V_DIM), lambda h, c: (h, 0, 0)),
        ],
        out_specs=pl.BlockSpec((ch, C_V_DIM), lambda h, c: (c, h)),
        scratch_shapes=[pltpu.VMEM((C_QK_DIM, C_V_DIM), F32)],
        compiler_params=_params("arbitrary", "arbitrary"),
    )(r_all, r_all, r_all, r_all, cos, sin, dmask, qdec, kdec, cdec)


def _hybrid_mixer(u_hi, u_lo, w_in, q_norm, w_uq, w_qidx, kv_norm, w_ukv, kidx_norm,
                  conv_w, a_log, dt_bias, o_norm, w_out, *, top_k, pad, tm, tm_wide):
    tp = u_hi.shape[0]
    o = [0]
    for s in (A_Q_RANK, A_KV_RANK, IDX_DIM, IDX_HEADS, 3 * B_HEADS * B_HEAD_DIM,
              B_HEADS * B_HEAD_DIM, B_HEADS, B_HEADS):
        o.append(o[-1] + s)
    col = lambda i: w_in[:, o[i]:o[i + 1]]
    n_small = IDX_DIM + IDX_HEADS + 2 * B_HEADS
    w1 = jnp.concatenate([col(0), col(2), col(3), col(6), col(7),
                          jnp.zeros((w_in.shape[0], LANES - n_small), w_in.dtype)], axis=1)
    tn2 = 2 * MXU_DIM
    n2 = o[6] - o[4] + A_KV_RANK
    w2 = jnp.concatenate([col(4), col(5), col(1),
                          jnp.zeros((w_in.shape[0], -n2 % tn2), w_in.dtype)], axis=1).astype(BF16)
    p1t, small = _matmul("hy_in_x3", (u_hi, u_lo), _hilo(w1), F32, tm=tm, tn=w1.shape[1],
                         epilogue="transposed")
    p2 = _matmul("hy_in", (u_hi,), (w2,), F32, tm=tm_wide, tn=tn2)

    cqt_hi, cqt_lo = _rms_t(p1t, q_norm, width=A_Q_RANK, row_block=0, tc=tm, lo=True)
    up_rows = w_uq.shape[1]
    qi3t = _matmul("dsa_qidx", _hilo(w_qidx.T), (cqt_hi, cqt_lo), BF16, tm=up_rows, tn=tm,
                   epilogue="split3")
    q_scale = A_HEAD_DIM ** -0.5 * math.log2(math.e)
    qt = _matmul("dsa_q", ((w_uq.T * q_scale).astype(BF16),), (cqt_hi,), BF16, tm=up_rows, tn=tm)
    k3 = _kidx(small, kidx_norm, tm=tm)
    wt = p1t[A_Q_RANK + SM_WIDX:A_Q_RANK + SM_BETA]
    ckv_block = (o[6] - o[4]) // A_KV_RANK
    kvn, kvnt = _rms_both(p2, kv_norm, width=A_KV_RANK, col_block=ckv_block, tm=tm)
    w_kv = w_ukv.reshape(A_KV_RANK, A_HEADS, 2, A_HEAD_DIM)
    w_k = w_kv[:, :, 0].reshape(A_KV_RANK, A_HEADS * A_HEAD_DIM).astype(BF16)
    w_vt = w_kv[:, :, 1].reshape(A_KV_RANK, A_HEADS * A_HEAD_DIM).T.astype(BF16)
    k = _matmul("dsa_k", (kvn,), (w_k,), BF16, tm=tm, tn=up_rows)
    vt = _matmul("dsa_vt", (w_vt,), (kvnt,), BF16, tm=up_rows, tn=tm)
    o_a = _dsa(qt, qi3t, wt, k3, k, vt, top_k=top_k, pad=pad, tq=MXU_DIM, tk=tm)

    o_b = _gdn(p2, small, conv_w, a_log, dt_bias, o_norm, pad=pad)

    return _matmul_w32("hy_out", [o_a, o_b], w_out[None], 0, BF16, tm=tm_wide, tn=2 * MXU_DIM)


def _retention_mixer(u, w_in, w_out, *, pad, tm, tm_wide):
    tp = u.shape[0]
    half = C_QK_DIM // 2
    r_all = _matmul_w32("ret_in", [u], w_in[None], 0, BF16, tm=tm_wide, tn=2 * MXU_DIM)
    inv = 1.0 / (ROT_BASE ** jnp.linspace(0.0, 1.0, half, dtype=F32))
    posn = (jnp.arange(tp) - pad).astype(F32)
    ang = posn[:, None] * jnp.repeat(inv, 2)[None, :]
    sign = jnp.tile(jnp.array([-1.0, 1.0], F32), half)
    og = _retention(r_all, jnp.cos(ang), jnp.sin(ang) * sign[None, :])
    return _matmul_w32("ret_out", [og], w_out[None], 0, BF16, tm=tm_wide, tn=2 * MXU_DIM,
                       vmem=BIG_TILE_VMEM_LIMIT)


def _ffn(u, w_gate, w_up, w_down, layer, *, tm, tm_wide):
    act = _ffn_act(u, w_gate, w_up, layer, tm=tm_wide, tn=2 * MXU_DIM)
    return _matmul_w32("ffn_down", [act], w_down, layer, BF16, tm=tm, tn=2 * MXU_DIM,
                       vmem=BIG_TILE_VMEM_LIMIT)


def kernel(x, meta_tokens, mix_norm_pre, mix_norm_post, ffn_norm_pre, ffn_norm_post, hy_w_in, dsa_q_norm, dsa_w_uq, dsa_w_qidx, dsa_kv_norm, dsa_w_ukv, dsa_kidx_norm, gdn_conv_w, gdn_a_log, gdn_dt_bias, gdn_o_norm, hy_w_out, ret_w_in, ret_w_out, ffn_w_gate, ffn_w_up, ffn_w_down):
    b, seq, d = x.shape
    assert b == 1 and mix_norm_pre.shape[0] == 2
    pad = (-N_META) % ROW_ALIGN
    tp = pad + N_META + seq
    tm = ROW_TILE
    assert tp % tm == 0
    tm_wide = WIDE_ROW_TILE if tp % WIDE_ROW_TILE == 0 else tm
    top_k = min(TOPK_MAX, seq // TOPK_FRAC)
    h, u_hi, u_lo = _embed_norm(x[0], meta_tokens, mix_norm_pre[0], pad=pad)
    mix = _hybrid_mixer(u_hi, u_lo, hy_w_in[0], dsa_q_norm[0], dsa_w_uq[0], dsa_w_qidx[0],
                        dsa_kv_norm[0], dsa_w_ukv[0], dsa_kidx_norm[0], gdn_conv_w[0],
                        gdn_a_log[0], gdn_dt_bias[0], gdn_o_norm[0], hy_w_out[0],
                        top_k=top_k, pad=pad, tm=tm, tm_wide=tm_wide)
    h, u = _resid_norm(h, mix, mix_norm_post[0], ffn_norm_pre[0], tm=tm)
    f = _ffn(u, ffn_w_gate, ffn_w_up, ffn_w_down, 0, tm=tm, tm_wide=tm_wide)
    h, u = _resid_norm(h, f, ffn_norm_post[0], mix_norm_pre[1], tm=tm)
    mix = _retention_mixer(u, ret_w_in[0], ret_w_out[0], pad=pad, tm=tm, tm_wide=tm_wide)
    h, u = _resid_norm(h, mix, mix_norm_post[1], ffn_norm_pre[1], tm=tm)
    f = _ffn(u, ffn_w_gate, ffn_w_up, ffn_w_down, 1, tm=tm, tm_wide=tm_wide)
    out = _resid_norm(h, f, ffn_norm_post[1], None, tm=ROW_ALIGN, row0=pad + N_META,
                      out_dtype=x.dtype)
    return out[None]
```

```python
import functools
import math

import jax
import jax.numpy as jnp
from jax import lax
from jax.experimental import pallas as pl
from jax.experimental.pallas import tpu as pltpu

F32 = jnp.float32
BF16 = jnp.bfloat16

N_META = 16
NORM_EPS = 1e-6

A_HEADS = 8
A_HEAD_DIM = 128
A_Q_RANK = 512
A_KV_RANK = 256
IDX_HEADS = 16
IDX_DIM = 64
TOPK_MAX = 256
TOPK_FRAC = 4

B_HEADS = 8
B_HEAD_DIM = 128
CONV_K = 4
GDN_CHUNK = 128
DSA_HEADS_PER_STEP = 2
DSA_VMEM_LIMIT = 56 * 1024 * 1024
BIG_TILE_VMEM_LIMIT = 56 * 1024 * 1024
DSA_SEGMENTS = 3
DSA_SUB_BLOCK = 64

C_HEADS = 8
C_QK_DIM = 256
C_V_DIM = 512
ROT_BASE = 10000.0
RET_CHUNK = 384

LANES = 128
SUBLANES = 8
MXU_DIM = 256
ROW_ALIGN = MXU_DIM
ROW_TILE = 3 * MXU_DIM
WIDE_ROW_TILE = 11 * LANES
VMEM_LIMIT = 48 * 1024 * 1024

SM_WIDX = IDX_DIM
SM_BETA = SM_WIDX + IDX_HEADS
SM_A = SM_BETA + B_HEADS

INT_MIN = -2 ** 31
KEY_FLT_MAX = 0x7F7FFFFF
KEY_MIN_NORMAL = 0x00800000
KEY_FLT_LOWEST = (0xFF7FFFFF ^ 0x7FFFFFFF) - 2 ** 32


def _params(*sem, vmem=VMEM_LIMIT):
    return pltpu.CompilerParams(dimension_semantics=sem, vmem_limit_bytes=vmem)


def _hilo(x):
    hi = x.astype(BF16)
    return hi, (x - hi.astype(F32)).astype(BF16)


def _silu(x):
    return x * jax.nn.sigmoid(x)


def _mm_kernel(*refs, n_parts, epilogue):
    a, b, o = refs[:n_parts], refs[n_parts:2 * n_parts], refs[2 * n_parts:]
    a_hi, b_hi = a[0][...], b[0][...]
    acc = jnp.dot(a_hi, b_hi, preferred_element_type=F32)
    if n_parts == 2:
        acc += jnp.dot(a_hi, b[1][...], preferred_element_type=F32)
        acc += jnp.dot(a[1][...], b_hi, preferred_element_type=F32)
    if epilogue is None:
        o[0][...] = acc.astype(o[0].dtype)
    elif epilogue == "transposed":
        o[0][...] = acc.T
        o[1][...] = acc[:, acc.shape[1] - LANES:]
    elif epilogue == "split3":
        for g in range(acc.shape[0] // IDX_DIM):
            hi, lo = _hilo(acc[g * IDX_DIM:(g + 1) * IDX_DIM, :])
            base = g * MXU_DIM
            o[0][base:base + IDX_DIM, :] = hi
            o[0][base + IDX_DIM:base + 2 * IDX_DIM, :] = lo
            o[0][base + 2 * IDX_DIM:base + 3 * IDX_DIM, :] = hi
            o[0][base + 3 * IDX_DIM:base + MXU_DIM, :] = jnp.zeros_like(hi)


def _matmul(name, a_parts, b_parts, out_dtype, *, tm, tn, epilogue=None):
    m, kdim = a_parts[0].shape
    n = b_parts[0].shape[1]
    assert m % tm == 0 and n % tn == 0, (m, n, tm, tn)
    assert len(a_parts) == len(b_parts)
    if epilogue is None:
        out_shape = [jax.ShapeDtypeStruct((m, n), out_dtype)]
        out_specs = [pl.BlockSpec((tm, tn), lambda i, j: (i, j))]
    elif epilogue == "transposed":
        assert tn == n
        out_shape = [jax.ShapeDtypeStruct((n, m), out_dtype),
                     jax.ShapeDtypeStruct((m, LANES), out_dtype)]
        out_specs = [pl.BlockSpec((n, tm), lambda i, j: (0, i)),
                     pl.BlockSpec((tm, LANES), lambda i, j: (i, 0))]
    else:
        scale = MXU_DIM // IDX_DIM
        out_shape = [jax.ShapeDtypeStruct((m * scale, n), out_dtype)]
        out_specs = [pl.BlockSpec((tm * scale, tn), lambda i, j: (i, j))]
    out = pl.pallas_call(
        functools.partial(_mm_kernel, n_parts=len(a_parts), epilogue=epilogue),
        name=name,
        out_shape=out_shape,
        grid=(m // tm, n // tn),
        in_specs=[pl.BlockSpec((tm, kdim), lambda i, j: (i, 0))] * len(a_parts)
        + [pl.BlockSpec((kdim, tn), lambda i, j: (0, j))] * len(b_parts),
        out_specs=out_specs,
        compiler_params=_params("parallel", "parallel"),
    )(*a_parts, *b_parts)
    return out[0] if len(out) == 1 else out


def _mm_w32_kernel(*refs, n_a):
    a, w_ref, o_ref = refs[:n_a], refs[n_a], refs[n_a + 1]
    w = w_ref[...].astype(BF16)
    acc, k0 = None, 0
    for a_ref in a:
        k1 = k0 + a_ref.shape[1]
        part = jnp.dot(a_ref[...], w[k0:k1, :], preferred_element_type=F32)
        acc = part if acc is None else acc + part
        k0 = k1
    o_ref[...] = acc.astype(o_ref.dtype)


def _matmul_w32(name, a_list, w, layer, out_dtype, *, tm, tn, col0=0, n=None, vmem=VMEM_LIMIT):
    m = a_list[0].shape[0]
    kdim = sum(a.shape[1] for a in a_list)
    n = w.shape[2] - col0 if n is None else n
    assert w.shape[1] == kdim and m % tm == 0 and n % tn == 0 and col0 % tn == 0
    jb = col0 // tn
    return pl.pallas_call(
        functools.partial(_mm_w32_kernel, n_a=len(a_list)),
        name=name,
        out_shape=jax.ShapeDtypeStruct((m, n), out_dtype),
        grid=(m // tm, n // tn),
        in_specs=[pl.BlockSpec((tm, a.shape[1]), lambda i, j: (i, 0)) for a in a_list]
        + [pl.BlockSpec((None, kdim, tn), lambda i, j: (layer, 0, j + jb))],
        out_specs=pl.BlockSpec((tm, tn), lambda i, j: (i, j)),
        compiler_params=_params("parallel", "parallel", vmem=vmem),
    )(*a_list, w)


def _ffn_act_kernel(x_ref, wg_ref, wu_ref, o_ref):
    x = x_ref[...]
    g = jnp.dot(x, wg_ref[...].astype(BF16), preferred_element_type=F32)
    u = jnp.dot(x, wu_ref[...].astype(BF16), preferred_element_type=F32)
    o_ref[...] = (_silu(g) * u).astype(o_ref.dtype)


def _ffn_act(x, wg, wu, layer, *, tm, tn):
    m, kdim = x.shape
    n = wg.shape[2]
    assert m % tm == 0 and n % tn == 0
    w_spec = pl.BlockSpec((None, kdim, tn), lambda i, j: (layer, 0, j))
    return pl.pallas_call(
        _ffn_act_kernel,
        name="ffn_act",
        out_shape=jax.ShapeDtypeStruct((m, n), BF16),
        grid=(m // tm, n // tn),
        in_specs=[pl.BlockSpec((tm, kdim), lambda i, j: (i, 0)), w_spec, w_spec],
        out_specs=pl.BlockSpec((tm, tn), lambda i, j: (i, j)),
        compiler_params=_params("parallel", "parallel"),
    )(x, wg, wu)


def _rms_rows(x):
    return x * lax.rsqrt(jnp.mean(x * x, axis=-1, keepdims=True) + NORM_EPS)


def _embed_kernel(x_ref, head_ref, g_ref, h_ref, hi_ref, lo_ref):
    h = jnp.where(pl.program_id(0) == 0, head_ref[...], x_ref[...].astype(F32))
    h_ref[...] = h
    y = _rms_rows(h) * g_ref[...]
    hi = y.astype(BF16)
    hi_ref[...] = hi
    lo_ref[...] = (y - hi.astype(F32)).astype(BF16)


def _embed_norm(x, meta, gain, *, pad):
    seq, d = x.shape
    tb = pad + meta.shape[0]
    assert seq % tb == 0
    head = jnp.concatenate([jnp.zeros((pad, d), F32), meta.astype(F32)], axis=0)
    tp = tb + seq
    row = pl.BlockSpec((tb, d), lambda i: (i, 0))
    return pl.pallas_call(
        _embed_kernel,
        name="embed_norm",
        out_shape=[jax.ShapeDtypeStruct((tp, d), F32)] + [jax.ShapeDtypeStruct((tp, d), BF16)] * 2,
        grid=(tp // tb,),
        in_specs=[pl.BlockSpec((tb, d), lambda i: (jnp.maximum(i - 1, 0), 0)),
                  pl.BlockSpec((tb, d), lambda i: (0, 0)),
                  pl.BlockSpec((1, d), lambda i: (0, 0))],
        out_specs=[row, row, row],
        compiler_params=_params("parallel"),
    )(x, head, gain.reshape(1, d).astype(F32))


def _rms_both_kernel(x_ref, g_ref, o_ref, ot_ref):
    y = _rms_rows(x_ref[...]) * g_ref[...]
    o_ref[...] = y.astype(BF16)
    ot_ref[...] = y.T.astype(BF16)


def _rms_both(x, gain, *, width, col_block, tm):
    m = x.shape[0]
    return pl.pallas_call(
        _rms_both_kernel,
        name="rms_both",
        out_shape=[jax.ShapeDtypeStruct((m, width), BF16), jax.ShapeDtypeStruct((width, m), BF16)],
        grid=(m // tm,),
        in_specs=[pl.BlockSpec((tm, width), lambda i: (i, col_block)),
                  pl.BlockSpec((1, width), lambda i: (0, 0))],
        out_specs=[pl.BlockSpec((tm, width), lambda i: (i, 0)),
                   pl.BlockSpec((width, tm), lambda i: (0, i))],
        compiler_params=_params("parallel"),
    )(x, gain.reshape(1, width).astype(F32))


def _rms_t_kernel(x_ref, g_ref, *o_refs):
    x = x_ref[...]
    y = x * lax.rsqrt(jnp.mean(x * x, axis=0, keepdims=True) + NORM_EPS) * g_ref[...]
    hi = y.astype(BF16)
    o_refs[0][...] = hi
    if len(o_refs) == 2:
        o_refs[1][...] = (y - hi.astype(F32)).astype(BF16)


def _rms_t(xt, gain, *, width, row_block, tc, lo=False):
    t = xt.shape[1]
    n_out = 2 if lo else 1
    out = pl.pallas_call(
        _rms_t_kernel,
        name="rms_t",
        out_shape=[jax.ShapeDtypeStruct((width, t), BF16)] * n_out,
        grid=(t // tc,),
        in_specs=[pl.BlockSpec((width, tc), lambda i: (row_block, i)),
                  pl.BlockSpec((width, 1), lambda i: (0, 0))],
        out_specs=[pl.BlockSpec((width, tc), lambda i: (0, i))] * n_out,
        compiler_params=_params("parallel"),
    )(xt, gain.reshape(width, 1).astype(F32))
    return out if lo else out[0]


def _resid_kernel(h_ref, u_ref, gp_ref, *rest, with_next):
    hn = h_ref[...] + _rms_rows(u_ref[...].astype(F32)) * gp_ref[...]
    if with_next:
        gn_ref, hn_ref, un_ref = rest
        un_ref[...] = (_rms_rows(hn) * gn_ref[...]).astype(BF16)
    else:
        (hn_ref,) = rest
    hn_ref[...] = hn.astype(hn_ref.dtype)


def _resid_norm(h, u, g_post, g_next, *, tm, row0=0, out_dtype=F32):
    m, d = h.shape
    assert row0 % tm == 0 and m % tm == 0
    b0 = row0 // tm
    row_in = pl.BlockSpec((tm, d), lambda i: (i + b0, 0))
    row_out = pl.BlockSpec((tm, d), lambda i: (i, 0))
    vec = pl.BlockSpec((1, d), lambda i: (0, 0))
    with_next = g_next is not None
    gains = [g_post.reshape(1, d).astype(F32)]
    out_shape = [jax.ShapeDtypeStruct((m - row0, d), out_dtype)]
    if with_next:
        gains.append(g_next.reshape(1, d).astype(F32))
        out_shape.append(jax.ShapeDtypeStruct((m - row0, d), BF16))
    out = pl.pallas_call(
        functools.partial(_resid_kernel, with_next=with_next),
        name="resid_norm",
        out_shape=out_shape,
        grid=((m - row0) // tm,),
        in_specs=[row_in, row_in] + [vec] * len(gains),
        out_specs=[row_out] * len(out_shape),
        compiler_params=_params("parallel"),
    )(h, u, *gains)
    return out if with_next else out[0]


def _kidx_kernel(sm_ref, g_ref, o_ref):
    sm = sm_ref[...]
    lane = lax.broadcasted_iota(jnp.int32, sm.shape, 1)
    x = jnp.where(lane < IDX_DIM, sm, 0.0)
    ms = jnp.sum(x * x, axis=-1, keepdims=True) * (1.0 / IDX_DIM)
    y = x * lax.rsqrt(ms + NORM_EPS) * g_ref[...]
    hi = y.astype(BF16)
    lo = (y - hi.astype(F32)).astype(BF16)
    y2 = y + pltpu.roll(y, IDX_DIM, axis=1)
    o_ref[:, 0:LANES] = y2.astype(BF16)
    o_ref[:, LANES:2 * LANES] = lo


def _kidx(p1, gain, *, tm):
    m = p1.shape[0]
    g = jnp.zeros((1, LANES), F32).at[0, :IDX_DIM].set(gain.astype(F32))
    return pl.pallas_call(
        _kidx_kernel,
        name="dsa_kidx",
        out_shape=jax.ShapeDtypeStruct((m, 2 * LANES), BF16),
        grid=(m // tm,),
        in_specs=[pl.BlockSpec((tm, LANES), lambda i: (i, 0)),
                  pl.BlockSpec((1, LANES), lambda i: (0, 0))],
        out_specs=pl.BlockSpec((tm, 2 * LANES), lambda i: (i, 0)),
        compiler_params=_params("parallel"),
    )(p1, g)


def _dsa_kernel(qt_ref, qi3t_ref, wt_ref, k3_ref, k_ref, vt_ref, o_ref, sc_scr, sc16_scr, acc_scr,
                sa_scr, sb_scr, pa_scr, pb_scr, *, tq, tk, sb, qb0, top_k, pad, hps, w_scale,
                pos_bits):
    qi = pl.program_id(0)
    hd = pl.program_id(1)
    q0 = (qi + qb0) * tq
    nkt = (q0 + tq + tk - 1) // tk

    krel = lax.broadcasted_iota(jnp.int32, (tk, tq), 0)

    @pl.when(hd == 0)
    def _select():
        qpos = q0 + lax.broadcasted_iota(jnp.int32, (tk, tq), 1)

        def score_tile(kt, carry):
            ks = pl.multiple_of(kt * tk, tk)
            k3 = k3_ref[pl.ds(ks, tk), :]
            acc = jnp.zeros((tk, tq), F32)
            for ih in range(IDX_HEADS):
                d = jnp.dot(k3, qi3t_ref[ih * MXU_DIM:(ih + 1) * MXU_DIM, :],
                            preferred_element_type=F32)
                acc = acc + (wt_ref[ih:ih + 1, :] * w_scale) * jnp.maximum(d, 0.0)
            kpos = ks + krel
            ok = (kpos <= qpos) & (kpos >= pad)
            masked = jnp.where(ok, acc, -jnp.inf)
            sc_scr[pl.ds(ks, tk), :] = masked
            sc16_scr[pl.ds(ks, tk), :] = masked.astype(BF16)
            return carry

        lax.fori_loop(0, nkt, score_tile, 0)

        def count(pred):
            def body(kt, acc):
                ks = pl.multiple_of(kt * tk, tk)
                hit = pred(sc_scr[pl.ds(ks, tk), :], ks).astype(jnp.int32)
                return acc + jnp.sum(hit.reshape(tk // SUBLANES, SUBLANES, tq), axis=0)

            acc = lax.fori_loop(0, nkt, body, jnp.zeros((SUBLANES, tq), jnp.int32))
            return jnp.sum(acc, axis=0, keepdims=True)

        def key_to_f32(t):
            t = jnp.clip(t, jnp.int32(KEY_FLT_LOWEST), jnp.int32(KEY_FLT_MAX))
            return pltpu.bitcast(jnp.where(t >= 0, t, t ^ jnp.int32(0x7FFFFFFF)), F32)

        def count16(cand_b):
            rows = 2 * SUBLANES
            one, nil = jnp.ones((), BF16), jnp.zeros((), BF16)
            cand_b = jnp.maximum(cand_b, jnp.finfo(BF16).min)

            def body(kt, acc):
                ks = pl.multiple_of(kt * tk, tk)
                hit = jnp.where(sc16_scr[pl.ds(ks, tk), :] >= cand_b, one, nil)
                slabs = [hit[r:r + rows, :] for r in range(0, tk, rows)]
                while len(slabs) > 1:
                    pairs = [a + b for a, b in zip(slabs[0::2], slabs[1::2])]
                    slabs = pairs + slabs[2 * len(pairs):]
                return acc + slabs[0].astype(F32)

            acc = lax.fori_loop(0, nkt, body, jnp.zeros((rows, tq), F32))
            return jnp.sum(acc, axis=0, keepdims=True).astype(jnp.int32)

        zero = jnp.zeros((1, tq), jnp.int32)
        accept = lambda n, cand: (n >= top_k) & (cand <= jnp.int32(KEY_FLT_MAX))
        n0 = count16(jnp.zeros((1, tq), BF16))
        t_hi = jnp.where(n0 >= top_k, zero, jnp.int32(INT_MIN))

        def hi_body(i, t):
            cand = t + lax.shift_left(jnp.int32(1), 30 - i)
            n = count16(key_to_f32(cand).astype(BF16))
            return jnp.where(accept(n, cand), cand, t)

        t_hi = lax.fori_loop(0, 15, hi_body, t_hi)

        base = t_hi - jnp.int32(1 << 15)

        def lo_body(i, state):
            off, n_at = state
            bit = lax.shift_left(jnp.int32(1), 16 - i)
            cand = base + (off | bit)
            cand_f = key_to_f32(cand)
            n = count(lambda s, ks: s >= cand_f)
            ok = accept(n, cand)
            return jnp.where(ok, off | bit, off), jnp.where(ok, n, n_at)

        off, n_at = lax.fori_loop(0, 17, lo_body, (zero, zero + (top_k + 1)))
        few = t_hi == jnp.int32(INT_MIN)
        t = jnp.where(few, t_hi, base + off)
        thr = key_to_f32(t)

        n_ge = jnp.where(few, zero, n_at)

        @pl.when(jnp.max(n_ge) > top_k)
        def _ties():
            flushed = (t >= jnp.int32(-KEY_MIN_NORMAL)) & (t < jnp.int32(KEY_MIN_NORMAL))
            thr_next = key_to_f32(jnp.where(flushed, jnp.int32(KEY_MIN_NORMAL), t + 1))
            tie = lambda s: (s >= thr) & jnp.logical_not(s >= thr_next)
            want = top_k - count(lambda s, ks: s >= thr_next)

            def pos_body(i, cut):
                cand = cut + lax.shift_left(jnp.int32(1), pos_bits - 1 - i)
                n = count(lambda s, ks: tie(s) & (ks + krel < cand))
                return jnp.where(n < want, cand, cut)

            cut = lax.fori_loop(0, pos_bits, pos_body, zero)

            def drop_tile(kt, carry):
                ks = pl.multiple_of(kt * tk, tk)
                s = sc_scr[pl.ds(ks, tk), :]
                sc_scr[pl.ds(ks, tk), :] = jnp.where(tie(s) & (ks + krel > cut), -jnp.inf, s)
                return carry

            lax.fori_loop(0, nkt, drop_tile, 0)

        def bias_tile(kt, carry):
            ks = pl.multiple_of(kt * tk, tk)
            sc_scr[pl.ds(ks, tk), :] = jnp.where(sc_scr[pl.ds(ks, tk), :] >= thr, 0.0, -jnp.inf)
            return carry

        lax.fori_loop(0, nkt, bias_tile, 0)

    hd_sl = [slice(j * A_HEAD_DIM, (j + 1) * A_HEAD_DIM) for j in range(hps)]
    q = [qt_ref[sl, :] for sl in hd_sl]

    def logits_into(kt, s_ref):
        ks = pl.multiple_of(kt * tk, tk)
        for j in range(hps):
            s_ref[j] = jnp.dot(k_ref[pl.ds(ks, tk), hd_sl[j]], q[j], preferred_element_type=F32)

    fold = lambda x: x.reshape(sb // SUBLANES, SUBLANES, tq)

    def att_tile(kt, kt_next, bias_row, s_cur, s_next, p_ref, carry):
        if s_next is not None:
            logits_into(kt_next, s_next)
        ks = pl.multiple_of(kt * tk, tk)

        def masked(j, i):
            r = i * sb
            return s_cur[j, r:r + sb, :] + sc_scr[pl.ds(bias_row + r, sb), :]

        out, alphas = [], []
        for j in range(hps):
            m, l = carry[j]
            mx = jnp.full((SUBLANES, tq), -jnp.inf, F32)
            for i in range(tk // sb):
                mx = jnp.maximum(mx, jnp.max(fold(masked(j, i)), axis=0))
            m_new = jnp.maximum(m, jnp.max(mx, axis=0, keepdims=True))
            m_safe = jnp.where(m_new == -jnp.inf, 0.0, m_new)
            for i in range(tk // sb):
                p = jnp.exp2(masked(j, i) - m_safe)
                p_ref[j, i * sb:(i + 1) * sb, :] = p.astype(BF16)
            alphas.append(jnp.exp2(m - m_safe))
            out.append(m_new)
        ones = jnp.ones((2 * SUBLANES, tk), BF16)
        pv = [jnp.dot(jnp.concatenate([vt_ref[hd_sl[j], pl.ds(ks, tk)], ones], axis=0), p_ref[j],
                      preferred_element_type=F32) for j in range(hps)]
        for j in range(hps):
            acc_scr[j] = alphas[j] * acc_scr[j] + pv[j][:A_HEAD_DIM, :]
        return tuple((out[j], alphas[j] * carry[j][1] + pv[j][A_HEAD_DIM:A_HEAD_DIM + 1, :])
                     for j in range(hps))

    def att_pair(i, carry):
        kt0 = 2 * i
        kt2 = jnp.minimum(kt0 + 2, nkt - 1)
        row = lambda kt: pl.multiple_of(kt * tk, tk)
        carry = att_tile(kt0, kt0 + 1, row(kt0), sa_scr, sb_scr, pa_scr, carry)
        return att_tile(kt0 + 1, kt2, row(kt0 + 1), sb_scr, sa_scr, pb_scr, carry)

    acc_scr[...] = jnp.zeros_like(acc_scr)
    logits_into(0, sa_scr)
    init = (jnp.full((1, tq), -jnp.inf, F32), jnp.zeros((1, tq), F32))
    res = lax.fori_loop(0, nkt // 2, att_pair, (init,) * hps)
    last = nkt - 1
    res = lax.cond(
        nkt % 2 == 1,
        lambda c: att_tile(last, None, pl.multiple_of(last * tk, tk), sa_scr, None, pa_scr, c),
        lambda c: c, res)
    for j in range(hps):
        l = res[j][1]
        o_ref[:, hd_sl[j]] = jnp.where(l > 0.0, acc_scr[j] / l, 0.0).T.astype(o_ref.dtype)


def _dsa(qt, qi3t, wt, k3, k, vt, *, top_k, pad, tq, tk):
    tp = k.shape[0]
    assert tp % tq == 0 and tp % tk == 0 and tk >= top_k
    hps = DSA_HEADS_PER_STEP
    hw = hps * A_HEAD_DIM
    n_tiles = tp // tk
    outs = []
    for seg in range(DSA_SEGMENTS):
        t0, t1 = seg * n_tiles // DSA_SEGMENTS, (seg + 1) * n_tiles // DSA_SEGMENTS
        if t1 == t0:
            continue
        qb0, nq, nk = t0 * tk // tq, (t1 - t0) * tk // tq, t1 * tk
        kern = functools.partial(
            _dsa_kernel, tq=tq, tk=tk, sb=DSA_SUB_BLOCK, qb0=qb0, top_k=top_k, pad=pad, hps=hps,
            w_scale=(IDX_HEADS * IDX_DIM) ** -0.5, pos_bits=tp.bit_length())
        outs.append(pl.pallas_call(
            kern,
            name="dsa",
            out_shape=jax.ShapeDtypeStruct((nq * tq, A_HEADS * A_HEAD_DIM), BF16),
            grid=(nq, A_HEADS // hps),
            in_specs=[
                pl.BlockSpec((hw, tq), lambda i, h, qb0=qb0: (h, i + qb0)),
                pl.BlockSpec((IDX_HEADS * MXU_DIM, tq), lambda i, h, qb0=qb0: (0, i + qb0)),
                pl.BlockSpec((IDX_HEADS, tq), lambda i, h, qb0=qb0: (0, i + qb0)),
                pl.BlockSpec((nk, MXU_DIM), lambda i, h: (0, 0)),
                pl.BlockSpec((nk, hw), lambda i, h: (0, h)),
                pl.BlockSpec((hw, nk), lambda i, h: (h, 0)),
            ],
            out_specs=pl.BlockSpec((tq, hw), lambda i, h: (i, h)),
            scratch_shapes=[pltpu.VMEM((nk, tq), F32), pltpu.VMEM((nk, tq), BF16),
                            pltpu.VMEM((hps, A_HEAD_DIM, tq), F32)]
            + [pltpu.VMEM((hps, tk, tq), F32)] * 2 + [pltpu.VMEM((hps, tk, tq), BF16)] * 2,
            compiler_params=_params("arbitrary", "arbitrary", vmem=DSA_VMEM_LIMIT),
        )(qt, qi3t, wt, k3, k, vt))
    return outs[0] if len(outs) == 1 else jnp.concatenate(outs, axis=0)


def _dot3(a, b, dims=(((1,), (0,)), ((), ()))):
    ah, al = _hilo(a)
    bh, bl = _hilo(b)
    f = lambda x, y: lax.dot_general(x, y, dims, preferred_element_type=F32)
    return f(ah, bh) + f(ah, bl) + f(al, bh)


_NN = (((1,), (0,)), ((), ()))
_NT = (((1,), (1,)), ((), ()))
_TN = (((0,), (0,)), ((), ()))


def _mm(a, b, dims=_NN):
    return lax.dot_general(a.astype(BF16), b.astype(BF16), dims, preferred_element_type=F32)


def _gdn_kernel(qkv_ref, z_ref, sm_ref, convw_ref, alog_ref, dtb_ref, onorm_ref, o_ref,
                xbuf, s_scr, *, pad):
    c = pl.program_id(0)
    ch = GDN_CHUNK
    hist = SUBLANES

    @pl.when(c == 0)
    def _():
        xbuf[0:hist, :] = jnp.zeros((hist, xbuf.shape[1]), F32)
        s_scr[...] = jnp.zeros_like(s_scr)

    x = qkv_ref[...]
    xbuf[hist:hist + ch, :] = x
    w = convw_ref[...]
    y = x * w[CONV_K - 1:CONV_K, :]
    for j in range(CONV_K - 1):
        y = y + xbuf[hist - (CONV_K - 1) + j:hist - (CONV_K - 1) + j + ch, :] * w[j:j + 1, :]
    xbuf[0:hist, :] = x[ch - hist:ch, :]
    y = _silu(y)

    sm = sm_ref[...]
    row = c * ch + lax.broadcasted_iota(jnp.int32, sm.shape, 0)
    beta = jax.nn.sigmoid(sm)
    zsm = sm + dtb_ref[...]
    softplus = jnp.maximum(zsm, 0.0) + jnp.log(1.0 + jnp.exp(-jnp.abs(zsm)))
    g = jnp.where(row >= pad, -jnp.exp(alog_ref[...]) * softplus, 0.0)

    ri = lax.broadcasted_iota(jnp.int32, (ch, ch), 0)
    ci = lax.broadcasted_iota(jnp.int32, (ch, ch), 1)
    lower = ri >= ci
    strict = ri > ci
    tri = jnp.where(lower, 1.0, 0.0).astype(BF16)
    g1 = g.astype(BF16)
    r1 = g - g1.astype(F32)
    g2 = r1.astype(BF16)
    g3 = (r1 - g2.astype(F32)).astype(BF16)
    gcum = (jnp.dot(tri, g1, preferred_element_type=F32)
            + jnp.dot(tri, g2, preferred_element_type=F32)
            + jnp.dot(tri, g3, preferred_element_type=F32))
    gcum_t = gcum.T
    eye = jnp.where(ri == ci, 1.0, 0.0).astype(F32)

    bw = B_HEADS * B_HEAD_DIM
    hs = range(B_HEADS)
    sl = [slice(h * B_HEAD_DIM, (h + 1) * B_HEAD_DIM) for h in hs]
    part = lambda j: [y[:, j * bw + h * B_HEAD_DIM:j * bw + (h + 1) * B_HEAD_DIM] for h in hs]
    l2n = lambda x: x * lax.rsqrt(jnp.sum(x * x, axis=-1, keepdims=True) + NORM_EPS)
    q = [l2n(x) * B_HEAD_DIM ** -0.5 for x in part(0)]
    k = [l2n(x) for x in part(1)]
    v = part(2)
    b = [beta[:, SM_BETA + h:SM_BETA + h + 1] for h in hs]
    gc = [gcum[:, SM_A + h:SM_A + h + 1] for h in hs]
    gr = [gcum_t[SM_A + h:SM_A + h + 1, :] for h in hs]
    gl = [x[ch - 1:ch, :] for x in gc]
    eg = [jnp.exp(x) for x in gc]
    decay = [jnp.where(lower, jnp.exp(jnp.where(lower, gc[h] - gr[h], 0.0)), 0.0) for h in hs]
    kb = [k[h] * b[h] for h in hs]
    a_mat = [jnp.where(strict, _mm(kb[h], k[h], _NT) * decay[h], 0.0) for h in hs]
    t_mat = [eye - a for a in a_mat]
    pw = a_mat
    for _ in range(ch.bit_length() - 2):
        pw = [_dot3(p, p) for p in pw]
        t_mat = [t_mat[h] + _dot3(t_mat[h], pw[h]) for h in hs]
    u = [_mm(t_mat[h], v[h] * b[h]) for h in hs]
    wm = [_mm(t_mat[h], kb[h] * eg[h]) for h in hs]
    qk = [jnp.where(lower, _mm(q[h], k[h], _NT) * decay[h], 0.0) for h in hs]
    state = [s_scr[h] for h in hs]
    v_new = [u[h] - _mm(wm[h], state[h]) for h in hs]
    o = [_mm(q[h] * eg[h], state[h]) + _mm(qk[h], v_new[h]) for h in hs]
    kd = [k[h] * jnp.exp(gl[h] - gc[h]) for h in hs]
    new_state = [state[h] * jnp.exp(gl[h]) + _mm(kd[h], v_new[h], _TN) for h in hs]
    for h in hs:
        s_scr[h] = new_state[h]
    for h in hs:
        o_h = _rms_rows(o[h]) * onorm_ref[...] * _silu(z_ref[:, sl[h]])
        o_ref[:, sl[h]] = o_h.astype(o_ref.dtype)


def _gdn(p2, p1, conv_w, a_log, dt_bias, o_norm, *, pad):
    tp = p2.shape[0]
    bw = B_HEADS * B_HEAD_DIM
    alog = jnp.zeros((1, LANES), F32).at[0, SM_A:SM_A + B_HEADS].set(a_log.astype(F32))
    dtb = jnp.zeros((1, LANES), F32).at[0, SM_A:SM_A + B_HEADS].set(dt_bias.astype(F32))
    ch = GDN_CHUNK
    return pl.pallas_call(
        functools.partial(_gdn_kernel, pad=pad),
        name="gdn",
        out_shape=jax.ShapeDtypeStruct((tp, bw), BF16),
        grid=(tp // ch,),
        in_specs=[
            pl.BlockSpec((ch, 3 * bw), lambda c: (c, 0)),
            pl.BlockSpec((ch, bw), lambda c: (c, 3)),
            pl.BlockSpec((ch, LANES), lambda c: (c, 0)),
            pl.BlockSpec((CONV_K, 3 * bw), lambda c: (0, 0)),
            pl.BlockSpec((1, LANES), lambda c: (0, 0)),
            pl.BlockSpec((1, LANES), lambda c: (0, 0)),
            pl.BlockSpec((1, B_HEAD_DIM), lambda c: (0, 0)),
        ],
        out_specs=pl.BlockSpec((ch, bw), lambda c: (c, 0)),
        scratch_shapes=[pltpu.VMEM((SUBLANES + ch, 3 * bw), F32),
                        pltpu.VMEM((B_HEADS, B_HEAD_DIM, B_HEAD_DIM), F32)],
        compiler_params=_params("arbitrary"),
    )(p2, p2, p1, conv_w.astype(F32), alog, dtb, o_norm.reshape(1, B_HEAD_DIM).astype(F32))


def _ret_kernel(q_ref, k_ref, v_ref, g_ref, cos_ref, sin_ref, dmask_ref, qdec_ref, kdec_ref,
                cdec_ref, o_ref, r_scr):
    @pl.when(pl.program_id(1) == 0)
    def _():
        r_scr[...] = jnp.zeros_like(r_scr)

    cos, sin = cos_ref[...], sin_ref[...]
    even = lax.broadcasted_iota(jnp.int32, cos.shape, 1) % 2 == 0

    def rot(x):
        partner = jnp.where(even, pltpu.roll(x, C_QK_DIM - 1, axis=1), pltpu.roll(x, 1, axis=1))
        return x * cos + partner * sin

    q = rot(q_ref[...].astype(F32))
    k = rot(k_ref[...].astype(F32)) * C_QK_DIM ** -0.5
    v = v_ref[...]
    s = lax.dot_general(q.astype(BF16), k.astype(BF16), _NT, preferred_element_type=F32)
    s = s * dmask_ref[...]
    r = r_scr[...]
    o = jnp.dot(s.astype(BF16), v, preferred_element_type=F32)
    o = o + jnp.dot((q * qdec_ref[...]).astype(BF16), r.astype(BF16), preferred_element_type=F32)
    kd = (k * kdec_ref[...]).astype(BF16)
    r_scr[...] = r * cdec_ref[...] + lax.dot_general(kd, v, _TN, preferred_element_type=F32)
    gate = g_ref[...].astype(F32)
    o_ref[...] = (_silu(gate) * _rms_rows(o)).astype(o_ref.dtype)


def _retention(r_all, cos, sin):
    tp = r_all.shape[0]
    ch = RET_CHUNK
    v0 = 2 * C_HEADS * C_QK_DIM // C_V_DIM
    log_gamma = jnp.log(1.0 - 2.0 ** (-5.0 - jnp.arange(C_HEADS, dtype=F32)))
    pos = jnp.arange(ch, dtype=F32)
    rel = pos[:, None] - pos[None, :]
    dmask = jnp.where(rel >= 0, jnp.exp(jnp.maximum(rel, 0.0)[None] * log_gamma[:, None, None]), 0.0)
    qdec = jnp.exp((pos + 1.0)[None, :] * log_gamma[:, None])[:, :, None]
    kdec = jnp.exp((ch - 1.0 - pos)[None, :] * log_gamma[:, None])[:, :, None]
    cdec = jnp.broadcast_to(jnp.exp(ch * log_gamma)[:, None, None], (C_HEADS, 1, C_V_DIM))
    return pl.pallas_call(
        _ret_kernel,
        name="retention",
        out_shape=jax.ShapeDtypeStruct((tp, C_HEADS * C_V_DIM), BF16),
        grid=(C_HEADS, tp // ch),
        in_specs=[
            pl.BlockSpec((ch, C_QK_DIM), lambda h, c: (c, h)),
            pl.BlockSpec((ch, C_QK_DIM), lambda h, c: (c, C_HEADS + h)),
            pl.BlockSpec((ch, C_V_DIM), lambda h, c: (c, v0 + h)),
            pl.BlockSpec((ch, C_V_DIM), lambda h, c: (c, v0 + C_HEADS + h)),
            pl.BlockSpec((ch, C_QK_DIM), lambda h, c: (c, 0)),
            pl.BlockSpec((ch, C_QK_DIM), lambda h, c: (c, 0)),
            pl.BlockSpec((None, ch, ch), lambda h, c: (h, 0, 0)),
            pl.BlockSpec((None, ch, 1), lambda h, c: (h, 0, 0)),
            pl.BlockSpec((None, ch, 1), lambda h, c: (h, 0, 0)),
            pl.BlockSpec((None, 1, C_V_DIM), lambda h, c: (h, 0, 0)),
        ],
        out_specs=pl.BlockSpec((ch, C_V_DIM), lambda h, c: (c, h)),
        scratch_shapes=[pltpu.VMEM((C_QK_DIM, C_V_DIM), F32)],
        compiler_params=_params("arbitrary", "arbitrary"),
    )(r_all, r_all, r_all, r_all, cos, sin, dmask, qdec, kdec, cdec)


def _hybrid_mixer(u_hi, u_lo, w_in, q_norm, w_uq, w_qidx, kv_norm, w_ukv, kidx_norm,
                  conv_w, a_log, dt_bias, o_norm, w_out, *, top_k, pad, tm, tm_wide):
    tp = u_hi.shape[0]
    o = [0]
    for s in (A_Q_RANK, A_KV_RANK, IDX_DIM, IDX_HEADS, 3 * B_HEADS * B_HEAD_DIM,
              B_HEADS * B_HEAD_DIM, B_HEADS, B_HEADS):
        o.append(o[-1] + s)
    col = lambda i: w_in[:, o[i]:o[i + 1]]
    n_small = IDX_DIM + IDX_HEADS + 2 * B_HEADS
    w1 = jnp.concatenate([col(0), col(2), col(3), col(6), col(7),
                          jnp.zeros((w_in.shape[0], LANES - n_small), w_in.dtype)], axis=1)
    tn2 = 2 * MXU_DIM
    n2 = o[6] - o[4] + A_KV_RANK
    w2 = jnp.concatenate([col(4), col(5), col(1),
                          jnp.zeros((w_in.shape[0], -n2 % tn2), w_in.dtype)], axis=1).astype(BF16)
    p1t, small = _matmul("hy_in_x3", (u_hi, u_lo), _hilo(w1), F32, tm=tm, tn=w1.shape[1],
                         epilogue="transposed")
    p2 = _matmul("hy_in", (u_hi,), (w2,), F32, tm=tm_wide, tn=tn2)

    cqt_hi, cqt_lo = _rms_t(p1t, q_norm, width=A_Q_RANK, row_block=0, tc=tm, lo=True)
    up_rows = w_uq.shape[1]
    qi3t = _matmul("dsa_qidx", _hilo(w_qidx.T), (cqt_hi, cqt_lo), BF16, tm=up_rows, tn=tm,
                   epilogue="split3")
    q_scale = A_HEAD_DIM ** -0.5 * math.log2(math.e)
    qt = _matmul("dsa_q", ((w_uq.T * q_scale).astype(BF16),), (cqt_hi,), BF16, tm=up_rows, tn=tm)
    k3 = _kidx(small, kidx_norm, tm=tm)
    wt = p1t[A_Q_RANK + SM_WIDX:A_Q_RANK + SM_BETA]
    ckv_block = (o[6] - o[4]) // A_KV_RANK
    kvn, kvnt = _rms_both(p2, kv_norm, width=A_KV_RANK, col_block=ckv_block, tm=tm)
    w_kv = w_ukv.reshape(A_KV_RANK, A_HEADS, 2, A_HEAD_DIM)
    w_k = w_kv[:, :, 0].reshape(A_KV_RANK, A_HEADS * A_HEAD_DIM).astype(BF16)
    w_vt = w_kv[:, :, 1].reshape(A_KV_RANK, A_HEADS * A_HEAD_DIM).T.astype(BF16)
    k = _matmul("dsa_k", (kvn,), (w_k,), BF16, tm=tm, tn=up_rows)
    vt = _matmul("dsa_vt", (w_vt,), (kvnt,), BF16, tm=up_rows, tn=tm)
    o_a = _dsa(qt, qi3t, wt, k3, k, vt, top_k=top_k, pad=pad, tq=MXU_DIM, tk=tm)

    o_b = _gdn(p2, small, conv_w, a_log, dt_bias, o_norm, pad=pad)

    return _matmul_w32("hy_out", [o_a, o_b], w_out[None], 0, BF16, tm=tm_wide, tn=2 * MXU_DIM)


def _retention_mixer(u, w_in, w_out, *, pad, tm, tm_wide):
    tp = u.shape[0]
    half = C_QK_DIM // 2
    r_all = _matmul_w32("ret_in", [u], w_in[None], 0, BF16, tm=tm_wide, tn=4 * MXU_DIM)
    inv = 1.0 / (ROT_BASE ** jnp.linspace(0.0, 1.0, half, dtype=F32))
    posn = (jnp.arange(tp) - pad).astype(F32)
    ang = posn[:, None] * jnp.repeat(inv, 2)[None, :]
    sign = jnp.tile(jnp.array([-1.0, 1.0], F32), half)
    og = _retention(r_all, jnp.cos(ang), jnp.sin(ang) * sign[None, :])
    return _matmul_w32("ret_out", [og], w_out[None], 0, BF16, tm=tm_wide, tn=2 * MXU_DIM,
                       vmem=BIG_TILE_VMEM_LIMIT)


def _ffn(u, w_gate, w_up, w_down, layer, *, tm, tm_wide):
    act = _ffn_act(u, w_gate, w_up, layer, tm=tm_wide, tn=2 * MXU_DIM)
    return _matmul_w32("ffn_down", [act], w_down, layer, BF16, tm=tm, tn=2 * MXU_DIM,
                       vmem=BIG_TILE_VMEM_LIMIT)


def kernel(x, meta_tokens, mix_norm_pre, mix_norm_post, ffn_norm_pre, ffn_norm_post, hy_w_in, dsa_q_norm, dsa_w_uq, dsa_w_qidx, dsa_kv_norm, dsa_w_ukv, dsa_kidx_norm, gdn_conv_w, gdn_a_log, gdn_dt_bias, gdn_o_norm, hy_w_out, ret_w_in, ret_w_out, ffn_w_gate, ffn_w_up, ffn_w_down):
    b, seq, d = x.shape
    assert b == 1 and mix_norm_pre.shape[0] == 2
    pad = (-N_META) % ROW_ALIGN
    tp = pad + N_META + seq
    tm = ROW_TILE
    assert tp % tm == 0
    tm_wide = WIDE_ROW_TILE if tp % WIDE_ROW_TILE == 0 else tm
    top_k = min(TOPK_MAX, seq // TOPK_FRAC)
    h, u_hi, u_lo = _embed_norm(x[0], meta_tokens, mix_norm_pre[0], pad=pad)
    mix = _hybrid_mixer(u_hi, u_lo, hy_w_in[0], dsa_q_norm[0], dsa_w_uq[0], dsa_w_qidx[0],
                        dsa_kv_norm[0], dsa_w_ukv[0], dsa_kidx_norm[0], gdn_conv_w[0],
                        gdn_a_log[0], gdn_dt_bias[0], gdn_o_norm[0], hy_w_out[0],
                        top_k=top_k, pad=pad, tm=tm, tm_wide=tm_wide)
    h, u = _resid_norm(h, mix, mix_norm_post[0], ffn_norm_pre[0], tm=tm)
    f = _ffn(u, ffn_w_gate, ffn_w_up, ffn_w_down, 0, tm=tm, tm_wide=tm_wide)
    h, u = _resid_norm(h, f, ffn_norm_post[0], mix_norm_pre[1], tm=tm)
    mix = _retention_mixer(u, ret_w_in[0], ret_w_out[0], pad=pad, tm=tm, tm_wide=tm_wide)
    h, u = _resid_norm(h, mix, mix_norm_post[1], ffn_norm_pre[1], tm=tm)
    f = _ffn(u, ffn_w_gate, ffn_w_up, ffn_w_down, 1, tm=tm, tm_wide=tm_wide)
    out = _resid_norm(h, f, ffn_norm_post[1], None, tm=ROW_ALIGN, row0=pad + N_META,
                      out_dtype=x.dtype)
    return out[None]
```

```python
import functools
import math

import jax
import jax.numpy as jnp
from jax import lax
from jax.experimental import pallas as pl
from jax.experimental.pallas import tpu as pltpu

F32 = jnp.float32
BF16 = jnp.bfloat16

N_META = 16
NORM_EPS = 1e-6

A_HEADS = 8
A_HEAD_DIM = 128
A_Q_RANK = 512
A_KV_RANK = 256
IDX_HEADS = 16
IDX_DIM = 64
TOPK_MAX = 256
TOPK_FRAC = 4

B_HEADS = 8
B_HEAD_DIM = 128
CONV_K = 4
GDN_CHUNK = 128
DSA_HEADS_PER_STEP = 2
DSA_VMEM_LIMIT = 56 * 1024 * 1024
BIG_TILE_VMEM_LIMIT = 56 * 1024 * 1024
DSA_SEGMENTS = 3
DSA_SUB_BLOCK = 64

C_HEADS = 8
C_QK_DIM = 256
C_V_DIM = 512
ROT_BASE = 10000.0
RET_CHUNK = 384
RET_HEADS_PER_STEP = 2

LANES = 128
SUBLANES = 8
MXU_DIM = 256
ROW_ALIGN = MXU_DIM
ROW_TILE = 3 * MXU_DIM
WIDE_ROW_TILE = 11 * LANES
VMEM_LIMIT = 48 * 1024 * 1024

SM_WIDX = IDX_DIM
SM_BETA = SM_WIDX + IDX_HEADS
SM_A = SM_BETA + B_HEADS

INT_MIN = -2 ** 31
KEY_FLT_MAX = 0x7F7FFFFF
KEY_MIN_NORMAL = 0x00800000
KEY_FLT_LOWEST = (0xFF7FFFFF ^ 0x7FFFFFFF) - 2 ** 32


def _params(*sem, vmem=VMEM_LIMIT):
    return pltpu.CompilerParams(dimension_semantics=sem, vmem_limit_bytes=vmem)


def _hilo(x):
    hi = x.astype(BF16)
    return hi, (x - hi.astype(F32)).astype(BF16)


def _silu(x):
    return x * jax.nn.sigmoid(x)


def _mm_kernel(*refs, n_parts, epilogue):
    a, b, o = refs[:n_parts], refs[n_parts:2 * n_parts], refs[2 * n_parts:]
    a_hi, b_hi = a[0][...], b[0][...]
    acc = jnp.dot(a_hi, b_hi, preferred_element_type=F32)
    if n_parts == 2:
        acc += jnp.dot(a_hi, b[1][...], preferred_element_type=F32)
        acc += jnp.dot(a[1][...], b_hi, preferred_element_type=F32)
    if epilogue is None:
        o[0][...] = acc.astype(o[0].dtype)
    elif epilogue == "transposed":
        o[0][...] = acc.T
        o[1][...] = acc[:, acc.shape[1] - LANES:]
    elif epilogue == "split3":
        for g in range(acc.shape[0] // IDX_DIM):
            hi, lo = _hilo(acc[g * IDX_DIM:(g + 1) * IDX_DIM, :])
            base = g * MXU_DIM
            o[0][base:base + IDX_DIM, :] = hi
            o[0][base + IDX_DIM:base + 2 * IDX_DIM, :] = lo
            o[0][base + 2 * IDX_DIM:base + 3 * IDX_DIM, :] = hi
            o[0][base + 3 * IDX_DIM:base + MXU_DIM, :] = jnp.zeros_like(hi)


def _matmul(name, a_parts, b_parts, out_dtype, *, tm, tn, epilogue=None):
    m, kdim = a_parts[0].shape
    n = b_parts[0].shape[1]
    assert m % tm == 0 and n % tn == 0, (m, n, tm, tn)
    assert len(a_parts) == len(b_parts)
    if epilogue is None:
        out_shape = [jax.ShapeDtypeStruct((m, n), out_dtype)]
        out_specs = [pl.BlockSpec((tm, tn), lambda i, j: (i, j))]
    elif epilogue == "transposed":
        assert tn == n
        out_shape = [jax.ShapeDtypeStruct((n, m), out_dtype),
                     jax.ShapeDtypeStruct((m, LANES), out_dtype)]
        out_specs = [pl.BlockSpec((n, tm), lambda i, j: (0, i)),
                     pl.BlockSpec((tm, LANES), lambda i, j: (i, 0))]
    else:
        scale = MXU_DIM // IDX_DIM
        out_shape = [jax.ShapeDtypeStruct((m * scale, n), out_dtype)]
        out_specs = [pl.BlockSpec((tm * scale, tn), lambda i, j: (i, j))]
    out = pl.pallas_call(
        functools.partial(_mm_kernel, n_parts=len(a_parts), epilogue=epilogue),
        name=name,
        out_shape=out_shape,
        grid=(m // tm, n // tn),
        in_specs=[pl.BlockSpec((tm, kdim), lambda i, j: (i, 0))] * len(a_parts)
        + [pl.BlockSpec((kdim, tn), lambda i, j: (0, j))] * len(b_parts),
        out_specs=out_specs,
        compiler_params=_params("parallel", "parallel"),
    )(*a_parts, *b_parts)
    return out[0] if len(out) == 1 else out


def _mm_w32_kernel(*refs, n_a):
    a, w_ref, o_ref = refs[:n_a], refs[n_a], refs[n_a + 1]
    w = w_ref[...].astype(BF16)
    acc, k0 = None, 0
    for a_ref in a:
        k1 = k0 + a_ref.shape[1]
        part = jnp.dot(a_ref[...], w[k0:k1, :], preferred_element_type=F32)
        acc = part if acc is None else acc + part
        k0 = k1
    o_ref[...] = acc.astype(o_ref.dtype)


def _matmul_w32(name, a_list, w, layer, out_dtype, *, tm, tn, col0=0, n=None, vmem=VMEM_LIMIT):
    m = a_list[0].shape[0]
    kdim = sum(a.shape[1] for a in a_list)
    n = w.shape[2] - col0 if n is None else n
    assert w.shape[1] == kdim and m % tm == 0 and n % tn == 0 and col0 % tn == 0
    jb = col0 // tn
    return pl.pallas_call(
        functools.partial(_mm_w32_kernel, n_a=len(a_list)),
        name=name,
        out_shape=jax.ShapeDtypeStruct((m, n), out_dtype),
        grid=(m // tm, n // tn),
        in_specs=[pl.BlockSpec((tm, a.shape[1]), lambda i, j: (i, 0)) for a in a_list]
        + [pl.BlockSpec((None, kdim, tn), lambda i, j: (layer, 0, j + jb))],
        out_specs=pl.BlockSpec((tm, tn), lambda i, j: (i, j)),
        compiler_params=_params("parallel", "parallel", vmem=vmem),
    )(*a_list, w)


def _ffn_act_kernel(x_ref, wg_ref, wu_ref, o_ref):
    x = x_ref[...]
    g = jnp.dot(x, wg_ref[...].astype(BF16), preferred_element_type=F32)
    u = jnp.dot(x, wu_ref[...].astype(BF16), preferred_element_type=F32)
    o_ref[...] = (_silu(g) * u).astype(o_ref.dtype)


def _ffn_act(x, wg, wu, layer, *, tm, tn):
    m, kdim = x.shape
    n = wg.shape[2]
    assert m % tm == 0 and n % tn == 0
    w_spec = pl.BlockSpec((None, kdim, tn), lambda i, j: (layer, 0, j))
    return pl.pallas_call(
        _ffn_act_kernel,
        name="ffn_act",
        out_shape=jax.ShapeDtypeStruct((m, n), BF16),
        grid=(m // tm, n // tn),
        in_specs=[pl.BlockSpec((tm, kdim), lambda i, j: (i, 0)), w_spec, w_spec],
        out_specs=pl.BlockSpec((tm, tn), lambda i, j: (i, j)),
        compiler_params=_params("parallel", "parallel"),
    )(x, wg, wu)


def _rms_rows(x):
    return x * lax.rsqrt(jnp.mean(x * x, axis=-1, keepdims=True) + NORM_EPS)


def _embed_kernel(x_ref, head_ref, g_ref, h_ref, hi_ref, lo_ref):
    h = jnp.where(pl.program_id(0) == 0, head_ref[...], x_ref[...].astype(F32))
    h_ref[...] = h
    y = _rms_rows(h) * g_ref[...]
    hi = y.astype(BF16)
    hi_ref[...] = hi
    lo_ref[...] = (y - hi.astype(F32)).astype(BF16)


def _embed_norm(x, meta, gain, *, pad):
    seq, d = x.shape
    tb = pad + meta.shape[0]
    assert seq % tb == 0
    head = jnp.concatenate([jnp.zeros((pad, d), F32), meta.astype(F32)], axis=0)
    tp = tb + seq
    row = pl.BlockSpec((tb, d), lambda i: (i, 0))
    return pl.pallas_call(
        _embed_kernel,
        name="embed_norm",
        out_shape=[jax.ShapeDtypeStruct((tp, d), F32)] + [jax.ShapeDtypeStruct((tp, d), BF16)] * 2,
        grid=(tp // tb,),
        in_specs=[pl.BlockSpec((tb, d), lambda i: (jnp.maximum(i - 1, 0), 0)),
                  pl.BlockSpec((tb, d), lambda i: (0, 0)),
                  pl.BlockSpec((1, d), lambda i: (0, 0))],
        out_specs=[row, row, row],
        compiler_params=_params("parallel"),
    )(x, head, gain.reshape(1, d).astype(F32))


def _rms_both_kernel(x_ref, g_ref, o_ref, ot_ref):
    y = _rms_rows(x_ref[...]) * g_ref[...]
    o_ref[...] = y.astype(BF16)
    ot_ref[...] = y.T.astype(BF16)


def _rms_both(x, gain, *, width, col_block, tm):
    m = x.shape[0]
    return pl.pallas_call(
        _rms_both_kernel,
        name="rms_both",
        out_shape=[jax.ShapeDtypeStruct((m, width), BF16), jax.ShapeDtypeStruct((width, m), BF16)],
        grid=(m // tm,),
        in_specs=[pl.BlockSpec((tm, width), lambda i: (i, col_block)),
                  pl.BlockSpec((1, width), lambda i: (0, 0))],
        out_specs=[pl.BlockSpec((tm, width), lambda i: (i, 0)),
                   pl.BlockSpec((width, tm), lambda i: (0, i))],
        compiler_params=_params("parallel"),
    )(x, gain.reshape(1, width).astype(F32))


def _rms_t_kernel(x_ref, g_ref, *o_refs):
    x = x_ref[...]
    y = x * lax.rsqrt(jnp.mean(x * x, axis=0, keepdims=True) + NORM_EPS) * g_ref[...]
    hi = y.astype(BF16)
    o_refs[0][...] = hi
    if len(o_refs) == 2:
        o_refs[1][...] = (y - hi.astype(F32)).astype(BF16)


def _rms_t(xt, gain, *, width, row_block, tc, lo=False):
    t = xt.shape[1]
    n_out = 2 if lo else 1
    out = pl.pallas_call(
        _rms_t_kernel,
        name="rms_t",
        out_shape=[jax.ShapeDtypeStruct((width, t), BF16)] * n_out,
        grid=(t // tc,),
        in_specs=[pl.BlockSpec((width, tc), lambda i: (row_block, i)),
                  pl.BlockSpec((width, 1), lambda i: (0, 0))],
        out_specs=[pl.BlockSpec((width, tc), lambda i: (0, i))] * n_out,
        compiler_params=_params("parallel"),
    )(xt, gain.reshape(width, 1).astype(F32))
    return out if lo else out[0]


def _resid_kernel(h_ref, u_ref, gp_ref, *rest, with_next):
    hn = h_ref[...] + _rms_rows(u_ref[...].astype(F32)) * gp_ref[...]
    if with_next:
        gn_ref, hn_ref, un_ref = rest
        un_ref[...] = (_rms_rows(hn) * gn_ref[...]).astype(BF16)
    else:
        (hn_ref,) = rest
    hn_ref[...] = hn.astype(hn_ref.dtype)


def _resid_norm(h, u, g_post, g_next, *, tm, row0=0, out_dtype=F32):
    m, d = h.shape
    assert row0 % tm == 0 and m % tm == 0
    b0 = row0 // tm
    row_in = pl.BlockSpec((tm, d), lambda i: (i + b0, 0))
    row_out = pl.BlockSpec((tm, d), lambda i: (i, 0))
    vec = pl.BlockSpec((1, d), lambda i: (0, 0))
    with_next = g_next is not None
    gains = [g_post.reshape(1, d).astype(F32)]
    out_shape = [jax.ShapeDtypeStruct((m - row0, d), out_dtype)]
    if with_next:
        gains.append(g_next.reshape(1, d).astype(F32))
        out_shape.append(jax.ShapeDtypeStruct((m - row0, d), BF16))
    out = pl.pallas_call(
        functools.partial(_resid_kernel, with_next=with_next),
        name="resid_norm",
        out_shape=out_shape,
        grid=((m - row0) // tm,),
        in_specs=[row_in, row_in] + [vec] * len(gains),
        out_specs=[row_out] * len(out_shape),
        compiler_params=_params("parallel"),
    )(h, u, *gains)
    return out if with_next else out[0]


def _kidx_kernel(sm_ref, g_ref, o_ref):
    sm = sm_ref[...]
    lane = lax.broadcasted_iota(jnp.int32, sm.shape, 1)
    x = jnp.where(lane < IDX_DIM, sm, 0.0)
    ms = jnp.sum(x * x, axis=-1, keepdims=True) * (1.0 / IDX_DIM)
    y = x * lax.rsqrt(ms + NORM_EPS) * g_ref[...]
    hi = y.astype(BF16)
    lo = (y - hi.astype(F32)).astype(BF16)
    y2 = y + pltpu.roll(y, IDX_DIM, axis=1)
    o_ref[:, 0:LANES] = y2.astype(BF16)
    o_ref[:, LANES:2 * LANES] = lo


def _kidx(p1, gain, *, tm):
    m = p1.shape[0]
    g = jnp.zeros((1, LANES), F32).at[0, :IDX_DIM].set(gain.astype(F32))
    return pl.pallas_call(
        _kidx_kernel,
        name="dsa_kidx",
        out_shape=jax.ShapeDtypeStruct((m, 2 * LANES), BF16),
        grid=(m // tm,),
        in_specs=[pl.BlockSpec((tm, LANES), lambda i: (i, 0)),
                  pl.BlockSpec((1, LANES), lambda i: (0, 0))],
        out_specs=pl.BlockSpec((tm, 2 * LANES), lambda i: (i, 0)),
        compiler_params=_params("parallel"),
    )(p1, g)


def _dsa_kernel(qt_ref, qi3t_ref, wt_ref, k3_ref, k_ref, vt_ref, o_ref, sc_scr, sc16_scr, acc_scr,
                sa_scr, sb_scr, pa_scr, pb_scr, *, tq, tk, sb, qb0, top_k, pad, hps, w_scale,
                pos_bits):
    qi = pl.program_id(0)
    hd = pl.program_id(1)
    q0 = (qi + qb0) * tq
    nkt = (q0 + tq + tk - 1) // tk

    krel = lax.broadcasted_iota(jnp.int32, (tk, tq), 0)

    @pl.when(hd == 0)
    def _select():
        qpos = q0 + lax.broadcasted_iota(jnp.int32, (tk, tq), 1)

        def score_tile(kt, carry):
            ks = pl.multiple_of(kt * tk, tk)
            k3 = k3_ref[pl.ds(ks, tk), :]
            acc = jnp.zeros((tk, tq), F32)
            for ih in range(IDX_HEADS):
                d = jnp.dot(k3, qi3t_ref[ih * MXU_DIM:(ih + 1) * MXU_DIM, :],
                            preferred_element_type=F32)
                acc = acc + (wt_ref[ih:ih + 1, :] * w_scale) * jnp.maximum(d, 0.0)
            kpos = ks + krel
            ok = (kpos <= qpos) & (kpos >= pad)
            masked = jnp.where(ok, acc, -jnp.inf)
            sc_scr[pl.ds(ks, tk), :] = masked
            sc16_scr[pl.ds(ks, tk), :] = masked.astype(BF16)
            return carry

        lax.fori_loop(0, nkt, score_tile, 0)

        def count(pred):
            def body(kt, acc):
                ks = pl.multiple_of(kt * tk, tk)
                hit = pred(sc_scr[pl.ds(ks, tk), :], ks).astype(jnp.int32)
                return acc + jnp.sum(hit.reshape(tk // SUBLANES, SUBLANES, tq), axis=0)

            acc = lax.fori_loop(0, nkt, body, jnp.zeros((SUBLANES, tq), jnp.int32))
            return jnp.sum(acc, axis=0, keepdims=True)

        def key_to_f32(t):
            t = jnp.clip(t, jnp.int32(KEY_FLT_LOWEST), jnp.int32(KEY_FLT_MAX))
            return pltpu.bitcast(jnp.where(t >= 0, t, t ^ jnp.int32(0x7FFFFFFF)), F32)

        def count16(cand_b):
            rows = 2 * SUBLANES
            one, nil = jnp.ones((), BF16), jnp.zeros((), BF16)
            cand_b = jnp.maximum(cand_b, jnp.finfo(BF16).min)

            def body(kt, acc):
                ks = pl.multiple_of(kt * tk, tk)
                hit = jnp.where(sc16_scr[pl.ds(ks, tk), :] >= cand_b, one, nil)
                slabs = [hit[r:r + rows, :] for r in range(0, tk, rows)]
                while len(slabs) > 1:
                    pairs = [a + b for a, b in zip(slabs[0::2], slabs[1::2])]
                    slabs = pairs + slabs[2 * len(pairs):]
                return acc + slabs[0].astype(F32)

            acc = lax.fori_loop(0, nkt, body, jnp.zeros((rows, tq), F32))
            return jnp.sum(acc, axis=0, keepdims=True).astype(jnp.int32)

        zero = jnp.zeros((1, tq), jnp.int32)
        accept = lambda n, cand: (n >= top_k) & (cand <= jnp.int32(KEY_FLT_MAX))
        n0 = count16(jnp.zeros((1, tq), BF16))
        t_hi = jnp.where(n0 >= top_k, zero, jnp.int32(INT_MIN))

        def hi_body(i, t):
            cand = t + lax.shift_left(jnp.int32(1), 30 - i)
            n = count16(key_to_f32(cand).astype(BF16))
            return jnp.where(accept(n, cand), cand, t)

        t_hi = lax.fori_loop(0, 15, hi_body, t_hi)

        base = t_hi - jnp.int32(1 << 15)

        def lo_body(i, state):
            off, n_at = state
            bit = lax.shift_left(jnp.int32(1), 16 - i)
            cand = base + (off | bit)
            cand_f = key_to_f32(cand)
            n = count(lambda s, ks: s >= cand_f)
            ok = accept(n, cand)
            return jnp.where(ok, off | bit, off), jnp.where(ok, n, n_at)

        off, n_at = lax.fori_loop(0, 17, lo_body, (zero, zero + (top_k + 1)))
        few = t_hi == jnp.int32(INT_MIN)
        t = jnp.where(few, t_hi, base + off)
        thr = key_to_f32(t)

        n_ge = jnp.where(few, zero, n_at)

        @pl.when(jnp.max(n_ge) > top_k)
        def _ties():
            flushed = (t >= jnp.int32(-KEY_MIN_NORMAL)) & (t < jnp.int32(KEY_MIN_NORMAL))
            thr_next = key_to_f32(jnp.where(flushed, jnp.int32(KEY_MIN_NORMAL), t + 1))
            tie = lambda s: (s >= thr) & jnp.logical_not(s >= thr_next)
            want = top_k - count(lambda s, ks: s >= thr_next)

            def pos_body(i, cut):
                cand = cut + lax.shift_left(jnp.int32(1), pos_bits - 1 - i)
                n = count(lambda s, ks: tie(s) & (ks + krel < cand))
                return jnp.where(n < want, cand, cut)

            cut = lax.fori_loop(0, pos_bits, pos_body, zero)

            def drop_tile(kt, carry):
                ks = pl.multiple_of(kt * tk, tk)
                s = sc_scr[pl.ds(ks, tk), :]
                sc_scr[pl.ds(ks, tk), :] = jnp.where(tie(s) & (ks + krel > cut), -jnp.inf, s)
                return carry

            lax.fori_loop(0, nkt, drop_tile, 0)

        def bias_tile(kt, carry):
            ks = pl.multiple_of(kt * tk, tk)
            sc_scr[pl.ds(ks, tk), :] = jnp.where(sc_scr[pl.ds(ks, tk), :] >= thr, 0.0, -jnp.inf)
            return carry

        lax.fori_loop(0, nkt, bias_tile, 0)

    hd_sl = [slice(j * A_HEAD_DIM, (j + 1) * A_HEAD_DIM) for j in range(hps)]
    q = [qt_ref[sl, :] for sl in hd_sl]

    def logits_into(kt, s_ref):
        ks = pl.multiple_of(kt * tk, tk)
        for j in range(hps):
            s_ref[j] = jnp.dot(k_ref[pl.ds(ks, tk), hd_sl[j]], q[j], preferred_element_type=F32)

    fold = lambda x: x.reshape(sb // SUBLANES, SUBLANES, tq)

    def att_tile(kt, kt_next, bias_row, s_cur, s_next, p_ref, carry):
        if s_next is not None:
            logits_into(kt_next, s_next)
        ks = pl.multiple_of(kt * tk, tk)

        def masked(j, i):
            r = i * sb
            return s_cur[j, r:r + sb, :] + sc_scr[pl.ds(bias_row + r, sb), :]

        out, alphas = [], []
        for j in range(hps):
            m, l = carry[j]
            mx = jnp.full((SUBLANES, tq), -jnp.inf, F32)
            for i in range(tk // sb):
                mx = jnp.maximum(mx, jnp.max(fold(masked(j, i)), axis=0))
            m_new = jnp.maximum(m, jnp.max(mx, axis=0, keepdims=True))
            m_safe = jnp.where(m_new == -jnp.inf, 0.0, m_new)
            for i in range(tk // sb):
                p = jnp.exp2(masked(j, i) - m_safe)
                p_ref[j, i * sb:(i + 1) * sb, :] = p.astype(BF16)
            alphas.append(jnp.exp2(m - m_safe))
            out.append(m_new)
        ones = jnp.ones((2 * SUBLANES, tk), BF16)
        pv = [jnp.dot(jnp.concatenate([vt_ref[hd_sl[j], pl.ds(ks, tk)], ones], axis=0), p_ref[j],
                      preferred_element_type=F32) for j in range(hps)]
        for j in range(hps):
            acc_scr[j] = alphas[j] * acc_scr[j] + pv[j][:A_HEAD_DIM, :]
        return tuple((out[j], alphas[j] * carry[j][1] + pv[j][A_HEAD_DIM:A_HEAD_DIM + 1, :])
                     for j in range(hps))

    def att_pair(i, carry):
        kt0 = 2 * i
        kt2 = jnp.minimum(kt0 + 2, nkt - 1)
        row = lambda kt: pl.multiple_of(kt * tk, tk)
        carry = att_tile(kt0, kt0 + 1, row(kt0), sa_scr, sb_scr, pa_scr, carry)
        return att_tile(kt0 + 1, kt2, row(kt0 + 1), sb_scr, sa_scr, pb_scr, carry)

    acc_scr[...] = jnp.zeros_like(acc_scr)
    logits_into(0, sa_scr)
    init = (jnp.full((1, tq), -jnp.inf, F32), jnp.zeros((1, tq), F32))
    res = lax.fori_loop(0, nkt // 2, att_pair, (init,) * hps)
    last = nkt - 1
    res = lax.cond(
        nkt % 2 == 1,
        lambda c: att_tile(last, None, pl.multiple_of(last * tk, tk), sa_scr, None, pa_scr, c),
        lambda c: c, res)
    for j in range(hps):
        l = res[j][1]
        o_ref[:, hd_sl[j]] = jnp.where(l > 0.0, acc_scr[j] / l, 0.0).T.astype(o_ref.dtype)


def _dsa(qt, qi3t, wt, k3, k, vt, *, top_k, pad, tq, tk):
    tp = k.shape[0]
    assert tp % tq == 0 and tp % tk == 0 and tk >= top_k
    hps = DSA_HEADS_PER_STEP
    hw = hps * A_HEAD_DIM
    n_tiles = tp // tk
    outs = []
    for seg in range(DSA_SEGMENTS):
        t0, t1 = seg * n_tiles // DSA_SEGMENTS, (seg + 1) * n_tiles // DSA_SEGMENTS
        if t1 == t0:
            continue
        qb0, nq, nk = t0 * tk // tq, (t1 - t0) * tk // tq, t1 * tk
        kern = functools.partial(
            _dsa_kernel, tq=tq, tk=tk, sb=DSA_SUB_BLOCK, qb0=qb0, top_k=top_k, pad=pad, hps=hps,
            w_scale=(IDX_HEADS * IDX_DIM) ** -0.5, pos_bits=tp.bit_length())
        outs.append(pl.pallas_call(
            kern,
            name="dsa",
            out_shape=jax.ShapeDtypeStruct((nq * tq, A_HEADS * A_HEAD_DIM), BF16),
            grid=(nq, A_HEADS // hps),
            in_specs=[
                pl.BlockSpec((hw, tq), lambda i, h, qb0=qb0: (h, i + qb0)),
                pl.BlockSpec((IDX_HEADS * MXU_DIM, tq), lambda i, h, qb0=qb0: (0, i + qb0)),
                pl.BlockSpec((IDX_HEADS, tq), lambda i, h, qb0=qb0: (0, i + qb0)),
                pl.BlockSpec((nk, MXU_DIM), lambda i, h: (0, 0)),
                pl.BlockSpec((nk, hw), lambda i, h: (0, h)),
                pl.BlockSpec((hw, nk), lambda i, h: (h, 0)),
            ],
            out_specs=pl.BlockSpec((tq, hw), lambda i, h: (i, h)),
            scratch_shapes=[pltpu.VMEM((nk, tq), F32), pltpu.VMEM((nk, tq), BF16),
                            pltpu.VMEM((hps, A_HEAD_DIM, tq), F32)]
            + [pltpu.VMEM((hps, tk, tq), F32)] * 2 + [pltpu.VMEM((hps, tk, tq), BF16)] * 2,
            compiler_params=_params("arbitrary", "arbitrary", vmem=DSA_VMEM_LIMIT),
        )(qt, qi3t, wt, k3, k, vt))
    return outs[0] if len(outs) == 1 else jnp.concatenate(outs, axis=0)


def _dot3(a, b, dims=(((1,), (0,)), ((), ()))):
    ah, al = _hilo(a)
    bh, bl = _hilo(b)
    f = lambda x, y: lax.dot_general(x, y, dims, preferred_element_type=F32)
    return f(ah, bh) + f(ah, bl) + f(al, bh)


_NN = (((1,), (0,)), ((), ()))
_NT = (((1,), (1,)), ((), ()))
_TN = (((0,), (0,)), ((), ()))


def _mm(a, b, dims=_NN):
    return lax.dot_general(a.astype(BF16), b.astype(BF16), dims, preferred_element_type=F32)


def _gdn_kernel(qkv_ref, z_ref, sm_ref, convw_ref, alog_ref, dtb_ref, onorm_ref, o_ref,
                xbuf, s_scr, *, pad):
    c = pl.program_id(0)
    ch = GDN_CHUNK
    hist = SUBLANES

    @pl.when(c == 0)
    def _():
        xbuf[0:hist, :] = jnp.zeros((hist, xbuf.shape[1]), F32)
        s_scr[...] = jnp.zeros_like(s_scr)

    x = qkv_ref[...]
    xbuf[hist:hist + ch, :] = x
    w = convw_ref[...]
    y = x * w[CONV_K - 1:CONV_K, :]
    for j in range(CONV_K - 1):
        y = y + xbuf[hist - (CONV_K - 1) + j:hist - (CONV_K - 1) + j + ch, :] * w[j:j + 1, :]
    xbuf[0:hist, :] = x[ch - hist:ch, :]
    y = _silu(y)

    sm = sm_ref[...]
    row = c * ch + lax.broadcasted_iota(jnp.int32, sm.shape, 0)
    beta = jax.nn.sigmoid(sm)
    zsm = sm + dtb_ref[...]
    softplus = jnp.maximum(zsm, 0.0) + jnp.log(1.0 + jnp.exp(-jnp.abs(zsm)))
    g = jnp.where(row >= pad, -jnp.exp(alog_ref[...]) * softplus, 0.0)

    ri = lax.broadcasted_iota(jnp.int32, (ch, ch), 0)
    ci = lax.broadcasted_iota(jnp.int32, (ch, ch), 1)
    lower = ri >= ci
    strict = ri > ci
    tri = jnp.where(lower, 1.0, 0.0).astype(BF16)
    g1 = g.astype(BF16)
    r1 = g - g1.astype(F32)
    g2 = r1.astype(BF16)
    g3 = (r1 - g2.astype(F32)).astype(BF16)
    gcum = (jnp.dot(tri, g1, preferred_element_type=F32)
            + jnp.dot(tri, g2, preferred_element_type=F32)
            + jnp.dot(tri, g3, preferred_element_type=F32))
    gcum_t = gcum.T
    eye = jnp.where(ri == ci, 1.0, 0.0).astype(F32)

    bw = B_HEADS * B_HEAD_DIM
    hs = range(B_HEADS)
    sl = [slice(h * B_HEAD_DIM, (h + 1) * B_HEAD_DIM) for h in hs]
    part = lambda j: [y[:, j * bw + h * B_HEAD_DIM:j * bw + (h + 1) * B_HEAD_DIM] for h in hs]
    l2n = lambda x: x * lax.rsqrt(jnp.sum(x * x, axis=-1, keepdims=True) + NORM_EPS)
    q = [l2n(x) * B_HEAD_DIM ** -0.5 for x in part(0)]
    k = [l2n(x) for x in part(1)]
    v = part(2)
    b = [beta[:, SM_BETA + h:SM_BETA + h + 1] for h in hs]
    gc = [gcum[:, SM_A + h:SM_A + h + 1] for h in hs]
    gr = [gcum_t[SM_A + h:SM_A + h + 1, :] for h in hs]
    gl = [x[ch - 1:ch, :] for x in gc]
    eg = [jnp.exp(x) for x in gc]
    decay = [jnp.where(lower, jnp.exp(jnp.where(lower, gc[h] - gr[h], 0.0)), 0.0) for h in hs]
    kb = [k[h] * b[h] for h in hs]
    a_mat = [jnp.where(strict, _mm(kb[h], k[h], _NT) * decay[h], 0.0) for h in hs]
    t_mat = [eye - a for a in a_mat]
    pw = a_mat
    for _ in range(ch.bit_length() - 2):
        pw = [_dot3(p, p) for p in pw]
        t_mat = [t_mat[h] + _dot3(t_mat[h], pw[h]) for h in hs]
    u = [_mm(t_mat[h], v[h] * b[h]) for h in hs]
    wm = [_mm(t_mat[h], kb[h] * eg[h]) for h in hs]
    qk = [jnp.where(lower, _mm(q[h], k[h], _NT) * decay[h], 0.0) for h in hs]
    state = [s_scr[h] for h in hs]
    v_new = [u[h] - _mm(wm[h], state[h]) for h in hs]
    o = [_mm(q[h] * eg[h], state[h]) + _mm(qk[h], v_new[h]) for h in hs]
    kd = [k[h] * jnp.exp(gl[h] - gc[h]) for h in hs]
    new_state = [state[h] * jnp.exp(gl[h]) + _mm(kd[h], v_new[h], _TN) for h in hs]
    for h in hs:
        s_scr[h] = new_state[h]
    for h in hs:
        o_h = _rms_rows(o[h]) * onorm_ref[...] * _silu(z_ref[:, sl[h]])
        o_ref[:, sl[h]] = o_h.astype(o_ref.dtype)


def _gdn(p2, p1, conv_w, a_log, dt_bias, o_norm, *, pad):
    tp = p2.shape[0]
    bw = B_HEADS * B_HEAD_DIM
    alog = jnp.zeros((1, LANES), F32).at[0, SM_A:SM_A + B_HEADS].set(a_log.astype(F32))
    dtb = jnp.zeros((1, LANES), F32).at[0, SM_A:SM_A + B_HEADS].set(dt_bias.astype(F32))
    ch = GDN_CHUNK
    return pl.pallas_call(
        functools.partial(_gdn_kernel, pad=pad),
        name="gdn",
        out_shape=jax.ShapeDtypeStruct((tp, bw), BF16),
        grid=(tp // ch,),
        in_specs=[
            pl.BlockSpec((ch, 3 * bw), lambda c: (c, 0)),
            pl.BlockSpec((ch, bw), lambda c: (c, 3)),
            pl.BlockSpec((ch, LANES), lambda c: (c, 0)),
            pl.BlockSpec((CONV_K, 3 * bw), lambda c: (0, 0)),
            pl.BlockSpec((1, LANES), lambda c: (0, 0)),
            pl.BlockSpec((1, LANES), lambda c: (0, 0)),
            pl.BlockSpec((1, B_HEAD_DIM), lambda c: (0, 0)),
        ],
        out_specs=pl.BlockSpec((ch, bw), lambda c: (c, 0)),
        scratch_shapes=[pltpu.VMEM((SUBLANES + ch, 3 * bw), F32),
                        pltpu.VMEM((B_HEADS, B_HEAD_DIM, B_HEAD_DIM), F32)],
        compiler_params=_params("arbitrary"),
    )(p2, p2, p1, conv_w.astype(F32), alog, dtb, o_norm.reshape(1, B_HEAD_DIM).astype(F32))


def _ret_kernel(q_ref, k_ref, v_ref, g_ref, cos_ref, sin_ref, dmask_ref, qdec_ref, kdec_ref,
                cdec_ref, o_ref, r_scr):
    @pl.when(pl.program_id(1) == 0)
    def _():
        r_scr[...] = jnp.zeros_like(r_scr)

    cos, sin = cos_ref[...], sin_ref[...]
    even = lax.broadcasted_iota(jnp.int32, cos.shape, 1) % 2 == 0

    def rot(x):
        partner = jnp.where(even, pltpu.roll(x, C_QK_DIM - 1, axis=1), pltpu.roll(x, 1, axis=1))
        return x * cos + partner * sin

    hs = range(RET_HEADS_PER_STEP)
    qk = lambda ref, j: ref[:, j * C_QK_DIM:(j + 1) * C_QK_DIM].astype(F32)
    vsl = [slice(j * C_V_DIM, (j + 1) * C_V_DIM) for j in hs]
    q = [rot(qk(q_ref, j)) for j in hs]
    k = [rot(qk(k_ref, j)) * C_QK_DIM ** -0.5 for j in hs]
    v = [v_ref[:, vsl[j]] for j in hs]
    s = [lax.dot_general(q[j].astype(BF16), k[j].astype(BF16), _NT, preferred_element_type=F32)
         * dmask_ref[j] for j in hs]
    r = [r_scr[j] for j in hs]
    o = [jnp.dot(s[j].astype(BF16), v[j], preferred_element_type=F32)
         + jnp.dot((q[j] * qdec_ref[j]).astype(BF16), r[j].astype(BF16),
                   preferred_element_type=F32) for j in hs]
    kd = [(k[j] * kdec_ref[j]).astype(BF16) for j in hs]
    r_new = [r[j] * cdec_ref[j] + lax.dot_general(kd[j], v[j], _TN, preferred_element_type=F32)
             for j in hs]
    for j in hs:
        r_scr[j] = r_new[j]
    for j in hs:
        gate = g_ref[:, vsl[j]].astype(F32)
        o_ref[:, vsl[j]] = (_silu(gate) * _rms_rows(o[j])).astype(o_ref.dtype)


def _retention(r_all, cos, sin):
    tp = r_all.shape[0]
    ch = RET_CHUNK
    hp = RET_HEADS_PER_STEP
    ng = C_HEADS // hp
    v0 = 2 * C_HEADS * C_QK_DIM // (hp * C_V_DIM)
    log_gamma = jnp.log(1.0 - 2.0 ** (-5.0 - jnp.arange(C_HEADS, dtype=F32)))
    pos = jnp.arange(ch, dtype=F32)
    rel = pos[:, None] - pos[None, :]
    dmask = jnp.where(rel >= 0, jnp.exp(jnp.maximum(rel, 0.0)[None] * log_gamma[:, None, None]), 0.0)
    qdec = jnp.exp((pos + 1.0)[None, :] * log_gamma[:, None])[:, :, None]
    kdec = jnp.exp((ch - 1.0 - pos)[None, :] * log_gamma[:, None])[:, :, None]
    cdec = jnp.broadcast_to(jnp.exp(ch * log_gamma)[:, None, None], (C_HEADS, 1, C_V_DIM))
    return pl.pallas_call(
        _ret_kernel,
        name="retention",
        out_shape=jax.ShapeDtypeStruct((tp, C_HEADS * C_V_DIM), BF16),
        grid=(ng, tp // ch),
        in_specs=[
            pl.BlockSpec((ch, hp * C_QK_DIM), lambda h, c: (c, h)),
            pl.BlockSpec((ch, hp * C_QK_DIM), lambda h, c: (c, ng + h)),
            pl.BlockSpec((ch, hp * C_V_DIM), lambda h, c: (c, v0 + h)),
            pl.BlockSpec((ch, hp * C_V_DIM), lambda h, c: (c, v0 + ng + h)),
            pl.BlockSpec((ch, C_QK_DIM), lambda h, c: (c, 0)),
            pl.BlockSpec((ch, C_QK_DIM), lambda h, c: (c, 0)),
            pl.BlockSpec((hp, ch, ch), lambda h, c: (h, 0, 0)),
            pl.BlockSpec((hp, ch, 1), lambda h, c: (h, 0, 0)),
            pl.BlockSpec((hp, ch, 1), lambda h, c: (h, 0, 0)),
            pl.BlockSpec((hp, 1, C_V_DIM), lambda h, c: (h, 0, 0)),
        ],
        out_specs=pl.BlockSpec((ch, hp * C_V_DIM), lambda h, c: (c, h)),
        scratch_shapes=[pltpu.VMEM((hp, C_QK_DIM, C_V_DIM), F32)],
        compiler_params=_params("arbitrary", "arbitrary"),
    )(r_all, r_all, r_all, r_all, cos, sin, dmask, qdec, kdec, cdec)


def _hybrid_mixer(u_hi, u_lo, w_in, q_norm, w_uq, w_qidx, kv_norm, w_ukv, kidx_norm,
                  conv_w, a_log, dt_bias, o_norm, w_out, *, top_k, pad, tm, tm_wide):
    tp = u_hi.shape[0]
    o = [0]
    for s in (A_Q_RANK, A_KV_RANK, IDX_DIM, IDX_HEADS, 3 * B_HEADS * B_HEAD_DIM,
              B_HEADS * B_HEAD_DIM, B_HEADS, B_HEADS):
        o.append(o[-1] + s)
    col = lambda i: w_in[:, o[i]:o[i + 1]]
    n_small = IDX_DIM + IDX_HEADS + 2 * B_HEADS
    w1 = jnp.concatenate([col(0), col(2), col(3), col(6), col(7),
                          jnp.zeros((w_in.shape[0], LANES - n_small), w_in.dtype)], axis=1)
    tn2 = 2 * MXU_DIM
    n2 = o[6] - o[4] + A_KV_RANK
    w2 = jnp.concatenate([col(4), col(5), col(1),
                          jnp.zeros((w_in.shape[0], -n2 % tn2), w_in.dtype)], axis=1).astype(BF16)
    p1t, small = _matmul("hy_in_x3", (u_hi, u_lo), _hilo(w1), F32, tm=tm, tn=w1.shape[1],
                         epilogue="transposed")
    p2 = _matmul("hy_in", (u_hi,), (w2,), F32, tm=tm_wide, tn=tn2)

    cqt_hi, cqt_lo = _rms_t(p1t, q_norm, width=A_Q_RANK, row_block=0, tc=tm, lo=True)
    up_rows = w_uq.shape[1]
    qi3t = _matmul("dsa_qidx", _hilo(w_qidx.T), (cqt_hi, cqt_lo), BF16, tm=up_rows, tn=tm,
                   epilogue="split3")
    q_scale = A_HEAD_DIM ** -0.5 * math.log2(math.e)
    qt = _matmul("dsa_q", ((w_uq.T * q_scale).astype(BF16),), (cqt_hi,), BF16, tm=up_rows, tn=tm)
    k3 = _kidx(small, kidx_norm, tm=tm)
    wt = p1t[A_Q_RANK + SM_WIDX:A_Q_RANK + SM_BETA]
    ckv_block = (o[6] - o[4]) // A_KV_RANK
    kvn, kvnt = _rms_both(p2, kv_norm, width=A_KV_RANK, col_block=ckv_block, tm=tm)
    w_kv = w_ukv.reshape(A_KV_RANK, A_HEADS, 2, A_HEAD_DIM)
    w_k = w_kv[:, :, 0].reshape(A_KV_RANK, A_HEADS * A_HEAD_DIM).astype(BF16)
    w_vt = w_kv[:, :, 1].reshape(A_KV_RANK, A_HEADS * A_HEAD_DIM).T.astype(BF16)
    k = _matmul("dsa_k", (kvn,), (w_k,), BF16, tm=tm, tn=up_rows)
    vt = _matmul("dsa_vt", (w_vt,), (kvnt,), BF16, tm=up_rows, tn=tm)
    o_a = _dsa(qt, qi3t, wt, k3, k, vt, top_k=top_k, pad=pad, tq=MXU_DIM, tk=tm)

    o_b = _gdn(p2, small, conv_w, a_log, dt_bias, o_norm, pad=pad)

    return _matmul_w32("hy_out", [o_a, o_b], w_out[None], 0, BF16, tm=tm_wide, tn=2 * MXU_DIM)


def _retention_mixer(u, w_in, w_out, *, pad, tm, tm_wide):
    tp = u.shape[0]
    half = C_QK_DIM // 2
    r_all = _matmul_w32("ret_in", [u], w_in[None], 0, BF16, tm=tm_wide, tn=4 * MXU_DIM)
    inv = 1.0 / (ROT_BASE ** jnp.linspace(0.0, 1.0, half, dtype=F32))
    posn = (jnp.arange(tp) - pad).astype(F32)
    ang = posn[:, None] * jnp.repeat(inv, 2)[None, :]
    sign = jnp.tile(jnp.array([-1.0, 1.0], F32), half)
    og = _retention(r_all, jnp.cos(ang), jnp.sin(ang) * sign[None, :])
    return _matmul_w32("ret_out", [og], w_out[None], 0, BF16, tm=tm_wide, tn=2 * MXU_DIM,
                       vmem=BIG_TILE_VMEM_LIMIT)


def _ffn(u, w_gate, w_up, w_down, layer, *, tm, tm_wide):
    act = _ffn_act(u, w_gate, w_up, layer, tm=tm_wide, tn=2 * MXU_DIM)
    return _matmul_w32("ffn_down", [act], w_down, layer, BF16, tm=tm, tn=2 * MXU_DIM,
                       vmem=BIG_TILE_VMEM_LIMIT)


def kernel(x, meta_tokens, mix_norm_pre, mix_norm_post, ffn_norm_pre, ffn_norm_post, hy_w_in, dsa_q_norm, dsa_w_uq, dsa_w_qidx, dsa_kv_norm, dsa_w_ukv, dsa_kidx_norm, gdn_conv_w, gdn_a_log, gdn_dt_bias, gdn_o_norm, hy_w_out, ret_w_in, ret_w_out, ffn_w_gate, ffn_w_up, ffn_w_down):
    b, seq, d = x.shape
    assert b == 1 and mix_norm_pre.shape[0] == 2
    pad = (-N_META) % ROW_ALIGN
    tp = pad + N_META + seq
    tm = ROW_TILE
    assert tp % tm == 0
    tm_wide = WIDE_ROW_TILE if tp % WIDE_ROW_TILE == 0 else tm
    top_k = min(TOPK_MAX, seq // TOPK_FRAC)
    h, u_hi, u_lo = _embed_norm(x[0], meta_tokens, mix_norm_pre[0], pad=pad)
    mix = _hybrid_mixer(u_hi, u_lo, hy_w_in[0], dsa_q_norm[0], dsa_w_uq[0], dsa_w_qidx[0],
                        dsa_kv_norm[0], dsa_w_ukv[0], dsa_kidx_norm[0], gdn_conv_w[0],
                        gdn_a_log[0], gdn_dt_bias[0], gdn_o_norm[0], hy_w_out[0],
                        top_k=top_k, pad=pad, tm=tm, tm_wide=tm_wide)
    h, u = _resid_norm(h, mix, mix_norm_post[0], ffn_norm_pre[0], tm=tm)
    f = _ffn(u, ffn_w_gate, ffn_w_up, ffn_w_down, 0, tm=tm, tm_wide=tm_wide)
    h, u = _resid_norm(h, f, ffn_norm_post[0], mix_norm_pre[1], tm=tm)
    mix = _retention_mixer(u, ret_w_in[0], ret_w_out[0], pad=pad, tm=tm, tm_wide=tm_wide)
    h, u = _resid_norm(h, mix, mix_norm_post[1], ffn_norm_pre[1], tm=tm)
    f = _ffn(u, ffn_w_gate, ffn_w_up, ffn_w_down, 1, tm=tm, tm_wide=tm_wide)
    out = _resid_norm(h, f, ffn_norm_post[1], None, tm=ROW_ALIGN, row0=pad + N_META,
                      out_dtype=x.dtype)
    return out[None]
```
